```python
import jax, jax.numpy as jnp
from jax import lax
import numpy as np

D_MODEL = 1024
BATCH = 2
SEQ = 16384
DEPTH = 4
DEC_BATCH = 16
DEC_SEQ = 32
PAST_LEN = 2048

CHUNK = 64
Q_BLOCK = 128
N_MIXERS = 2
N_FOX = (DEPTH + 1) // 2
N_RET = DEPTH // 2
FOX_HEADS = 16
FOX_HEAD_DIM = D_MODEL // FOX_HEADS
FOX_WIDTH = FOX_HEADS * FOX_HEAD_DIM
FOX_IN_COLS = 4 * FOX_WIDTH + FOX_HEADS
RET_HEADS = 8
RET_QK_DIM = D_MODEL // RET_HEADS
RET_V_DIM = 2 * RET_QK_DIM
RET_QK_WIDTH = RET_HEADS * RET_QK_DIM
RET_V_WIDTH = RET_HEADS * RET_V_DIM
RET_IN_COLS = 2 * RET_QK_WIDTH + 2 * RET_V_WIDTH
D_FF = 2816
RMS_EPS = 1e-6
GN_EPS = 1e-5
ROPE_BASE = 10000.0

kernel_name = 'fox_retention_macaron_stream_step'

F32 = jnp.float32


def _rmsnorm(x, gain):
    xf = x.astype(F32)
    y = xf * lax.rsqrt(jnp.mean(xf * xf, axis=-1, keepdims=True) + RMS_EPS)
    return (y * gain.astype(F32)).astype(x.dtype)


def _swiglu(h, w_in, w_out):
    g, u = jnp.split(h @ w_in, 2, axis=-1)
    return (jax.nn.silu(g) * u) @ w_out


def _fox_project(h, w_in, b_f, q_gain, k_gain):
    b, t, _ = h.shape
    proj = h @ w_in
    W = FOX_WIDTH
    q = proj[..., :W].reshape(b, t, FOX_HEADS, FOX_HEAD_DIM)
    k = proj[..., W:2 * W].reshape(b, t, FOX_HEADS, FOX_HEAD_DIM)
    v = proj[..., 2 * W:3 * W].reshape(b, t, FOX_HEADS, FOX_HEAD_DIM)
    og = proj[..., 3 * W:4 * W]
    fl = proj[..., 4 * W:]
    q = _rmsnorm(q, q_gain)
    k = _rmsnorm(k, k_gain)
    logf = jax.nn.log_sigmoid((fl + b_f).astype(F32))
    return q, k, v, og, logf


def _fox_attend_prompt(q, k, v, logf):
    b, s, h, d = q.shape
    nb = s // Q_BLOCK
    cum = jnp.cumsum(logf, axis=1).swapaxes(1, 2)
    kf = k.astype(F32)
    vf = v.astype(F32)
    qb = q.astype(F32).reshape(b, nb, Q_BLOCK, h, d).swapaxes(0, 1)
    cq = cum.reshape(b, h, nb, Q_BLOCK).transpose(2, 0, 1, 3)
    kpos = jnp.arange(s)
    scale = d ** -0.5

    def block(args):
        qi, ci, i = args
        sc = jnp.einsum('bqhd,bkhd->bhqk', qi, kf) * scale + (ci[..., :, None] - cum[:, :, None, :])
        qpos = i * Q_BLOCK + jnp.arange(Q_BLOCK)
        sc = jnp.where(kpos[None, :] <= qpos[:, None], sc, -jnp.inf)
        p = jax.nn.softmax(sc, axis=-1)
        return jnp.einsum('bhqk,bkhd->bqhd', p, vf)

    o = lax.map(block, (qb, cq, jnp.arange(nb)))
    return o.swapaxes(0, 1).reshape(b, s, h, d).astype(v.dtype)


def _fox_attend_sample(q, k, v, logf, k_cache, v_cache, logf_cache):
    b, n, h, d = q.shape
    past = k_cache.shape[1]
    kall = jnp.concatenate([k_cache.astype(F32), k.astype(F32)], axis=1)
    vall = jnp.concatenate([v_cache.astype(F32), v.astype(F32)], axis=1)
    cum = jnp.cumsum(jnp.concatenate([logf_cache.astype(F32), logf], axis=1), axis=1).swapaxes(1, 2)
    sc = (jnp.einsum('bqhd,bkhd->bhqk', q.astype(F32), kall) * (d ** -0.5)
          + (cum[:, :, past:, None] - cum[:, :, None, :]))
    qpos = past + jnp.arange(n)
    kpos = jnp.arange(past + n)
    sc = jnp.where(kpos[None, :] <= qpos[:, None], sc, -jnp.inf)
    p = jax.nn.softmax(sc, axis=-1)
    return jnp.einsum('bhqk,bkhd->bqhd', p, vall).astype(v.dtype)


def _fox_output(o, og, w_out):
    b, t, _, _ = o.shape
    return (o.reshape(b, t, FOX_WIDTH) * jax.nn.sigmoid(og)) @ w_out


def _ret_log_gamma():
    return jnp.log1p(-jnp.exp2(-5.0 - jnp.arange(RET_HEADS, dtype=F32)))


def _rotary(x, pos):
    half = x.shape[-1] // 2
    inv_freq = ROPE_BASE ** (-jnp.linspace(0.0, 1.0, half, dtype=F32))
    ang = pos[:, None] * inv_freq[None, :]
    cos = jnp.cos(ang)[None, :, None, :]
    sin = jnp.sin(ang)[None, :, None, :]
    x1, x2 = x[..., :half], x[..., half:]
    return jnp.concatenate([x1 * cos - x2 * sin, x1 * sin + x2 * cos], axis=-1)


def _ret_project(h, w_in, pos):
    b, t, _ = h.shape
    proj = h @ w_in
    A, B2 = RET_QK_WIDTH, 2 * RET_QK_WIDTH
    q = proj[..., :A].reshape(b, t, RET_HEADS, RET_QK_DIM).astype(F32)
    k = proj[..., A:B2].reshape(b, t, RET_HEADS, RET_QK_DIM).astype(F32)
    v = proj[..., B2:B2 + RET_V_WIDTH].reshape(b, t, RET_HEADS, RET_V_DIM).astype(F32)
    g = proj[..., B2 + RET_V_WIDTH:]
    q = _rotary(q, pos)
    k = _rotary(k, pos) * (RET_QK_DIM ** -0.5)
    return q, k, v, g


def _ret_chunk(state, q, k, v, lg):
    n = q.shape[1]
    idx = jnp.arange(n, dtype=F32)
    dec_in = jnp.exp(idx[:, None] * lg[None, :])
    dec_out = jnp.exp((n - idx)[:, None] * lg[None, :])
    dmat = jnp.exp(jnp.abs(idx[:, None] - idx[None, :])[None] * lg[:, None, None])
    inter = jnp.einsum('bihd,bhde->bihe', q, state) * dec_in[None, :, :, None]
    sc = jnp.einsum('bihd,bjhd->bhij', q, k) * dmat[None]
    intra = jnp.einsum('bhij,bjhe->bihe', sc, v)
    new_state = (state * jnp.exp(n * lg)[None, :, None, None]
                 + jnp.einsum('bjhd,bjhe->bhde', k * dec_out[None, :, :, None], v))
    return new_state, inter + intra


def _ret_prompt(q, k, v, lg):
    b, s, h, _ = q.shape
    nc = s // CHUNK

    def to_chunks(a):
        return a.reshape(b, nc, CHUNK, *a.shape[2:]).swapaxes(0, 1)

    s0 = jnp.zeros((b, RET_HEADS, RET_QK_DIM, RET_V_DIM), F32)

    def step(st, qkv):
        return _ret_chunk(st, qkv[0], qkv[1], qkv[2], lg)

    s_fin, o = lax.scan(step, s0, (to_chunks(q), to_chunks(k), to_chunks(v)))
    return s_fin, o.swapaxes(0, 1).reshape(b, s, h, RET_V_DIM)


def _ret_output(o, g, w_out):
    b, t, _, _ = o.shape
    mu = jnp.mean(o, axis=-1, keepdims=True)
    var = jnp.mean(jnp.square(o - mu), axis=-1, keepdims=True)
    on = ((o - mu) * lax.rsqrt(var + GN_EPS)).reshape(b, t, RET_V_WIDTH).astype(g.dtype)
    return (jax.nn.silu(g) * on) @ w_out


def setup_inputs(seed: int = 0) -> dict:
    key = jax.random.key(seed)
    ks = jax.random.split(key, 17)

    def nrm(k, shape, scale):
        return jax.random.normal(k, shape, F32) * scale

    return {
        'x_prompt': nrm(ks[0], (BATCH, SEQ, D_MODEL), 1.0),
        'x_sample': nrm(ks[1], (DEC_BATCH, DEC_SEQ, D_MODEL), 1.0),
        'cache_fox_k': nrm(ks[2], (N_FOX, DEC_BATCH, PAST_LEN, FOX_HEADS, FOX_HEAD_DIM), 1.0),
        'cache_fox_v': nrm(ks[3], (N_FOX, DEC_BATCH, PAST_LEN, FOX_HEADS, FOX_HEAD_DIM), 1.0),
        'cache_fox_logf': jax.nn.log_sigmoid(jax.random.uniform(ks[4], (N_FOX, DEC_BATCH, PAST_LEN, FOX_HEADS), F32, 1.0, 4.0)),
        'state_ret': nrm(ks[5], (N_RET, DEC_BATCH, RET_HEADS, RET_QK_DIM, RET_V_DIM), 0.5),
        'norm_gains': 1.0 + nrm(ks[6], (DEPTH, 3, D_MODEL), 0.05),
        'w_ffn_in': nrm(ks[7], (DEPTH, 2, D_MODEL, 2 * D_FF), D_MODEL ** -0.5),
        'w_ffn_out': nrm(ks[8], (DEPTH, 2, D_FF, D_MODEL), D_FF ** -0.5),
        'fox_w_in': nrm(ks[9], (N_FOX, D_MODEL, FOX_IN_COLS), D_MODEL ** -0.5),
        'fox_b_f': jax.random.uniform(ks[10], (N_FOX, FOX_HEADS), F32, 1.0, 4.0),
        'fox_q_gain': 1.0 + nrm(ks[11], (N_FOX, FOX_HEAD_DIM), 0.05),
        'fox_k_gain': 1.0 + nrm(ks[12], (N_FOX, FOX_HEAD_DIM), 0.05),
        'fox_w_out': nrm(ks[13], (N_FOX, FOX_WIDTH, D_MODEL), FOX_WIDTH ** -0.5),
        'ret_w_in': nrm(ks[14], (N_RET, D_MODEL, RET_IN_COLS), D_MODEL ** -0.5),
        'ret_w_out': nrm(ks[15], (N_RET, RET_V_WIDTH, D_MODEL), RET_V_WIDTH ** -0.5),
        'final_gain': 1.0 + nrm(ks[16], (D_MODEL,), 0.05),
    }


def reference(x_prompt, x_sample, cache_fox_k, cache_fox_v, cache_fox_logf, state_ret,
              norm_gains, w_ffn_in, w_ffn_out, fox_w_in, fox_b_f, fox_q_gain, fox_k_gain,
              fox_w_out, ret_w_in, ret_w_out, final_gain):
    seq = x_prompt.shape[1]
    n_new = x_sample.shape[1]
    past = cache_fox_k.shape[2]
    pos_p = jnp.arange(seq, dtype=F32)
    pos_s = past + jnp.arange(n_new, dtype=F32)
    lg = _ret_log_gamma()
    xp, xs = x_prompt, x_sample
    fk_p, fv_p, fl_p, rs_p = [], [], [], []
    fk_s, fv_s, fl_s, rs_s = [], [], [], []
    for layer in range(DEPTH):
        g = norm_gains[layer]
        xp = xp + 0.5 * _swiglu(_rmsnorm(xp, g[0]), w_ffn_in[layer, 0], w_ffn_out[layer, 0])
        xs = xs + 0.5 * _swiglu(_rmsnorm(xs, g[0]), w_ffn_in[layer, 0], w_ffn_out[layer, 0])
        hp = _rmsnorm(xp, g[1])
        hs = _rmsnorm(xs, g[1])
        j = layer // N_MIXERS
        if layer % N_MIXERS == 0:
            q, k, v, og, lf = _fox_project(hp, fox_w_in[j], fox_b_f[j], fox_q_gain[j], fox_k_gain[j])
            xp = xp + _fox_output(_fox_attend_prompt(q, k, v, lf), og, fox_w_out[j])
            fk_p.append(k)
            fv_p.append(v)
            fl_p.append(lf.astype(xp.dtype))
            q, k, v, og, lf = _fox_project(hs, fox_w_in[j], fox_b_f[j], fox_q_gain[j], fox_k_gain[j])
            o = _fox_attend_sample(q, k, v, lf, cache_fox_k[j], cache_fox_v[j], cache_fox_logf[j])
            xs = xs + _fox_output(o, og, fox_w_out[j])
            fk_s.append(k.astype(cache_fox_k.dtype))
            fv_s.append(v.astype(cache_fox_v.dtype))
            fl_s.append(lf.astype(cache_fox_logf.dtype))
        else:
            q, k, v, gt = _ret_project(hp, ret_w_in[j], pos_p)
            st, o = _ret_prompt(q, k, v, lg)
            xp = xp + _ret_output(o, gt, ret_w_out[j])
            rs_p.append(st.astype(xp.dtype))
            q, k, v, gt = _ret_project(hs, ret_w_in[j], pos_s)
            st, o = _ret_chunk(state_ret[j].astype(F32), q, k, v, lg)
            xs = xs + _ret_output(o, gt, ret_w_out[j])
            rs_s.append(st.astype(state_ret.dtype))
        xp = xp + 0.5 * _swiglu(_rmsnorm(xp, g[2]), w_ffn_in[layer, 1], w_ffn_out[layer, 1])
        xs = xs + 0.5 * _swiglu(_rmsnorm(xs, g[2]), w_ffn_in[layer, 1], w_ffn_out[layer, 1])
    y_prompt = _rmsnorm(xp, final_gain)
    y_sample = _rmsnorm(xs, final_gain)
    return (y_prompt, y_sample, jnp.stack(fk_p), jnp.stack(fv_p), jnp.stack(fl_p), jnp.stack(rs_p),
            jnp.stack(fk_s), jnp.stack(fv_s), jnp.stack(fl_s), jnp.stack(rs_s))
```

```python
import functools
import math

import numpy as np
import jax
import jax.numpy as jnp
from jax import lax
from jax.experimental import pallas as pl
from jax.experimental.pallas import tpu as pltpu

F32 = jnp.float32
BF16 = jnp.bfloat16

D_MODEL = 1024
CHUNK = 64
FOX_HEADS = 16
FOX_HEAD_DIM = 64
FOX_PAIRS = FOX_HEADS // 2
RET_HEADS = 8
RET_QK_DIM = 128
RET_V_DIM = 256
RET_QK_WIDTH = RET_HEADS * RET_QK_DIM
RET_V_WIDTH = RET_HEADS * RET_V_DIM
D_FF = 2816
RMS_EPS = 1e-6
GN_EPS = 1e-5
ROPE_BASE = 10000.0

LANES = 128
MXU_DIM = 256
TOKEN_TILE = 512
FFN_CHUNK = MXU_DIM
ATTN_TILE = 256
VMEM_LIMIT = 56 * 1024 * 1024
NEG_BIG = -1e30

_NT = (((1,), (1,)), ((), ()))
_TN = (((0,), (0,)), ((), ()))


def _params(*sem):
    return pltpu.CompilerParams(dimension_semantics=sem, vmem_limit_bytes=VMEM_LIMIT)


def _resident(shape):
    nd = len(shape)
    return pl.BlockSpec(shape, lambda *_: (0,) * nd, pipeline_mode=pl.Buffered(1))


def _rmsnorm(x, gain):
    ms = jnp.mean(x * x, axis=-1, keepdims=True)
    return x * lax.rsqrt(ms + RMS_EPS) * gain


def _dot(a, b):
    return jnp.dot(a, b, preferred_element_type=F32)


def _split3(x):
    hi = x.astype(BF16)
    r = x - hi.astype(F32)
    mid = r.astype(BF16)
    lo = (r - mid.astype(F32)).astype(BF16)
    return hi, mid, lo


def _tri(n):
    row = lax.broadcasted_iota(jnp.int32, (n, n), 0)
    col = lax.broadcasted_iota(jnp.int32, (n, n), 1)
    return jnp.where(row >= col, 1.0, 0.0).astype(BF16)


def _cumsum_rows(tri, x):
    hi, mid, lo = _split3(x)
    return _dot(tri, hi) + _dot(tri, mid) + _dot(tri, lo)


def _ffn_body(x_ref, g_ref, win_ref, wout_ref, fg_ref, o_ref, acc_ref, *, final):
    x = x_ref[...]
    h = _rmsnorm(x, g_ref[...]).astype(BF16)
    for c in range(D_FF // FFN_CHUNK):
        lo, hi = c * FFN_CHUNK, (c + 1) * FFN_CHUNK
        gate = _dot(h, win_ref[:, lo:hi])
        up = _dot(h, win_ref[:, D_FF + lo:D_FF + hi])
        act = (gate * jax.nn.sigmoid(gate) * up).astype(BF16)
        part = _dot(act, wout_ref[lo:hi, :])
        if c == 0:
            acc_ref[...] = part
        else:
            acc_ref[...] += part
    y = x + 0.5 * acc_ref[...]
    if final:
        y = _rmsnorm(y, fg_ref[...])
    o_ref[...] = y


def _ffn(x, gain, w_in, w_out, final_gain, final):
    n = x.shape[0]
    tile = pl.BlockSpec((TOKEN_TILE, D_MODEL), lambda i: (i, 0))
    return pl.pallas_call(
        functools.partial(_ffn_body, final=final),
        out_shape=jax.ShapeDtypeStruct((n, D_MODEL), F32),
        grid=(n // TOKEN_TILE,),
        in_specs=[tile, _resident((1, D_MODEL)), _resident((D_MODEL, 2 * D_FF)),
                  _resident((D_FF, D_MODEL)), _resident((1, D_MODEL))],
        out_specs=tile,
        scratch_shapes=[pltpu.VMEM((TOKEN_TILE, D_MODEL), F32)],
        compiler_params=_params("parallel"),
        name="ffn_final" if final else "ffn",
    )(x, gain, w_in, w_out, final_gain)


def _proj_res_body(y_ref, w_ref, x_ref, o_ref):
    o_ref[...] = x_ref[...] + _dot(y_ref[...], w_ref[...])


def _proj_res(y, w, x):
    n, k = y.shape
    xt = pl.BlockSpec((TOKEN_TILE, D_MODEL), lambda i: (i, 0))
    return pl.pallas_call(
        _proj_res_body,
        out_shape=jax.ShapeDtypeStruct((n, D_MODEL), F32),
        grid=(n // TOKEN_TILE,),
        in_specs=[pl.BlockSpec((TOKEN_TILE, k), lambda i: (i, 0)), _resident((k, D_MODEL)), xt],
        out_specs=xt,
        compiler_params=_params("parallel"),
        name="proj_res",
    )(y, w, x)


def _head_rmsnorm(t, grp, gain):
    outs = []
    for c in range(D_MODEL // MXU_DIM):
        blk = t[:, c * MXU_DIM:(c + 1) * MXU_DIM]
        sq = blk * blk
        hi = sq.astype(BF16)
        lo = (sq - hi.astype(F32)).astype(BF16)
        ss = _dot(hi, grp) + _dot(lo, grp)
        outs.append(blk * lax.rsqrt(ss * (1.0 / FOX_HEAD_DIM) + RMS_EPS))
    return jnp.concatenate(outs, axis=1) * gain


def _fox_proj_body(x_ref, g_ref, w_ref, wf_ref, bf_ref, qg_ref, kg_ref, grp_ref,
                   q_o, kf_o, kb_o, vf_o, vb_o, gate_o, lf_o, d_o, dt_o, carry_ref, *, tiles_per_seq):
    i = pl.program_id(0)
    h = _rmsnorm(x_ref[...], g_ref[...]).astype(BF16)
    grp = grp_ref[...]
    w = D_MODEL
    q = _head_rmsnorm(_dot(h, w_ref[:, 0:w]), grp, qg_ref[...])
    q_o[...] = (q * (FOX_HEAD_DIM ** -0.5)).astype(BF16)
    k = _head_rmsnorm(_dot(h, w_ref[:, w:2 * w]), grp, kg_ref[...])
    kf_o[...] = k
    kb_o[...] = k.astype(BF16)
    v = _dot(h, w_ref[:, 2 * w:3 * w])
    vf_o[...] = v
    vb_o[...] = v.astype(BF16)
    gate_o[...] = jax.nn.sigmoid(_dot(h, w_ref[:, 3 * w:4 * w])).astype(BF16)
    z = _dot(h, wf_ref[...]) + bf_ref[...]
    logf = jnp.minimum(z, 0.0) - jnp.log1p(jnp.exp(-jnp.abs(z)))
    lf_o[...] = logf

    @pl.when(i % tiles_per_seq == 0)
    def _():
        carry_ref[...] = jnp.zeros_like(carry_ref)

    d = _cumsum_rows(_tri(TOKEN_TILE), logf) + carry_ref[0:1, :]
    carry_ref[...] = jnp.broadcast_to(d[TOKEN_TILE - 1:TOKEN_TILE, :], carry_ref.shape)
    d_o[...] = d
    dt_o[...] = d.T[0:FOX_HEADS, :]


def _fox_proj(x, gain, w, wf, bf, qg, kg, grp, tiles_per_seq):
    n = x.shape[0]
    tile = pl.BlockSpec((TOKEN_TILE, D_MODEL), lambda i: (i, 0))
    nar = pl.BlockSpec((TOKEN_TILE, LANES), lambda i: (i, 0))
    wide = lambda dt: jax.ShapeDtypeStruct((n, D_MODEL), dt)
    return pl.pallas_call(
        functools.partial(_fox_proj_body, tiles_per_seq=tiles_per_seq),
        out_shape=(wide(BF16), wide(F32), wide(BF16), wide(F32), wide(BF16), wide(BF16),
                   jax.ShapeDtypeStruct((n, LANES), F32), jax.ShapeDtypeStruct((n, LANES), F32),
                   jax.ShapeDtypeStruct((FOX_HEADS, n), F32)),
        grid=(n // TOKEN_TILE,),
        in_specs=[tile, _resident((1, D_MODEL)), _resident((D_MODEL, 4 * D_MODEL)),
                  _resident((D_MODEL, LANES)), _resident((1, LANES)), _resident((1, D_MODEL)),
                  _resident((1, D_MODEL)), _resident((MXU_DIM, MXU_DIM))],
        out_specs=(tile, tile, tile, tile, tile, tile, nar, nar,
                   pl.BlockSpec((FOX_HEADS, TOKEN_TILE), lambda i: (0, i))),
        scratch_shapes=[pltpu.VMEM((8, LANES), F32)],
        compiler_params=_params("arbitrary"),
        name="fox_proj",
    )(x, gain, w, wf, bf, qg, kg, grp)


def _pair_split(q2):
    lane = lax.broadcasted_iota(jnp.int32, q2.shape, 1)
    left = lane < FOX_HEAD_DIM
    zero = jnp.zeros_like(q2)
    return left, (jnp.where(left, q2, zero), jnp.where(left, zero, q2))


def _fox_attn_body(q_ref, k_ref, v_ref, gate_ref, dq_ref, dk_ref, o_ref):
    t = ATTN_TILE
    i = pl.program_id(2)
    left, qs = _pair_split(q_ref[...])
    dq = dq_ref[0]
    dqs = (dq[:, 0:1], dq[:, 1:2])
    row = lax.broadcasted_iota(jnp.int32, (t, t), 0)
    col = lax.broadcasted_iota(jnp.int32, (t, t), 1)
    causal = col <= row

    def step(j, carry, masked):
        off = pl.multiple_of(j * t, t)
        k2 = k_ref[pl.ds(off, t), :]
        v2 = v_ref[pl.ds(off, t), :]
        dk = dk_ref[0, :, pl.ds(off, t)]
        out = []
        for hh in range(2):
            m, l, acc = carry[hh]
            s = lax.dot_general(qs[hh], k2, _NT, preferred_element_type=F32)
            s = s + (dqs[hh] - dk[hh:hh + 1, :])
            if masked:
                s = jnp.where(causal, s, NEG_BIG)
            m_new = jnp.maximum(m, jnp.max(s, axis=1, keepdims=True))
            alpha = jnp.exp(m - m_new)
            p = jnp.exp(s - m_new)
            l_new = alpha * l + jnp.sum(p, axis=1, keepdims=True)
            acc_new = alpha * acc + _dot(p.astype(BF16), v2)
            out.append((m_new, l_new, acc_new))
        return tuple(out)

    init_one = (jnp.full((t, 1), NEG_BIG, F32), jnp.zeros((t, 1), F32), jnp.zeros((t, LANES), F32))
    carry = step(i, (init_one, init_one), True)
    carry = lax.fori_loop(0, i, lambda j, c: step(j, c, False), carry)
    (_, la, acca), (_, lb, accb) = carry
    o = jnp.where(left, acca / la, accb / lb)
    o_ref[...] = (o * gate_ref[...].astype(F32)).astype(BF16)


def _fox_attn_prompt(q, kb, vb, gate, dq, dk, batch, seq):
    n = q.shape[0]
    nq = seq // ATTN_TILE
    qspec = pl.BlockSpec((ATTN_TILE, LANES), lambda b, p, i: (b * nq + i, p))
    kvspec = pl.BlockSpec((seq, LANES), lambda b, p, i: (b, p))
    return pl.pallas_call(
        _fox_attn_body,
        out_shape=jax.ShapeDtypeStruct((n, D_MODEL), BF16),
        grid=(batch, FOX_PAIRS, nq),
        in_specs=[qspec, kvspec, kvspec, qspec,
                  pl.BlockSpec((1, ATTN_TILE, 2), lambda b, p, i: (p, b * nq + i, 0)),
                  pl.BlockSpec((1, 2, seq), lambda b, p, i: (p, 0, b))],
        out_specs=qspec,
        compiler_params=_params("parallel", "parallel", "arbitrary"),
        name="fox_attn_prompt",
    )(q, kb, vb, gate, dq, dk)


def _sample_bias_body(cl_ref, ln_ref, rt_o, ncol_o, nt_o):
    past = cl_ref.shape[1]
    n_new = ln_ref.shape[0]
    tri = _tri(MXU_DIM)
    carry = jnp.zeros((1, LANES), F32)
    prefix = []
    for b in range(past // MXU_DIM):
        p = _cumsum_rows(tri, cl_ref[0, b * MXU_DIM:(b + 1) * MXU_DIM, :]) + carry
        carry = p[MXU_DIM - 1:MXU_DIM, :]
        prefix.append(p)
    for b, p in enumerate(prefix):
        rt_o[0, :, b * MXU_DIM:(b + 1) * MXU_DIM] = (carry - p).T[0:FOX_HEADS, :]
    ln = jnp.concatenate([ln_ref[...], jnp.zeros((LANES - n_new, LANES), F32)], axis=0)
    nn = _cumsum_rows(_tri(LANES), ln)
    ncol_o[0] = nn[0:n_new, :]
    nt_o[0] = nn.T[0:FOX_HEADS, 0:n_new]


def _sample_bias(cl_pad, lf, n_new, row0):
    dec_batch, past, _ = cl_pad.shape
    blk0 = row0 // n_new
    return pl.pallas_call(
        _sample_bias_body,
        out_shape=(jax.ShapeDtypeStruct((dec_batch, FOX_HEADS, past), F32),
                   jax.ShapeDtypeStruct((dec_batch, n_new, LANES), F32),
                   jax.ShapeDtypeStruct((dec_batch, FOX_HEADS, n_new), F32)),
        grid=(dec_batch,),
        in_specs=[pl.BlockSpec((1, past, LANES), lambda b: (b, 0, 0)),
                  pl.BlockSpec((n_new, LANES), lambda b: (blk0 + b, 0))],
        out_specs=(pl.BlockSpec((1, FOX_HEADS, past), lambda b: (b, 0, 0)),
                   pl.BlockSpec((1, n_new, LANES), lambda b: (b, 0, 0)),
                   pl.BlockSpec((1, FOX_HEADS, n_new), lambda b: (b, 0, 0))),
        compiler_params=_params("parallel"),
        name="fox_sample_bias",
    )(cl_pad, lf)


def _fox_sample_body(q_ref, kn_ref, vn_ref, gate_ref, kc_ref, vc_ref, rt_ref, nc_ref, nt_ref,
                     y_any, o_ref):
    del y_any
    n_new = q_ref.shape[0]
    left, qs = _pair_split(q_ref[...])
    kc = kc_ref[0].astype(BF16)
    vc = vc_ref[0].astype(BF16)
    kn = kn_ref[...]
    vn = vn_ref[...]
    rt = rt_ref[0, 0]
    nc = nc_ref[0, 0]
    nt = nt_ref[0, 0]
    row = lax.broadcasted_iota(jnp.int32, (n_new, n_new), 0)
    col = lax.broadcasted_iota(jnp.int32, (n_new, n_new), 1)
    outs = []
    for hh in range(2):
        nq = nc[:, hh:hh + 1]
        s1 = lax.dot_general(qs[hh], kc, _NT, preferred_element_type=F32) + (nq + rt[hh:hh + 1, :])
        s2 = lax.dot_general(qs[hh], kn, _NT, preferred_element_type=F32) + (nq - nt[hh:hh + 1, :])
        s2 = jnp.where(col <= row, s2, NEG_BIG)
        m = jnp.maximum(jnp.max(s1, axis=1, keepdims=True), jnp.max(s2, axis=1, keepdims=True))
        p1 = jnp.exp(s1 - m)
        p2 = jnp.exp(s2 - m)
        l = jnp.sum(p1, axis=1, keepdims=True) + jnp.sum(p2, axis=1, keepdims=True)
        outs.append((_dot(p1.astype(BF16), vc) + _dot(p2.astype(BF16), vn)) / l)
    o = jnp.where(left, outs[0], outs[1])
    o_ref[...] = (o * gate_ref[...].astype(F32)).astype(BF16)


def _fox_attn_sample(q, kb, vb, gate, kc, vc, rt, nc, nt, y_all, n_new, row0):
    dec_batch, past, _ = kc.shape
    blk0 = row0 // n_new
    new = pl.BlockSpec((n_new, LANES), lambda b, p: (blk0 + b, p))
    cache = pl.BlockSpec((1, past, LANES), lambda b, p: (b, 0, p))
    return pl.pallas_call(
        _fox_sample_body,
        out_shape=jax.ShapeDtypeStruct(y_all.shape, BF16),
        grid=(dec_batch, FOX_PAIRS),
        in_specs=[new, new, new, new, cache, cache,
                  pl.BlockSpec((1, 1, 2, past), lambda b, p: (b, p, 0, 0)),
                  pl.BlockSpec((1, 1, n_new, 2), lambda b, p: (b, p, 0, 0)),
                  pl.BlockSpec((1, 1, 2, n_new), lambda b, p: (b, p, 0, 0)),
                  pl.BlockSpec(memory_space=pl.ANY)],
        out_specs=new,
        input_output_aliases={9: 0},
        compiler_params=_params("parallel", "parallel"),
        name="fox_attn_sample",
    )(q, kb, vb, gate, kc, vc, rt, nc, nt, y_all)


def _ret_log_gammas():
    return [float(v) for v in np.log1p(-np.exp2(-5.0 - np.arange(RET_HEADS, dtype=np.float32)))]


def _ret_proj_body(x_ref, g_ref, w_ref, cos_ref, sin_ref, q_o, k_o, v_o, sg_o):
    h = _rmsnorm(x_ref[...], g_ref[...]).astype(BF16)
    cos = cos_ref[...]
    sin = sin_ref[...]
    a, b2 = RET_QK_WIDTH, 2 * RET_QK_WIDTH

    def rotary(t, scale):
        outs = []
        for hd in range(RET_HEADS):
            th = t[:, hd * RET_QK_DIM:(hd + 1) * RET_QK_DIM]
            outs.append((th * cos + pltpu.roll(th, RET_QK_DIM // 2, 1) * sin) * scale)
        return jnp.concatenate(outs, axis=1).astype(BF16)

    q_o[...] = rotary(_dot(h, w_ref[:, 0:a]), 1.0)
    k_o[...] = rotary(_dot(h, w_ref[:, a:b2]), RET_QK_DIM ** -0.5)
    v_o[...] = _dot(h, w_ref[:, b2:b2 + RET_V_WIDTH]).astype(BF16)
    g = _dot(h, w_ref[:, b2 + RET_V_WIDTH:b2 + 2 * RET_V_WIDTH])
    sg_o[...] = (g * jax.nn.sigmoid(g)).astype(BF16)


def _ret_proj(x, gain, w, cos, sin):
    n = x.shape[0]
    tile = lambda width: pl.BlockSpec((TOKEN_TILE, width), lambda i: (i, 0))
    return pl.pallas_call(
        _ret_proj_body,
        out_shape=(jax.ShapeDtypeStruct((n, RET_QK_WIDTH), BF16), jax.ShapeDtypeStruct((n, RET_QK_WIDTH), BF16),
                   jax.ShapeDtypeStruct((n, RET_V_WIDTH), BF16), jax.ShapeDtypeStruct((n, RET_V_WIDTH), BF16)),
        grid=(n // TOKEN_TILE,),
        in_specs=[tile(D_MODEL), _resident((1, D_MODEL)), _resident(w.shape), tile(LANES), tile(LANES)],
        out_specs=(tile(RET_QK_WIDTH), tile(RET_QK_WIDTH), tile(RET_V_WIDTH), tile(RET_V_WIDTH)),
        compiler_params=_params("parallel"),
        name="ret_proj",
    )(x, gain, w, cos, sin)


def _ret_decay_matrix(t, lg):
    row = lax.broadcasted_iota(jnp.int32, (t, t), 0)
    col = lax.broadcasted_iota(jnp.int32, (t, t), 1)
    dist = jnp.abs(row - col).astype(F32)
    visible = (col // CHUNK) <= (row // CHUNK)
    return jnp.where(visible, jnp.exp(dist * lg), 0.0)


def _ret_tile(q_ref, k_ref, v_ref, sg_ref, y_ref, state_in, state_out, decay, t):
    lgs = _ret_log_gammas()
    pos = lax.broadcasted_iota(jnp.int32, (t, 1), 0).astype(F32)
    for hd in range(RET_HEADS):
        lg = lgs[hd]
        qs = slice(hd * RET_QK_DIM, (hd + 1) * RET_QK_DIM)
        vs = slice(hd * RET_V_DIM, (hd + 1) * RET_V_DIM)
        qh = q_ref[:, qs]
        kh = k_ref[:, qs]
        vh = v_ref[:, vs]
        state = state_in(hd)
        sc = lax.dot_general(qh, kh, _NT, preferred_element_type=F32) * decay(hd)
        o = _dot(sc.astype(BF16), vh) + _dot(qh, state.astype(BF16)) * jnp.exp(pos * lg)
        kd = (kh.astype(F32) * jnp.exp((t - pos) * lg)).astype(BF16)
        state_out(hd, state * math.exp(t * lg) + lax.dot_general(kd, vh, _TN, preferred_element_type=F32))
        mu = jnp.mean(o, axis=-1, keepdims=True)
        dev = o - mu
        var = jnp.mean(dev * dev, axis=-1, keepdims=True)
        y_ref[:, vs] = (sg_ref[:, vs].astype(F32) * (dev * lax.rsqrt(var + GN_EPS))).astype(BF16)


def _ret_prompt_body(q_ref, k_ref, v_ref, sg_ref, y_ref, st_ref, decay_ref):
    t = TOKEN_TILE

    @pl.when((pl.program_id(0) == 0) & (pl.program_id(1) == 0))
    def _():
        for hd, lg in enumerate(_ret_log_gammas()):
            decay_ref[hd] = _ret_decay_matrix(t, lg)

    @pl.when(pl.program_id(1) == 0)
    def _():
        st_ref[...] = jnp.zeros_like(st_ref)

    def state_out(hd, val):
        st_ref[0, hd] = val

    _ret_tile(q_ref, k_ref, v_ref, sg_ref, y_ref, lambda hd: st_ref[0, hd], state_out,
              lambda hd: decay_ref[hd], t)


def _ret_prompt(q, k, v, sg, batch, seq):
    nt = seq // TOKEN_TILE
    tile = lambda width: pl.BlockSpec((TOKEN_TILE, width), lambda b, i: (b * nt + i, 0))
    return pl.pallas_call(
        _ret_prompt_body,
        out_shape=(jax.ShapeDtypeStruct((q.shape[0], RET_V_WIDTH), BF16),
                   jax.ShapeDtypeStruct((batch, RET_HEADS, RET_QK_DIM, RET_V_DIM), F32)),
        grid=(batch, nt),
        in_specs=[tile(RET_QK_WIDTH), tile(RET_QK_WIDTH), tile(RET_V_WIDTH), tile(RET_V_WIDTH)],
        out_specs=(tile(RET_V_WIDTH),
                   pl.BlockSpec((1, RET_HEADS, RET_QK_DIM, RET_V_DIM), lambda b, i: (b, 0, 0, 0))),
        scratch_shapes=[pltpu.VMEM((RET_HEADS, TOKEN_TILE, TOKEN_TILE), F32)],
        compiler_params=_params("arbitrary", "arbitrary"),
        name="ret_prompt",
    )(q, k, v, sg)


def _ret_sample_body(q_ref, k_ref, v_ref, sg_ref, st_in_ref, y_any, y_ref, st_out_ref):
    del y_any
    t = q_ref.shape[0]
    lgs = _ret_log_gammas()

    def state_out(hd, val):
        st_out_ref[0, hd] = val

    _ret_tile(q_ref, k_ref, v_ref, sg_ref, y_ref, lambda hd: st_in_ref[0, hd], state_out,
              lambda hd: _ret_decay_matrix(t, lgs[hd]), t)


def _ret_sample(q, k, v, sg, state, y_all, n_new, row0):
    dec_batch = state.shape[0]
    blk0 = row0 // n_new
    tile = lambda width: pl.BlockSpec((n_new, width), lambda b: (blk0 + b, 0))
    st = pl.BlockSpec((1, RET_HEADS, RET_QK_DIM, RET_V_DIM), lambda b: (b, 0, 0, 0))
    return pl.pallas_call(
        _ret_sample_body,
        out_shape=(jax.ShapeDtypeStruct(y_all.shape, BF16), jax.ShapeDtypeStruct(state.shape, F32)),
        grid=(dec_batch,),
        in_specs=[tile(RET_QK_WIDTH), tile(RET_QK_WIDTH), tile(RET_V_WIDTH), tile(RET_V_WIDTH), st,
                  pl.BlockSpec(memory_space=pl.ANY)],
        out_specs=(tile(RET_V_WIDTH), st),
        input_output_aliases={5: 0},
        compiler_params=_params("parallel"),
        name="ret_sample",
    )(q, k, v, sg, state, y_all)


def _rotary_tables(pos):
    half = RET_QK_DIM // 2
    inv_freq = ROPE_BASE ** (-jnp.linspace(0.0, 1.0, half, dtype=F32))
    ang = pos[:, None] * inv_freq[None, :]
    cos, sin = jnp.cos(ang), jnp.sin(ang)
    return jnp.concatenate([cos, cos], axis=1), jnp.concatenate([-sin, sin], axis=1)


def _head_tiled(gain):
    return jnp.tile(gain.astype(F32), FOX_HEADS).reshape(1, D_MODEL)


def kernel(x_prompt, x_sample, cache_fox_k, cache_fox_v, cache_fox_logf, state_ret, norm_gains, w_ffn_in,
           w_ffn_out, fox_w_in, fox_b_f, fox_q_gain, fox_k_gain, fox_w_out, ret_w_in, ret_w_out, final_gain):
    batch, seq, _ = x_prompt.shape
    dec_batch, n_new, _ = x_sample.shape
    past = cache_fox_k.shape[2]
    depth = norm_gains.shape[0]
    n_prompt = batch * seq
    n_sample = dec_batch * n_new
    assert seq % TOKEN_TILE == 0 and n_sample % TOKEN_TILE == 0 and TOKEN_TILE % n_new == 0
    assert seq % ATTN_TILE == 0 and past % MXU_DIM == 0 and TOKEN_TILE % CHUNK == 0 and n_new <= CHUNK

    x = jnp.concatenate([x_prompt.reshape(n_prompt, D_MODEL), x_sample.reshape(n_sample, D_MODEL)], axis=0)
    pos = jnp.concatenate([jnp.tile(jnp.arange(seq, dtype=F32), batch),
                           jnp.tile(past + jnp.arange(n_new, dtype=F32), dec_batch)])
    cos, sin = _rotary_tables(pos)
    grp_idx = np.arange(MXU_DIM) // FOX_HEAD_DIM
    grp = jnp.asarray(grp_idx[:, None] == grp_idx[None, :], dtype=BF16)
    fgain = final_gain.reshape(1, D_MODEL)

    fk_p, fv_p, fl_p, rs_p, fk_s, fv_s, fl_s, rs_s = [], [], [], [], [], [], [], []
    for layer in range(depth):
        g = norm_gains[layer].reshape(3, 1, D_MODEL)
        x = _ffn(x, g[0], w_ffn_in[layer, 0].astype(BF16), w_ffn_out[layer, 0].astype(BF16), fgain, False)
        j = layer // 2
        if layer % 2 == 0:
            w = fox_w_in[j]
            wf = jnp.pad(w[:, 4 * D_MODEL:], ((0, 0), (0, LANES - FOX_HEADS))).astype(BF16)
            bf = jnp.pad(fox_b_f[j], (0, LANES - FOX_HEADS)).reshape(1, LANES)
            q, kf, kb, vf, vb, gate, lf, d, dt = _fox_proj(
                x, g[1], w[:, :4 * D_MODEL].astype(BF16), wf, bf, _head_tiled(fox_q_gain[j]),
                _head_tiled(fox_k_gain[j]), grp, seq // TOKEN_TILE)
            dq = d[:n_prompt, :FOX_HEADS].reshape(n_prompt, FOX_PAIRS, 2).transpose(1, 0, 2)
            dk = dt.reshape(FOX_PAIRS, 2, -1)
            y = _fox_attn_prompt(q, kb, vb, gate, dq, dk, batch, seq)
            cl_pad = jnp.pad(cache_fox_logf[j], ((0, 0), (0, 0), (0, LANES - FOX_HEADS)))
            rt, nc, nt = _sample_bias(cl_pad, lf, n_new, n_prompt)
            y = _fox_attn_sample(
                q, kb, vb, gate, cache_fox_k[j].reshape(dec_batch, past, D_MODEL),
                cache_fox_v[j].reshape(dec_batch, past, D_MODEL), rt.reshape(dec_batch, FOX_PAIRS, 2, past),
                nc[:, :, :FOX_HEADS].reshape(dec_batch, n_new, FOX_PAIRS, 2).transpose(0, 2, 1, 3),
                nt.reshape(dec_batch, FOX_PAIRS, 2, n_new), y, n_new, n_prompt)
            x = _proj_res(y, fox_w_out[j].astype(BF16), x)
            fk_p.append(kf[:n_prompt].reshape(batch, seq, FOX_HEADS, FOX_HEAD_DIM))
            fv_p.append(vf[:n_prompt].reshape(batch, seq, FOX_HEADS, FOX_HEAD_DIM))
            fl_p.append(lf[:n_prompt, :FOX_HEADS].reshape(batch, seq, FOX_HEADS))
            fk_s.append(kf[n_prompt:].reshape(dec_batch, n_new, FOX_HEADS, FOX_HEAD_DIM))
            fv_s.append(vf[n_prompt:].reshape(dec_batch, n_new, FOX_HEADS, FOX_HEAD_DIM))
            fl_s.append(lf[n_prompt:, :FOX_HEADS].reshape(dec_batch, n_new, FOX_HEADS))
        else:
            q, k, v, sg = _ret_proj(x, g[1], ret_w_in[j].astype(BF16), cos, sin)
            y, st_p = _ret_prompt(q, k, v, sg, batch, seq)
            y, st_s = _ret_sample(q, k, v, sg, state_ret[j], y, n_new, n_prompt)
            x = _proj_res(y, ret_w_out[j].astype(BF16), x)
            rs_p.append(st_p)
            rs_s.append(st_s)
        x = _ffn(x, g[2], w_ffn_in[layer, 1].astype(BF16), w_ffn_out[layer, 1].astype(BF16), fgain,
                 layer == depth - 1)
    return (x[:n_prompt].reshape(batch, seq, D_MODEL), x[n_prompt:].reshape(dec_batch, n_new, D_MODEL),
            jnp.stack(fk_p), jnp.stack(fv_p), jnp.stack(fl_p), jnp.stack(rs_p),
            jnp.stack(fk_s), jnp.stack(fv_s), jnp.stack(fl_s), jnp.stack(rs_s))
```

```python
import functools
import math

import numpy as np
import jax
import jax.numpy as jnp
from jax import lax
from jax.experimental import pallas as pl
from jax.experimental.pallas import tpu as pltpu

F32 = jnp.float32
BF16 = jnp.bfloat16

D_MODEL = 1024
CHUNK = 64
FOX_HEADS = 16
FOX_HEAD_DIM = 64
FOX_PAIRS = FOX_HEADS // 2
RET_HEADS = 8
RET_QK_DIM = 128
RET_V_DIM = 256
RET_QK_WIDTH = RET_HEADS * RET_QK_DIM
RET_V_WIDTH = RET_HEADS * RET_V_DIM
D_FF = 2816
RMS_EPS = 1e-6
GN_EPS = 1e-5
ROPE_BASE = 10000.0

LANES = 128
MXU_DIM = 256
TOKEN_TILE = 512
FFN_CHUNK = MXU_DIM
ATTN_TILE = 256
VMEM_LIMIT = 56 * 1024 * 1024
NEG_BIG = -1e30
LOG2E = 1.4426950408889634
FOX_Q_SCALE = FOX_HEAD_DIM ** -0.5
AUG_PIECES = 3
SKIP_MARGIN = 50.0

_NT = (((1,), (1,)), ((), ()))
_TN = (((0,), (0,)), ((), ()))


def _params(*sem):
    return pltpu.CompilerParams(dimension_semantics=sem, vmem_limit_bytes=VMEM_LIMIT)


def _resident(shape):
    nd = len(shape)
    return pl.BlockSpec(shape, lambda *_: (0,) * nd, pipeline_mode=pl.Buffered(1))


def _rmsnorm(x, gain):
    ms = jnp.mean(x * x, axis=-1, keepdims=True)
    return x * lax.rsqrt(ms + RMS_EPS) * gain


def _dot(a, b):
    return jnp.dot(a, b, preferred_element_type=F32)


def _split3(x):
    hi = x.astype(BF16)
    r = x - hi.astype(F32)
    mid = r.astype(BF16)
    lo = (r - mid.astype(F32)).astype(BF16)
    return hi, mid, lo


def _tri(n, upper=False):
    row = lax.broadcasted_iota(jnp.int32, (n, n), 0)
    col = lax.broadcasted_iota(jnp.int32, (n, n), 1)
    return jnp.where((row <= col) if upper else (row >= col), 1.0, 0.0).astype(BF16)


def _cumsum_rows(tri, x):
    hi, mid, lo = _split3(x)
    return _dot(tri, hi) + _dot(tri, mid) + _dot(tri, lo)


def _cumsum_cols(tri_upper, x):
    hi, mid, lo = _split3(x)
    return _dot(hi, tri_upper) + _dot(mid, tri_upper) + _dot(lo, tri_upper)


def _log_sigmoid(z):
    return jnp.minimum(z, 0.0) - jnp.log1p(jnp.exp(-jnp.abs(z)))


def _ffn_body(x_ref, g_ref, win_ref, wout_ref, fg_ref, o_ref, acc_ref, *, final):
    x = x_ref[...]
    h = _rmsnorm(x, g_ref[...]).astype(BF16)
    for c in range(D_FF // FFN_CHUNK):
        lo, hi = c * FFN_CHUNK, (c + 1) * FFN_CHUNK
        gate = _dot(h, win_ref[:, lo:hi])
        up = _dot(h, win_ref[:, D_FF + lo:D_FF + hi])
        act = (gate * jax.nn.sigmoid(gate) * up).astype(BF16)
        part = _dot(act, wout_ref[lo:hi, :])
        if c == 0:
            acc_ref[...] = part
        else:
            acc_ref[...] += part
    y = x + 0.5 * acc_ref[...]
    if final:
        y = _rmsnorm(y, fg_ref[...])
    o_ref[...] = y


def _ffn(x, gain, w_in, w_out, final_gain, final):
    n = x.shape[0]
    tile = pl.BlockSpec((TOKEN_TILE, D_MODEL), lambda i: (i, 0))
    return pl.pallas_call(
        functools.partial(_ffn_body, final=final),
        out_shape=jax.ShapeDtypeStruct((n, D_MODEL), F32),
        grid=(n // TOKEN_TILE,),
        in_specs=[tile, _resident((1, D_MODEL)), _resident((D_MODEL, 2 * D_FF)),
                  _resident((D_FF, D_MODEL)), _resident((1, D_MODEL))],
        out_specs=tile,
        scratch_shapes=[pltpu.VMEM((TOKEN_TILE, D_MODEL), F32)],
        compiler_params=_params("parallel"),
        name="ffn_final" if final else "ffn",
    )(x, gain, w_in, w_out, final_gain)


def _proj_res_body(y_ref, w_ref, x_ref, o_ref):
    o_ref[...] = x_ref[...] + _dot(y_ref[...], w_ref[...])


def _proj_res(y, w, x):
    n, k = y.shape
    xt = pl.BlockSpec((TOKEN_TILE, D_MODEL), lambda i: (i, 0))
    return pl.pallas_call(
        _proj_res_body,
        out_shape=jax.ShapeDtypeStruct((n, D_MODEL), F32),
        grid=(n // TOKEN_TILE,),
        in_specs=[pl.BlockSpec((TOKEN_TILE, k), lambda i: (i, 0)), _resident((k, D_MODEL)), xt],
        out_specs=xt,
        compiler_params=_params("parallel"),
        name="proj_res",
    )(y, w, x)


def _aug_tables():
    sel_k = np.zeros((AUG_PIECES * LANES, D_MODEL), np.float32)
    sel_q = np.zeros((AUG_PIECES * LANES, D_MODEL), np.float32)
    const_k = np.zeros((1, D_MODEL), np.float32)
    const_q = np.zeros((1, D_MODEL), np.float32)
    for p in range(FOX_PAIRS):
        for hh in range(2):
            for piece in range(AUG_PIECES):
                src = piece * LANES + 2 * p + hh
                sel_k[src, p * LANES + 3 * hh + piece] = 1.0
                sel_q[src, p * LANES + 6 + 3 * hh + piece] = 1.0
                const_q[0, p * LANES + 3 * hh + piece] = -1.0
                const_k[0, p * LANES + 6 + 3 * hh + piece] = 1.0
    return sel_k, sel_q, const_k, const_q


def _head_rmsnorm(t, grp, gain):
    outs = []
    for c in range(D_MODEL // MXU_DIM):
        blk = t[:, c * MXU_DIM:(c + 1) * MXU_DIM]
        sq = blk * blk
        hi = sq.astype(BF16)
        lo = (sq - hi.astype(F32)).astype(BF16)
        ss = _dot(hi, grp) + _dot(lo, grp)
        outs.append(blk * lax.rsqrt(ss * (1.0 / FOX_HEAD_DIM) + RMS_EPS))
    return jnp.concatenate(outs, axis=1) * gain


def _head_rmsnorm_t(t, grp, gain):
    outs = []
    for c in range(D_MODEL // MXU_DIM):
        blk = t[c * MXU_DIM:(c + 1) * MXU_DIM, :]
        sq = blk * blk
        hi = sq.astype(BF16)
        lo = (sq - hi.astype(F32)).astype(BF16)
        ss = _dot(grp, hi) + _dot(grp, lo)
        outs.append(blk * lax.rsqrt(ss * (1.0 / FOX_HEAD_DIM) + RMS_EPS))
    return jnp.concatenate(outs, axis=0) * gain


def _fox_proj_body(x_ref, g_ref, w_ref, wf_ref, bf_ref, qg_ref, kg_ref, grp_ref, selk_ref, ck_ref,
                   wt_ref, wft_ref, bft_ref, qgt_ref, selqt_ref, cqt_ref,
                   qt_o, aqt_o, vt_o, kb_o, ak_o, kf_o, vf_o, gate_o, lf_o, d_o, qs_o, vs_o,
                   carry_ref, carryt_ref, *, tiles_per_seq):
    i = pl.program_id(0)
    tm, w = TOKEN_TILE, D_MODEL
    hf = _rmsnorm(x_ref[...], g_ref[...])
    h = hf.astype(BF16)
    ht = hf.T.astype(BF16)
    grp = grp_ref[...]

    @pl.when(i % tiles_per_seq == 0)
    def _():
        carry_ref[...] = jnp.zeros_like(carry_ref)
        carryt_ref[...] = jnp.zeros_like(carryt_ref)

    k = _head_rmsnorm(_dot(h, w_ref[:, w:2 * w]), grp, kg_ref[...])
    kf_o[...] = k
    kb_o[...] = k.astype(BF16)
    v = _dot(h, w_ref[:, 2 * w:3 * w])
    vf_o[...] = v
    gate_o[...] = jax.nn.sigmoid(_dot(h, w_ref[:, 3 * w:4 * w])).astype(BF16)
    logf = _log_sigmoid(_dot(h, wf_ref[...]) + bf_ref[...])
    lf_o[...] = logf
    d = _cumsum_rows(_tri(tm), logf) + carry_ref[0:1, :]
    carry_ref[...] = jnp.broadcast_to(d[tm - 1:tm, :], carry_ref.shape)
    d_o[...] = d
    pieces = jnp.concatenate(_split3(d * LOG2E), axis=1)
    ak_o[...] = (_dot(pieces, selk_ref[...]) + ck_ref[...]).astype(BF16)

    qt = _head_rmsnorm_t(_dot(wt_ref[0:w, :], ht), grp, qgt_ref[...])
    qt_o[...] = qt.astype(BF16)
    vt_o[...] = _dot(wt_ref[w:2 * w, :], ht).astype(BF16)
    logft = _log_sigmoid(_dot(wft_ref[...], ht) + bft_ref[...])
    dt = _cumsum_cols(_tri(tm, upper=True), logft) + carryt_ref[:, 0:1]
    carryt_ref[...] = jnp.broadcast_to(dt[:, tm - 1:tm], carryt_ref.shape)
    piecest = jnp.concatenate(_split3(dt * LOG2E), axis=0)
    aqt_o[...] = (_dot(selqt_ref[...], piecest) + cqt_ref[...]).astype(BF16)

    @pl.when(i == pl.num_programs(0) - 1)
    def _():
        q = _head_rmsnorm(_dot(h, w_ref[:, 0:w]), grp, qg_ref[...])
        qs_o[...] = (q * FOX_Q_SCALE).astype(BF16)
        vs_o[...] = v.astype(BF16)


def _fox_proj(x, gain, w, wf, bf, qg, kg, grp, selk, ck, wt, wft, bft, qgt, selqt, cqt, tiles_per_seq):
    n = x.shape[0]
    tile = pl.BlockSpec((TOKEN_TILE, D_MODEL), lambda i: (i, 0))
    tile_t = pl.BlockSpec((D_MODEL, TOKEN_TILE), lambda i: (0, i))
    nar = pl.BlockSpec((TOKEN_TILE, LANES), lambda i: (i, 0))
    last = pl.BlockSpec((TOKEN_TILE, D_MODEL), lambda i: (0, 0))
    wide = lambda dt: jax.ShapeDtypeStruct((n, D_MODEL), dt)
    wide_t = jax.ShapeDtypeStruct((D_MODEL, n), BF16)
    ins = (x, gain, w, wf, bf, qg, kg, grp, selk, ck, wt, wft, bft, qgt, selqt, cqt)
    return pl.pallas_call(
        functools.partial(_fox_proj_body, tiles_per_seq=tiles_per_seq),
        out_shape=(wide_t, wide_t, wide_t, wide(BF16), wide(BF16), wide(F32), wide(F32), wide(BF16),
                   jax.ShapeDtypeStruct((n, LANES), F32), jax.ShapeDtypeStruct((n, LANES), F32),
                   jax.ShapeDtypeStruct((TOKEN_TILE, D_MODEL), BF16),
                   jax.ShapeDtypeStruct((TOKEN_TILE, D_MODEL), BF16)),
        grid=(n // TOKEN_TILE,),
        in_specs=[tile] + [_resident(a.shape) for a in ins[1:]],
        out_specs=(tile_t, tile_t, tile_t, tile, tile, tile, tile, tile, nar, nar, last, last),
        scratch_shapes=[pltpu.VMEM((8, LANES), F32), pltpu.VMEM((LANES, LANES), F32)],
        compiler_params=_params("arbitrary"),
        name="fox_proj",
    )(*ins)


def _fox_attn_body(dfirst_ref, dlast_ref, thr_ref, qt_ref, aqt_ref, k_ref, ak_ref, vt_ref, gate_ref, o_ref,
                   *, nblk):
    t = ATTN_TILE
    half = FOX_HEAD_DIM
    b, p, i = pl.program_id(0), pl.program_id(1), pl.program_id(2)
    qt = jnp.concatenate([qt_ref[...], aqt_ref[...]], axis=0)
    sub = lax.broadcasted_iota(jnp.int32, qt.shape, 0)
    slot = sub - LANES
    keep_a = (sub < half) | ((slot >= 0) & (slot < 3)) | ((slot >= 6) & (slot < 9))
    keep_b = ((sub >= half) & (sub < LANES)) | ((slot >= 3) & (slot < 6)) | ((slot >= 9) & (slot < 12))
    zero = jnp.zeros_like(qt)
    qh = (jnp.where(keep_a, qt, zero), jnp.where(keep_b, qt, zero))
    top = lax.broadcasted_iota(jnp.int32, (LANES, t), 0) < half
    ones = jnp.ones((LANES, t), BF16)
    krow = lax.broadcasted_iota(jnp.int32, (t, t), 0)
    qcol = lax.broadcasted_iota(jnp.int32, (t, t), 1)
    causal = krow <= qcol

    def step(j, carry, masked):
        off = pl.multiple_of(j * t, t)
        kk = jnp.concatenate([k_ref[pl.ds(off, t), :], ak_ref[pl.ds(off, t), :]], axis=1)
        vt = vt_ref[:, pl.ds(off, t)]
        vsel = (jnp.where(top, vt, ones), jnp.where(top, ones, vt))
        out = []
        for hh in range(2):
            m, acc = carry[hh]
            s = _dot(kk, qh[hh])
            if masked:
                s = jnp.where(causal, s, NEG_BIG)
            m_new = jnp.maximum(m, jnp.max(s, axis=0, keepdims=True))
            alpha = jnp.exp2(m - m_new)
            pr = jnp.exp2(s - m_new).astype(BF16)
            out.append((m_new, alpha * acc + _dot(vsel[hh], pr)))
        return tuple(out)

    init_one = (jnp.full((1, t), NEG_BIG, F32), jnp.zeros((LANES, t), F32))
    carry = step(i, (init_one, init_one), True)

    base = (b * FOX_HEADS + 2 * p) * nblk
    first_a = dfirst_ref[base + i]
    first_b = dfirst_ref[base + nblk + i]
    thr = thr_ref[0]

    def live(state):
        j = state[0]
        jj = jnp.maximum(j, 0)
        near = (first_a - dlast_ref[base + jj] >= -thr) | (first_b - dlast_ref[base + nblk + jj] >= -thr)
        return (j >= 0) & near

    def body(state):
        j, c = state
        return j - 1, step(j, c, False)

    _, carry = lax.while_loop(live, body, (i - 1, carry))
    (_, acca), (_, accb) = carry
    ot = jnp.concatenate([acca[0:half, :] / acca[half:half + 1, :],
                          accb[half:LANES, :] / accb[0:1, :]], axis=0)
    o_ref[...] = (ot.T * gate_ref[...].astype(F32)).astype(BF16)


def _fox_attn_prompt(dfirst, dlast, thr, qt, aqt, kb, ak, vt, gate, batch, seq):
    n = kb.shape[0]
    nq = seq // ATTN_TILE
    qspec_t = pl.BlockSpec((LANES, ATTN_TILE), lambda b, p, i, *_: (p, b * nq + i))
    tok = pl.BlockSpec((ATTN_TILE, LANES), lambda b, p, i, *_: (b * nq + i, p))
    kspec = pl.BlockSpec((seq, LANES), lambda b, p, i, *_: (b, p))
    vspec_t = pl.BlockSpec((LANES, seq), lambda b, p, i, *_: (p, b))
    return pl.pallas_call(
        functools.partial(_fox_attn_body, nblk=nq),
        out_shape=jax.ShapeDtypeStruct((n, D_MODEL), BF16),
        grid_spec=pltpu.PrefetchScalarGridSpec(
            num_scalar_prefetch=3,
            grid=(batch, FOX_PAIRS, nq),
            in_specs=[qspec_t, qspec_t, kspec, kspec, vspec_t, tok],
            out_specs=tok),
        compiler_params=_params("parallel", "parallel", "arbitrary"),
        name="fox_attn_prompt",
    )(dfirst, dlast, thr, qt, aqt, kb, ak, vt, gate)


def _pair_split(q2):
    lane = lax.broadcasted_iota(jnp.int32, q2.shape, 1)
    left = lane < FOX_HEAD_DIM
    zero = jnp.zeros_like(q2)
    return left, (jnp.where(left, q2, zero), jnp.where(left, zero, q2))


def _sample_bias_body(cl_ref, ln_ref, rt_o, ncol_o, nt_o):
    past = cl_ref.shape[1]
    n_new = ln_ref.shape[0]
    tri = _tri(MXU_DIM)
    carry = jnp.zeros((1, LANES), F32)
    prefix = []
    for b in range(past // MXU_DIM):
        p = _cumsum_rows(tri, cl_ref[0, b * MXU_DIM:(b + 1) * MXU_DIM, :]) + carry
        carry = p[MXU_DIM - 1:MXU_DIM, :]
        prefix.append(p)
    for b, p in enumerate(prefix):
        rt_o[0, :, b * MXU_DIM:(b + 1) * MXU_DIM] = (carry - p).T[0:FOX_HEADS, :]
    ln = jnp.concatenate([ln_ref[...], jnp.zeros((LANES - n_new, LANES), F32)], axis=0)
    nn = _cumsum_rows(_tri(LANES), ln)
    ncol_o[0] = nn[0:n_new, :]
    nt_o[0] = nn.T[0:FOX_HEADS, 0:n_new]


def _sample_bias(cl_pad, lf, n_new, row0):
    dec_batch, past, _ = cl_pad.shape
    blk0 = row0 // n_new
    return pl.pallas_call(
        _sample_bias_body,
        out_shape=(jax.ShapeDtypeStruct((dec_batch, FOX_HEADS, past), F32),
                   jax.ShapeDtypeStruct((dec_batch, n_new, LANES), F32),
                   jax.ShapeDtypeStruct((dec_batch, FOX_HEADS, n_new), F32)),
        grid=(dec_batch,),
        in_specs=[pl.BlockSpec((1, past, LANES), lambda b: (b, 0, 0)),
                  pl.BlockSpec((n_new, LANES), lambda b: (blk0 + b, 0))],
        out_specs=(pl.BlockSpec((1, FOX_HEADS, past), lambda b: (b, 0, 0)),
                   pl.BlockSpec((1, n_new, LANES), lambda b: (b, 0, 0)),
                   pl.BlockSpec((1, FOX_HEADS, n_new), lambda b: (b, 0, 0))),
        compiler_params=_params("parallel"),
        name="fox_sample_bias",
    )(cl_pad, lf)


def _fox_sample_body(q_ref, vn_ref, kn_ref, gate_ref, kc_ref, vc_ref, rt_ref, nc_ref, nt_ref,
                     y_any, o_ref):
    del y_any
    n_new = q_ref.shape[0]
    left, qs = _pair_split(q_ref[...])
    kc = kc_ref[0].astype(BF16)
    vc = vc_ref[0].astype(BF16)
    kn = kn_ref[...]
    vn = vn_ref[...]
    rt = rt_ref[0, 0]
    nc = nc_ref[0, 0]
    nt = nt_ref[0, 0]
    row = lax.broadcasted_iota(jnp.int32, (n_new, n_new), 0)
    col = lax.broadcasted_iota(jnp.int32, (n_new, n_new), 1)
    outs = []
    for hh in range(2):
        nq = nc[:, hh:hh + 1]
        s1 = lax.dot_general(qs[hh], kc, _NT, preferred_element_type=F32) + (nq + rt[hh:hh + 1, :])
        s2 = lax.dot_general(qs[hh], kn, _NT, preferred_element_type=F32) + (nq - nt[hh:hh + 1, :])
        s2 = jnp.where(col <= row, s2, NEG_BIG)
        m = jnp.maximum(jnp.max(s1, axis=1, keepdims=True), jnp.max(s2, axis=1, keepdims=True))
        p1 = jnp.exp(s1 - m)
        p2 = jnp.exp(s2 - m)
        l = jnp.sum(p1, axis=1, keepdims=True) + jnp.sum(p2, axis=1, keepdims=True)
        outs.append((_dot(p1.astype(BF16), vc) + _dot(p2.astype(BF16), vn)) / l)
    o = jnp.where(left, outs[0], outs[1])
    o_ref[...] = (o * gate_ref[...].astype(F32)).astype(BF16)


def _fox_attn_sample(qs, vs, kb, gate, kc, vc, rt, nc, nt, y_all, n_new, row0):
    dec_batch, past, _ = kc.shape
    blk0 = row0 // n_new
    own = pl.BlockSpec((n_new, LANES), lambda b, p: (b, p))
    new = pl.BlockSpec((n_new, LANES), lambda b, p: (blk0 + b, p))
    cache = pl.BlockSpec((1, past, LANES), lambda b, p: (b, 0, p))
    return pl.pallas_call(
        _fox_sample_body,
        out_shape=jax.ShapeDtypeStruct(y_all.shape, BF16),
        grid=(dec_batch, FOX_PAIRS),
        in_specs=[own, own, new, new, cache, cache,
                  pl.BlockSpec((1, 1, 2, past), lambda b, p: (b, p, 0, 0)),
                  pl.BlockSpec((1, 1, n_new, 2), lambda b, p: (b, p, 0, 0)),
                  pl.BlockSpec((1, 1, 2, n_new), lambda b, p: (b, p, 0, 0)),
                  pl.BlockSpec(memory_space=pl.ANY)],
        out_specs=new,
        input_output_aliases={9: 0},
        compiler_params=_params("parallel", "parallel"),
        name="fox_attn_sample",
    )(qs, vs, kb, gate, kc, vc, rt, nc, nt, y_all)


def _ret_log_gammas():
    return [float(v) for v in np.log1p(-np.exp2(-5.0 - np.arange(RET_HEADS, dtype=np.float32)))]


def _ret_proj_body(x_ref, g_ref, w_ref, cos_ref, sin_ref, q_o, k_o, v_o, sg_o):
    h = _rmsnorm(x_ref[...], g_ref[...]).astype(BF16)
    cos = cos_ref[...]
    sin = sin_ref[...]
    a, b2 = RET_QK_WIDTH, 2 * RET_QK_WIDTH

    def rotary(t, scale):
        outs = []
        for hd in range(RET_HEADS):
            th = t[:, hd * RET_QK_DIM:(hd + 1) * RET_QK_DIM]
            outs.append((th * cos + pltpu.roll(th, RET_QK_DIM // 2, 1) * sin) * scale)
        return jnp.concatenate(outs, axis=1).astype(BF16)

    q_o[...] = rotary(_dot(h, w_ref[:, 0:a]), 1.0)
    k_o[...] = rotary(_dot(h, w_ref[:, a:b2]), RET_QK_DIM ** -0.5)
    v_o[...] = _dot(h, w_ref[:, b2:b2 + RET_V_WIDTH]).astype(BF16)
    g = _dot(h, w_ref[:, b2 + RET_V_WIDTH:b2 + 2 * RET_V_WIDTH])
    sg_o[...] = (g * jax.nn.sigmoid(g)).astype(BF16)


def _ret_proj(x, gain, w, cos, sin):
    n = x.shape[0]
    tile = lambda width: pl.BlockSpec((TOKEN_TILE, width), lambda i: (i, 0))
    return pl.pallas_call(
        _ret_proj_body,
        out_shape=(jax.ShapeDtypeStruct((n, RET_QK_WIDTH), BF16), jax.ShapeDtypeStruct((n, RET_QK_WIDTH), BF16),
                   jax.ShapeDtypeStruct((n, RET_V_WIDTH), BF16), jax.ShapeDtypeStruct((n, RET_V_WIDTH), BF16)),
        grid=(n // TOKEN_TILE,),
        in_specs=[tile(D_MODEL), _resident((1, D_MODEL)), _resident(w.shape), tile(LANES), tile(LANES)],
        out_specs=(tile(RET_QK_WIDTH), tile(RET_QK_WIDTH), tile(RET_V_WIDTH), tile(RET_V_WIDTH)),
        compiler_params=_params("parallel"),
        name="ret_proj",
    )(x, gain, w, cos, sin)


def _ret_decay_matrix(t, lg):
    row = lax.broadcasted_iota(jnp.int32, (t, t), 0)
    col = lax.broadcasted_iota(jnp.int32, (t, t), 1)
    dist = jnp.abs(row - col).astype(F32)
    visible = (col // CHUNK) <= (row // CHUNK)
    return jnp.where(visible, jnp.exp(dist * lg), 0.0)


def _ret_tile(q_ref, k_ref, v_ref, sg_ref, y_ref, state_in, state_out, decay, t):
    lgs = _ret_log_gammas()
    pos = lax.broadcasted_iota(jnp.int32, (t, 1), 0).astype(F32)
    for hd in range(RET_HEADS):
        lg = lgs[hd]
        qs = slice(hd * RET_QK_DIM, (hd + 1) * RET_QK_DIM)
        vs = slice(hd * RET_V_DIM, (hd + 1) * RET_V_DIM)
        qh = q_ref[:, qs]
        kh = k_ref[:, qs]
        vh = v_ref[:, vs]
        state = state_in(hd)
        sc = lax.dot_general(qh, kh, _NT, preferred_element_type=F32) * decay(hd)
        o = _dot(sc.astype(BF16), vh) + _dot(qh, state.astype(BF16)) * jnp.exp(pos * lg)
        kd = (kh.astype(F32) * jnp.exp((t - pos) * lg)).astype(BF16)
        state_out(hd, state * math.exp(t * lg) + lax.dot_general(kd, vh, _TN, preferred_element_type=F32))
        mu = jnp.mean(o, axis=-1, keepdims=True)
        dev = o - mu
        var = jnp.mean(dev * dev, axis=-1, keepdims=True)
        y_ref[:, vs] = (sg_ref[:, vs].astype(F32) * (dev * lax.rsqrt(var + GN_EPS))).astype(BF16)


def _ret_prompt_body(q_ref, k_ref, v_ref, sg_ref, y_ref, st_ref, decay_ref):
    t = TOKEN_TILE

    @pl.when((pl.program_id(0) == 0) & (pl.program_id(1) == 0))
    def _():
        for hd, lg in enumerate(_ret_log_gammas()):
            decay_ref[hd] = _ret_decay_matrix(t, lg)

    @pl.when(pl.program_id(1) == 0)
    def _():
        st_ref[...] = jnp.zeros_like(st_ref)

    def state_out(hd, val):
        st_ref[0, hd] = val

    _ret_tile(q_ref, k_ref, v_ref, sg_ref, y_ref, lambda hd: st_ref[0, hd], state_out,
              lambda hd: decay_ref[hd], t)


def _ret_prompt(q, k, v, sg, batch, seq):
    nt = seq // TOKEN_TILE
    tile = lambda width: pl.BlockSpec((TOKEN_TILE, width), lambda b, i: (b * nt + i, 0))
    return pl.pallas_call(
        _ret_prompt_body,
        out_shape=(jax.ShapeDtypeStruct((q.shape[0], RET_V_WIDTH), BF16),
                   jax.ShapeDtypeStruct((batch, RET_HEADS, RET_QK_DIM, RET_V_DIM), F32)),
        grid=(batch, nt),
        in_specs=[tile(RET_QK_WIDTH), tile(RET_QK_WIDTH), tile(RET_V_WIDTH), tile(RET_V_WIDTH)],
        out_specs=(tile(RET_V_WIDTH),
                   pl.BlockSpec((1, RET_HEADS, RET_QK_DIM, RET_V_DIM), lambda b, i: (b, 0, 0, 0))),
        scratch_shapes=[pltpu.VMEM((RET_HEADS, TOKEN_TILE, TOKEN_TILE), F32)],
        compiler_params=_params("arbitrary", "arbitrary"),
        name="ret_prompt",
    )(q, k, v, sg)


def _ret_sample_body(q_ref, k_ref, v_ref, sg_ref, st_in_ref, y_any, y_ref, st_out_ref):
    del y_any
    t = q_ref.shape[0]
    lgs = _ret_log_gammas()

    def state_out(hd, val):
        st_out_ref[0, hd] = val

    _ret_tile(q_ref, k_ref, v_ref, sg_ref, y_ref, lambda hd: st_in_ref[0, hd], state_out,
              lambda hd: _ret_decay_matrix(t, lgs[hd]), t)


def _ret_sample(q, k, v, sg, state, y_all, n_new, row0):
    dec_batch = state.shape[0]
    blk0 = row0 // n_new
    tile = lambda width: pl.BlockSpec((n_new, width), lambda b: (blk0 + b, 0))
    st = pl.BlockSpec((1, RET_HEADS, RET_QK_DIM, RET_V_DIM), lambda b: (b, 0, 0, 0))
    return pl.pallas_call(
        _ret_sample_body,
        out_shape=(jax.ShapeDtypeStruct(y_all.shape, BF16), jax.ShapeDtypeStruct(state.shape, F32)),
        grid=(dec_batch,),
        in_specs=[tile(RET_QK_WIDTH), tile(RET_QK_WIDTH), tile(RET_V_WIDTH), tile(RET_V_WIDTH), st,
                  pl.BlockSpec(memory_space=pl.ANY)],
        out_specs=(tile(RET_V_WIDTH), st),
        input_output_aliases={5: 0},
        compiler_params=_params("parallel"),
        name="ret_sample",
    )(q, k, v, sg, state, y_all)


def _rotary_tables(pos):
    half = RET_QK_DIM // 2
    inv_freq = ROPE_BASE ** (-jnp.linspace(0.0, 1.0, half, dtype=F32))
    ang = pos[:, None] * inv_freq[None, :]
    cos, sin = jnp.cos(ang), jnp.sin(ang)
    return jnp.concatenate([cos, cos], axis=1), jnp.concatenate([-sin, sin], axis=1)


def _head_tiled(gain):
    return jnp.tile(gain.astype(F32), FOX_HEADS)


def kernel(x_prompt, x_sample, cache_fox_k, cache_fox_v, cache_fox_logf, state_ret, norm_gains, w_ffn_in,
           w_ffn_out, fox_w_in, fox_b_f, fox_q_gain, fox_k_gain, fox_w_out, ret_w_in, ret_w_out, final_gain):
    batch, seq, _ = x_prompt.shape
    dec_batch, n_new, _ = x_sample.shape
    past = cache_fox_k.shape[2]
    depth = norm_gains.shape[0]
    n_prompt = batch * seq
    n_sample = dec_batch * n_new
    assert seq % TOKEN_TILE == 0 and n_sample == TOKEN_TILE and TOKEN_TILE % n_new == 0
    assert seq % ATTN_TILE == 0 and past % MXU_DIM == 0 and TOKEN_TILE % CHUNK == 0 and n_new <= CHUNK
    nblk = seq // ATTN_TILE

    x = jnp.concatenate([x_prompt.reshape(n_prompt, D_MODEL), x_sample.reshape(n_sample, D_MODEL)], axis=0)
    pos = jnp.concatenate([jnp.tile(jnp.arange(seq, dtype=F32), batch),
                           jnp.tile(past + jnp.arange(n_new, dtype=F32), dec_batch)])
    cos, sin = _rotary_tables(pos)
    grp_idx = np.arange(MXU_DIM) // FOX_HEAD_DIM
    grp = jnp.asarray(grp_idx[:, None] == grp_idx[None, :], dtype=BF16)
    sel_k, sel_q, const_k, const_q = _aug_tables()
    selk, ck = jnp.asarray(sel_k, BF16), jnp.asarray(const_k, F32)
    selqt, cqt = jnp.asarray(sel_q.T, BF16), jnp.asarray(const_q.T, F32)
    fgain = final_gain.reshape(1, D_MODEL)

    fk_p, fv_p, fl_p, rs_p, fk_s, fv_s, fl_s, rs_s = [], [], [], [], [], [], [], []
    for layer in range(depth):
        g = norm_gains[layer].reshape(3, 1, D_MODEL)
        x = _ffn(x, g[0], w_ffn_in[layer, 0].astype(BF16), w_ffn_out[layer, 0].astype(BF16), fgain, False)
        j = layer // 2
        if layer % 2 == 0:
            w = fox_w_in[j]
            wf = jnp.pad(w[:, 4 * D_MODEL:], ((0, 0), (0, LANES - FOX_HEADS)))
            bf = jnp.pad(fox_b_f[j], (0, LANES - FOX_HEADS))
            wt = jnp.concatenate([w[:, 0:D_MODEL], w[:, 2 * D_MODEL:3 * D_MODEL]], axis=1).T
            qg, kg = _head_tiled(fox_q_gain[j]), _head_tiled(fox_k_gain[j])
            qt, aqt, vt, kb, ak, kf, vf, gate, lf, d, qs, vs = _fox_proj(
                x, g[1], w[:, :4 * D_MODEL].astype(BF16), wf.astype(BF16), bf.reshape(1, LANES),
                qg.reshape(1, D_MODEL), kg.reshape(1, D_MODEL), grp, selk, ck,
                wt.astype(BF16), wf.T.astype(BF16), bf.reshape(LANES, 1),
                (qg * (FOX_Q_SCALE * LOG2E)).reshape(D_MODEL, 1), selqt, cqt, seq // TOKEN_TILE)
            dd = d[:n_prompt, :FOX_HEADS].reshape(batch, nblk, ATTN_TILE, FOX_HEADS)
            dfirst = dd[:, :, 0, :].transpose(0, 2, 1).reshape(-1)
            dlast = dd[:, :, ATTN_TILE - 1, :].transpose(0, 2, 1).reshape(-1)
            bound = 1.01 * (FOX_HEAD_DIM ** 0.5) * jnp.max(jnp.abs(fox_q_gain[j])) * jnp.max(jnp.abs(fox_k_gain[j]))
            thr = (2.0 * bound + SKIP_MARGIN).reshape(1).astype(F32)
            y = _fox_attn_prompt(dfirst, dlast, thr, qt, aqt, kb, ak, vt, gate, batch, seq)
            cl_pad = jnp.pad(cache_fox_logf[j], ((0, 0), (0, 0), (0, LANES - FOX_HEADS)))
            rt, nc, nt = _sample_bias(cl_pad, lf, n_new, n_prompt)
            y = _fox_attn_sample(
                qs, vs, kb, gate, cache_fox_k[j].reshape(dec_batch, past, D_MODEL),
                cache_fox_v[j].reshape(dec_batch, past, D_MODEL), rt.reshape(dec_batch, FOX_PAIRS, 2, past),
                nc[:, :, :FOX_HEADS].reshape(dec_batch, n_new, FOX_PAIRS, 2).transpose(0, 2, 1, 3),
                nt.reshape(dec_batch, FOX_PAIRS, 2, n_new), y, n_new, n_prompt)
            x = _proj_res(y, fox_w_out[j].astype(BF16), x)
            fk_p.append(kf[:n_prompt].reshape(batch, seq, FOX_HEADS, FOX_HEAD_DIM))
            fv_p.append(vf[:n_prompt].reshape(batch, seq, FOX_HEADS, FOX_HEAD_DIM))
            fl_p.append(lf[:n_prompt, :FOX_HEADS].reshape(batch, seq, FOX_HEADS))
            fk_s.append(kf[n_prompt:].reshape(dec_batch, n_new, FOX_HEADS, FOX_HEAD_DIM))
            fv_s.append(vf[n_prompt:].reshape(dec_batch, n_new, FOX_HEADS, FOX_HEAD_DIM))
            fl_s.append(lf[n_prompt:, :FOX_HEADS].reshape(dec_batch, n_new, FOX_HEADS))
        else:
            q, k, v, sg = _ret_proj(x, g[1], ret_w_in[j].astype(BF16), cos, sin)
            y, st_p = _ret_prompt(q, k, v, sg, batch, seq)
            y, st_s = _ret_sample(q, k, v, sg, state_ret[j], y, n_new, n_prompt)
            x = _proj_res(y, ret_w_out[j].astype(BF16), x)
            rs_p.append(st_p)
            rs_s.append(st_s)
        x = _ffn(x, g[2], w_ffn_in[layer, 1].astype(BF16), w_ffn_out[layer, 1].astype(BF16), fgain,
                 layer == depth - 1)
    return (x[:n_prompt].reshape(batch, seq, D_MODEL), x[n_prompt:].reshape(dec_batch, n_new, D_MODEL),
            jnp.stack(fk_p), jnp.stack(fv_p), jnp.stack(fl_p), jnp.stack(rs_p),
            jnp.stack(fk_s), jnp.stack(fv_s), jnp.stack(fl_s), jnp.stack(rs_s))
```

```python
import functools
import math

import numpy as np
import jax
import jax.numpy as jnp
from jax import lax
from jax.experimental import pallas as pl
from jax.experimental.pallas import tpu as pltpu

F32 = jnp.float32
BF16 = jnp.bfloat16

D_MODEL = 1024
CHUNK = 64
FOX_HEADS = 16
FOX_HEAD_DIM = 64
FOX_PAIRS = FOX_HEADS // 2
RET_HEADS = 8
RET_QK_DIM = 128
RET_V_DIM = 256
RET_QK_WIDTH = RET_HEADS * RET_QK_DIM
RET_V_WIDTH = RET_HEADS * RET_V_DIM
D_FF = 2816
RMS_EPS = 1e-6
GN_EPS = 1e-5
ROPE_BASE = 10000.0

LANES = 128
MXU_DIM = 256
TOKEN_TILE = 512
FFN_CHUNK = MXU_DIM
ATTN_Q = 512
ATTN_K = 256
ATTN_GROUP = 4
VMEM_LIMIT = 56 * 1024 * 1024
NEG_BIG = -1e30
LOG2E = 1.4426950408889634
FOX_Q_SCALE = FOX_HEAD_DIM ** -0.5
AUG_PIECES = 3
SKIP_MARGIN = 32.0

_NT = (((1,), (1,)), ((), ()))
_TN = (((0,), (0,)), ((), ()))


def _params(*sem):
    return pltpu.CompilerParams(dimension_semantics=sem, vmem_limit_bytes=VMEM_LIMIT)


def _resident(shape):
    nd = len(shape)
    return pl.BlockSpec(shape, lambda *_: (0,) * nd, pipeline_mode=pl.Buffered(1))


def _rmsnorm(x, gain):
    ms = jnp.mean(x * x, axis=-1, keepdims=True)
    return x * lax.rsqrt(ms + RMS_EPS) * gain


def _dot(a, b):
    return jnp.dot(a, b, preferred_element_type=F32)


def _split3(x):
    hi = x.astype(BF16)
    r = x - hi.astype(F32)
    mid = r.astype(BF16)
    lo = (r - mid.astype(F32)).astype(BF16)
    return hi, mid, lo


def _tri(n, upper=False):
    row = lax.broadcasted_iota(jnp.int32, (n, n), 0)
    col = lax.broadcasted_iota(jnp.int32, (n, n), 1)
    return jnp.where((row <= col) if upper else (row >= col), 1.0, 0.0).astype(BF16)


def _cumsum_rows(tri, x):
    hi, mid, lo = _split3(x)
    return _dot(tri, hi) + _dot(tri, mid) + _dot(tri, lo)


def _cumsum_cols(tri_upper, x):
    hi, mid, lo = _split3(x)
    return _dot(hi, tri_upper) + _dot(mid, tri_upper) + _dot(lo, tri_upper)


def _log_sigmoid(z):
    return jnp.minimum(z, 0.0) - jnp.log1p(jnp.exp(-jnp.abs(z)))


def _ffn_body(x_ref, g_ref, win_ref, wout_ref, fg_ref, o_ref, acc_ref, *, final):
    x = x_ref[...]
    h = _rmsnorm(x, g_ref[...]).astype(BF16)
    for c in range(D_FF // FFN_CHUNK):
        lo, hi = c * FFN_CHUNK, (c + 1) * FFN_CHUNK
        gate = _dot(h, win_ref[:, lo:hi])
        up = _dot(h, win_ref[:, D_FF + lo:D_FF + hi])
        act = (gate * jax.nn.sigmoid(gate) * up).astype(BF16)
        part = _dot(act, wout_ref[lo:hi, :])
        if c == 0:
            acc_ref[...] = part
        else:
            acc_ref[...] += part
    y = x + 0.5 * acc_ref[...]
    if final:
        y = _rmsnorm(y, fg_ref[...])
    o_ref[...] = y


def _ffn(x, gain, w_in, w_out, final_gain, final):
    n = x.shape[0]
    tile = pl.BlockSpec((TOKEN_TILE, D_MODEL), lambda i: (i, 0))
    return pl.pallas_call(
        functools.partial(_ffn_body, final=final),
        out_shape=jax.ShapeDtypeStruct((n, D_MODEL), F32),
        grid=(n // TOKEN_TILE,),
        in_specs=[tile, _resident((1, D_MODEL)), _resident((D_MODEL, 2 * D_FF)),
                  _resident((D_FF, D_MODEL)), _resident((1, D_MODEL))],
        out_specs=tile,
        scratch_shapes=[pltpu.VMEM((TOKEN_TILE, D_MODEL), F32)],
        compiler_params=_params("parallel"),
        name="ffn_final" if final else "ffn",
    )(x, gain, w_in, w_out, final_gain)


def _proj_res_body(y_ref, w_ref, x_ref, o_ref):
    o_ref[...] = x_ref[...] + _dot(y_ref[...], w_ref[...])


def _proj_res(y, w, x):
    n, k = y.shape
    xt = pl.BlockSpec((TOKEN_TILE, D_MODEL), lambda i: (i, 0))
    return pl.pallas_call(
        _proj_res_body,
        out_shape=jax.ShapeDtypeStruct((n, D_MODEL), F32),
        grid=(n // TOKEN_TILE,),
        in_specs=[pl.BlockSpec((TOKEN_TILE, k), lambda i: (i, 0)), _resident((k, D_MODEL)), xt],
        out_specs=xt,
        compiler_params=_params("parallel"),
        name="proj_res",
    )(y, w, x)


def _aug_tables():
    sel_k = np.zeros((AUG_PIECES * LANES, D_MODEL), np.float32)
    sel_q = np.zeros((AUG_PIECES * LANES, D_MODEL), np.float32)
    const_k = np.zeros((1, D_MODEL), np.float32)
    const_q = np.zeros((1, D_MODEL), np.float32)
    for p in range(FOX_PAIRS):
        for hh in range(2):
            for piece in range(AUG_PIECES):
                src = piece * LANES + 2 * p + hh
                sel_k[src, p * LANES + 3 * hh + piece] = 1.0
                sel_q[src, p * LANES + 6 + 3 * hh + piece] = 1.0
                const_q[0, p * LANES + 3 * hh + piece] = -1.0
                const_k[0, p * LANES + 6 + 3 * hh + piece] = 1.0
    return sel_k, sel_q, const_k, const_q


def _head_rmsnorm(t, grp, gain):
    outs = []
    for c in range(D_MODEL // MXU_DIM):
        blk = t[:, c * MXU_DIM:(c + 1) * MXU_DIM]
        sq = blk * blk
        hi = sq.astype(BF16)
        lo = (sq - hi.astype(F32)).astype(BF16)
        ss = _dot(hi, grp) + _dot(lo, grp)
        outs.append(blk * lax.rsqrt(ss * (1.0 / FOX_HEAD_DIM) + RMS_EPS))
    return jnp.concatenate(outs, axis=1) * gain


def _head_rmsnorm_t(t, grp, gain):
    outs = []
    for c in range(D_MODEL // MXU_DIM):
        blk = t[c * MXU_DIM:(c + 1) * MXU_DIM, :]
        sq = blk * blk
        hi = sq.astype(BF16)
        lo = (sq - hi.astype(F32)).astype(BF16)
        ss = _dot(grp, hi) + _dot(grp, lo)
        outs.append(blk * lax.rsqrt(ss * (1.0 / FOX_HEAD_DIM) + RMS_EPS))
    return jnp.concatenate(outs, axis=0) * gain


N_FOX_PROJ_INPUTS = 15
N_FOX_SAMPLE_INPUTS = 8


def _fox_token_major(h, w_ref, wf_ref, bf_ref, kg_ref, grp, kb_o, gate_o, lf_o, kf_o, vf_o):
    w = D_MODEL
    k = _head_rmsnorm(_dot(h, w_ref[:, w:2 * w]), grp, kg_ref[...])
    kb_o[...] = k.astype(BF16)
    kf_o[0] = k
    v = _dot(h, w_ref[:, 2 * w:3 * w])
    vf_o[0] = v
    gate_o[...] = jax.nn.sigmoid(_dot(h, w_ref[:, 3 * w:4 * w])).astype(BF16)
    logf = _log_sigmoid(_dot(h, wf_ref[...]) + bf_ref[...])
    lf_o[...] = logf
    return v, logf


def _fox_proj_sample_body(*refs, n_stacked):
    x_ref, g_ref, w_ref, wf_ref, bf_ref, qg_ref, kg_ref, grp_ref = refs[:N_FOX_SAMPLE_INPUTS]
    qs_o, kb_o, vs_o, gate_o, lf_o, kf_o, vf_o = refs[N_FOX_SAMPLE_INPUTS + n_stacked:]
    h = _rmsnorm(x_ref[...], g_ref[...]).astype(BF16)
    grp = grp_ref[...]
    v, _ = _fox_token_major(h, w_ref, wf_ref, bf_ref, kg_ref, grp, kb_o, gate_o, lf_o, kf_o, vf_o)
    vs_o[...] = v.astype(BF16)
    q = _head_rmsnorm(_dot(h, w_ref[:, 0:D_MODEL]), grp, qg_ref[...])
    qs_o[...] = (q * FOX_Q_SCALE).astype(BF16)


def _fox_proj_sample(ins, stacked, layer, n_layers, tile0):
    assert len(ins) == N_FOX_SAMPLE_INPUTS
    n = ins[0].shape[0] - tile0 * TOKEN_TILE
    tile = pl.BlockSpec((TOKEN_TILE, D_MODEL), lambda i: (i, 0))
    slab = pl.BlockSpec((1, TOKEN_TILE, D_MODEL), lambda i: (layer, i, 0))
    wide = jax.ShapeDtypeStruct((n, D_MODEL), BF16)
    stack = jax.ShapeDtypeStruct((n_layers, n, D_MODEL), F32)
    n_out_before_stacks = 5
    return pl.pallas_call(
        functools.partial(_fox_proj_sample_body, n_stacked=len(stacked)),
        out_shape=(wide, wide, wide, wide, jax.ShapeDtypeStruct((n, LANES), F32), stack, stack),
        grid=(n // TOKEN_TILE,),
        in_specs=([pl.BlockSpec((TOKEN_TILE, D_MODEL), lambda i: (tile0 + i, 0))]
                  + [_resident(a.shape) for a in ins[1:]] + [pl.BlockSpec(memory_space=pl.ANY)] * len(stacked)),
        out_specs=(tile, tile, tile, tile, pl.BlockSpec((TOKEN_TILE, LANES), lambda i: (i, 0)), slab, slab),
        input_output_aliases={N_FOX_SAMPLE_INPUTS + s: n_out_before_stacks + s for s in range(len(stacked))},
        compiler_params=_params("parallel"),
        name="fox_proj_sample",
    )(*ins, *stacked)


def _fox_proj_body(*refs, tiles_per_seq, n_stacked):
    (x_ref, g_ref, w_ref, wf_ref, bf_ref, kg_ref, grp_ref, selk_ref, ck_ref,
     wt_ref, wft_ref, bft_ref, qgt_ref, selqt_ref, cqt_ref) = refs[:N_FOX_PROJ_INPUTS]
    (qt_o, aqt_o, vt_o, kb_o, ak_o, gate_o, lf_o, d_o, kf_o, vf_o,
     carry_ref, carryt_ref) = refs[N_FOX_PROJ_INPUTS + n_stacked:]
    i = pl.program_id(0)
    tm, w = TOKEN_TILE, D_MODEL
    hf = _rmsnorm(x_ref[...], g_ref[...])
    h = hf.astype(BF16)
    ht = hf.T.astype(BF16)
    grp = grp_ref[...]

    @pl.when(i % tiles_per_seq == 0)
    def _():
        carry_ref[...] = jnp.zeros_like(carry_ref)
        carryt_ref[...] = jnp.zeros_like(carryt_ref)

    _, logf = _fox_token_major(h, w_ref, wf_ref, bf_ref, kg_ref, grp, kb_o, gate_o, lf_o, kf_o, vf_o)
    d = _cumsum_rows(_tri(tm), logf) + carry_ref[0:1, :]
    carry_ref[...] = jnp.broadcast_to(d[tm - 1:tm, :], carry_ref.shape)
    d_o[...] = d
    pieces = jnp.concatenate(_split3(d * LOG2E), axis=1)
    ak_o[...] = (_dot(pieces, selk_ref[...]) + ck_ref[...]).astype(BF16)

    qt = _head_rmsnorm_t(_dot(wt_ref[0:w, :], ht), grp, qgt_ref[...])
    qt_o[...] = qt.astype(BF16)
    vt_o[...] = _dot(wt_ref[w:2 * w, :], ht).astype(BF16)
    logft = _log_sigmoid(_dot(wft_ref[...], ht) + bft_ref[...])
    dt = _cumsum_cols(_tri(tm, upper=True), logft) + carryt_ref[:, 0:1]
    carryt_ref[...] = jnp.broadcast_to(dt[:, tm - 1:tm], carryt_ref.shape)
    piecest = jnp.concatenate(_split3(dt * LOG2E), axis=0)
    aqt_o[...] = (_dot(selqt_ref[...], piecest) + cqt_ref[...]).astype(BF16)


def _fox_proj(ins, stacked, layer, n_layers, n, tiles_per_seq):
    assert len(ins) == N_FOX_PROJ_INPUTS
    tile = pl.BlockSpec((TOKEN_TILE, D_MODEL), lambda i: (i, 0))
    tile_t = pl.BlockSpec((D_MODEL, TOKEN_TILE), lambda i: (0, i))
    nar = pl.BlockSpec((TOKEN_TILE, LANES), lambda i: (i, 0))
    slab = pl.BlockSpec((1, TOKEN_TILE, D_MODEL), lambda i: (layer, i, 0))
    wide = jax.ShapeDtypeStruct((n, D_MODEL), BF16)
    wide_t = jax.ShapeDtypeStruct((D_MODEL, n), BF16)
    narrow = jax.ShapeDtypeStruct((n, LANES), F32)
    stack = jax.ShapeDtypeStruct((n_layers, n, D_MODEL), F32)
    n_out_before_stacks = 8
    return pl.pallas_call(
        functools.partial(_fox_proj_body, tiles_per_seq=tiles_per_seq, n_stacked=len(stacked)),
        out_shape=(wide_t, wide_t, wide_t, wide, wide, wide, narrow, narrow, stack, stack),
        grid=(n // TOKEN_TILE,),
        in_specs=([tile] + [_resident(a.shape) for a in ins[1:]]
                  + [pl.BlockSpec(memory_space=pl.ANY)] * len(stacked)),
        out_specs=(tile_t, tile_t, tile_t, tile, tile, tile, nar, nar, slab, slab),
        input_output_aliases={N_FOX_PROJ_INPUTS + s: n_out_before_stacks + s for s in range(len(stacked))},
        scratch_shapes=[pltpu.VMEM((8, LANES), F32), pltpu.VMEM((LANES, LANES), F32)],
        compiler_params=_params("arbitrary"),
        name="fox_proj",
    )(*ins, *stacked)


def _fox_attn_body(dfirst_ref, dlast_ref, thr_ref, qt_ref, aqt_ref, k_ref, ak_ref, vt_ref, gate_ref, o_ref,
                   qh_ref, m_ref, acc_ref, *, nq, nkb):
    tq, tk = ATTN_Q, ATTN_K
    half = FOX_HEAD_DIM
    per_q = tq // tk
    b, p, i = pl.program_id(0), pl.program_id(1), pl.program_id(2)

    qt = jnp.concatenate([qt_ref[...], aqt_ref[...]], axis=0)
    sub = lax.broadcasted_iota(jnp.int32, qt.shape, 0)
    slot = sub - LANES
    keep_a = (sub < half) | ((slot >= 0) & (slot < 3)) | ((slot >= 6) & (slot < 9))
    zero = jnp.zeros_like(qt)
    qh_ref[0] = jnp.where(keep_a, qt, zero)
    keep_b = ((sub >= half) & (sub < LANES)) | ((slot >= 3) & (slot < 6)) | ((slot >= 9) & (slot < 12))
    qh_ref[1] = jnp.where(keep_b, qt, zero)
    m_ref[...] = jnp.full(m_ref.shape, NEG_BIG, F32)
    acc_ref[...] = jnp.zeros_like(acc_ref)

    def step(j0, nblocks, hh, masked):
        rows = nblocks * tk
        off = pl.multiple_of(j0 * tk, tk)
        kk = jnp.concatenate([k_ref[pl.ds(off, rows), :], ak_ref[pl.ds(off, rows), :]], axis=1)
        vt = vt_ref[:, pl.ds(off, rows)]
        top = lax.broadcasted_iota(jnp.int32, (LANES, rows), 0) < half
        ones = jnp.ones((LANES, rows), BF16)
        vsel = jnp.where(top, vt, ones) if hh == 0 else jnp.where(top, ones, vt)
        s = _dot(kk, qh_ref[hh])
        if masked:
            assert rows == tq
            kpos = lax.broadcasted_iota(jnp.int32, (rows, tq), 0)
            qpos = lax.broadcasted_iota(jnp.int32, (rows, tq), 1)
            s = jnp.where(kpos <= qpos, s, NEG_BIG)
        m_old = m_ref[hh, 0:1, :]
        m_new = jnp.maximum(m_old, jnp.max(s, axis=0, keepdims=True))
        alpha = jnp.exp2(m_old - m_new)
        pr = jnp.exp2(s - m_new).astype(BF16)
        acc_ref[hh] = alpha * acc_ref[hh] + _dot(vsel, pr)
        m_ref[hh] = jnp.broadcast_to(m_new, m_ref.shape[1:])

    thr = thr_ref[0]
    j_top = i * per_q - 1
    group = ATTN_GROUP

    for hh in range(2):
        step(i * per_q, per_q, hh, True)

        head = 2 * p + hh
        first = dfirst_ref[(b * FOX_HEADS + head) * nq + i]
        base = (b * FOX_HEADS + head) * nkb

        def near(n, first=first, base=base):
            j = jnp.maximum(j_top - n, 0)
            return (n <= j_top) & (first - dlast_ref[base + j] >= -thr)

        n_live = lax.while_loop(near, lambda n: n + 1, jnp.int32(0))
        n_groups = n_live // group

        def group_body(g, carry, hh=hh):
            step(j_top - group * (g + 1) + 1, group, hh, False)
            return carry

        lax.fori_loop(0, n_groups, group_body, 0)
        done = n_groups * group
        size = group // 2
        while size >= 1:
            take = ((n_live - done) & size) != 0

            @pl.when(take)
            def _(done=done, size=size, hh=hh):
                step(j_top - done - size + 1, size, hh, False)

            done = done + jnp.where(take, size, 0)
            size //= 2

    acca, accb = acc_ref[0], acc_ref[1]
    ot = jnp.concatenate([acca[0:half, :] / acca[half:half + 1, :],
                          accb[half:LANES, :] / accb[0:1, :]], axis=0)
    o_ref[...] = (ot.T * gate_ref[...].astype(F32)).astype(BF16)


def _fox_attn_prompt(dfirst, dlast, thr, qt, aqt, kb, ak, vt, gate, batch, seq, n):
    nq = seq // ATTN_Q
    qspec_t = pl.BlockSpec((LANES, ATTN_Q), lambda b, p, i, *_: (p, b * nq + i))
    tok = pl.BlockSpec((ATTN_Q, LANES), lambda b, p, i, *_: (b * nq + i, p))
    kspec = pl.BlockSpec((seq, LANES), lambda b, p, i, *_: (b, p))
    vspec_t = pl.BlockSpec((LANES, seq), lambda b, p, i, *_: (p, b))
    return pl.pallas_call(
        functools.partial(_fox_attn_body, nq=nq, nkb=seq // ATTN_K),
        out_shape=jax.ShapeDtypeStruct((n, D_MODEL), BF16),
        grid_spec=pltpu.PrefetchScalarGridSpec(
            num_scalar_prefetch=3,
            grid=(batch, FOX_PAIRS, nq),
            in_specs=[qspec_t, qspec_t, kspec, kspec, vspec_t, tok],
            out_specs=tok,
            scratch_shapes=[pltpu.VMEM((2, 2 * LANES, ATTN_Q), BF16), pltpu.VMEM((2, 8, ATTN_Q), F32),
                            pltpu.VMEM((2, LANES, ATTN_Q), F32)]),
        compiler_params=_params("parallel", "parallel", "arbitrary"),
        name="fox_attn_prompt",
    )(dfirst, dlast, thr, qt, aqt, kb, ak, vt, gate)


def _pair_split(q2):
    lane = lax.broadcasted_iota(jnp.int32, q2.shape, 1)
    left = lane < FOX_HEAD_DIM
    zero = jnp.zeros_like(q2)
    return left, (jnp.where(left, q2, zero), jnp.where(left, zero, q2))


def _sample_bias_body(cl_ref, ln_ref, rt_o, ncol_o, nt_o):
    past = cl_ref.shape[1]
    n_new = ln_ref.shape[0]
    tri = _tri(MXU_DIM)
    carry = jnp.zeros((1, LANES), F32)
    prefix = []
    for b in range(past // MXU_DIM):
        p = _cumsum_rows(tri, cl_ref[0, b * MXU_DIM:(b + 1) * MXU_DIM, :]) + carry
        carry = p[MXU_DIM - 1:MXU_DIM, :]
        prefix.append(p)
    for b, p in enumerate(prefix):
        rt_o[0, :, b * MXU_DIM:(b + 1) * MXU_DIM] = (carry - p).T[0:FOX_HEADS, :]
    ln = jnp.concatenate([ln_ref[...], jnp.zeros((LANES - n_new, LANES), F32)], axis=0)
    nn = _cumsum_rows(_tri(LANES), ln)
    ncol_o[0] = nn[0:n_new, :]
    nt_o[0] = nn.T[0:FOX_HEADS, 0:n_new]


def _sample_bias(cl_pad, lf, n_new, row0):
    dec_batch, past, _ = cl_pad.shape
    blk0 = row0 // n_new
    return pl.pallas_call(
        _sample_bias_body,
        out_shape=(jax.ShapeDtypeStruct((dec_batch, FOX_HEADS, past), F32),
                   jax.ShapeDtypeStruct((dec_batch, n_new, LANES), F32),
                   jax.ShapeDtypeStruct((dec_batch, FOX_HEADS, n_new), F32)),
        grid=(dec_batch,),
        in_specs=[pl.BlockSpec((1, past, LANES), lambda b: (b, 0, 0)),
                  pl.BlockSpec((n_new, LANES), lambda b: (blk0 + b, 0))],
        out_specs=(pl.BlockSpec((1, FOX_HEADS, past), lambda b: (b, 0, 0)),
                   pl.BlockSpec((1, n_new, LANES), lambda b: (b, 0, 0)),
                   pl.BlockSpec((1, FOX_HEADS, n_new), lambda b: (b, 0, 0))),
        compiler_params=_params("parallel"),
        name="fox_sample_bias",
    )(cl_pad, lf)


def _fox_sample_body(q_ref, vn_ref, kn_ref, gate_ref, kc_ref, vc_ref, rt_ref, nc_ref, nt_ref,
                     y_any, o_ref):
    del y_any
    n_new = q_ref.shape[0]
    left, qs = _pair_split(q_ref[...])
    kc = kc_ref[0].astype(BF16)
    vc = vc_ref[0].astype(BF16)
    kn = kn_ref[...]
    vn = vn_ref[...]
    rt = rt_ref[0, 0]
    nc = nc_ref[0, 0]
    nt = nt_ref[0, 0]
    row = lax.broadcasted_iota(jnp.int32, (n_new, n_new), 0)
    col = lax.broadcasted_iota(jnp.int32, (n_new, n_new), 1)
    outs = []
    for hh in range(2):
        nq = nc[:, hh:hh + 1]
        s1 = lax.dot_general(qs[hh], kc, _NT, preferred_element_type=F32) + (nq + rt[hh:hh + 1, :])
        s2 = lax.dot_general(qs[hh], kn, _NT, preferred_element_type=F32) + (nq - nt[hh:hh + 1, :])
        s2 = jnp.where(col <= row, s2, NEG_BIG)
        m = jnp.maximum(jnp.max(s1, axis=1, keepdims=True), jnp.max(s2, axis=1, keepdims=True))
        p1 = jnp.exp(s1 - m)
        p2 = jnp.exp(s2 - m)
        l = jnp.sum(p1, axis=1, keepdims=True) + jnp.sum(p2, axis=1, keepdims=True)
        outs.append((_dot(p1.astype(BF16), vc) + _dot(p2.astype(BF16), vn)) / l)
    o = jnp.where(left, outs[0], outs[1])
    o_ref[...] = (o * gate_ref[...].astype(F32)).astype(BF16)


def _fox_attn_sample(qs, vs, kb, gate, kc, vc, rt, nc, nt, y_all, n_new, row0):
    dec_batch, past, _ = kc.shape
    blk0 = row0 // n_new
    own = pl.BlockSpec((n_new, LANES), lambda b, p: (b, p))
    new = pl.BlockSpec((n_new, LANES), lambda b, p: (blk0 + b, p))
    cache = pl.BlockSpec((1, past, LANES), lambda b, p: (b, 0, p))
    return pl.pallas_call(
        _fox_sample_body,
        out_shape=jax.ShapeDtypeStruct(y_all.shape, BF16),
        grid=(dec_batch, FOX_PAIRS),
        in_specs=[own, own, own, own, cache, cache,
                  pl.BlockSpec((1, 1, 2, past), lambda b, p: (b, p, 0, 0)),
                  pl.BlockSpec((1, 1, n_new, 2), lambda b, p: (b, p, 0, 0)),
                  pl.BlockSpec((1, 1, 2, n_new), lambda b, p: (b, p, 0, 0)),
                  pl.BlockSpec(memory_space=pl.ANY)],
        out_specs=new,
        input_output_aliases={9: 0},
        compiler_params=_params("parallel", "parallel"),
        name="fox_attn_sample",
    )(qs, vs, kb, gate, kc, vc, rt, nc, nt, y_all)


def _ret_log_gammas():
    return [float(v) for v in np.log1p(-np.exp2(-5.0 - np.arange(RET_HEADS, dtype=np.float32)))]


def _ret_proj_body(x_ref, g_ref, w_ref, cos_ref, sin_ref, q_o, k_o, v_o, sg_o):
    h = _rmsnorm(x_ref[...], g_ref[...]).astype(BF16)
    cos = cos_ref[...]
    sin = sin_ref[...]
    a, b2 = RET_QK_WIDTH, 2 * RET_QK_WIDTH

    def rotary(t, scale):
        outs = []
        for hd in range(RET_HEADS):
            th = t[:, hd * RET_QK_DIM:(hd + 1) * RET_QK_DIM]
            outs.append((th * cos + pltpu.roll(th, RET_QK_DIM // 2, 1) * sin) * scale)
        return jnp.concatenate(outs, axis=1).astype(BF16)

    q_o[...] = rotary(_dot(h, w_ref[:, 0:a]), 1.0)
    k_o[...] = rotary(_dot(h, w_ref[:, a:b2]), RET_QK_DIM ** -0.5)
    v_o[...] = _dot(h, w_ref[:, b2:b2 + RET_V_WIDTH]).astype(BF16)
    g = _dot(h, w_ref[:, b2 + RET_V_WIDTH:b2 + 2 * RET_V_WIDTH])
    sg_o[...] = (g * jax.nn.sigmoid(g)).astype(BF16)


def _ret_proj(x, gain, w, cos, sin):
    n = x.shape[0]
    tile = lambda width: pl.BlockSpec((TOKEN_TILE, width), lambda i: (i, 0))
    return pl.pallas_call(
        _ret_proj_body,
        out_shape=(jax.ShapeDtypeStruct((n, RET_QK_WIDTH), BF16), jax.ShapeDtypeStruct((n, RET_QK_WIDTH), BF16),
                   jax.ShapeDtypeStruct((n, RET_V_WIDTH), BF16), jax.ShapeDtypeStruct((n, RET_V_WIDTH), BF16)),
        grid=(n // TOKEN_TILE,),
        in_specs=[tile(D_MODEL), _resident((1, D_MODEL)), _resident(w.shape), tile(LANES), tile(LANES)],
        out_specs=(tile(RET_QK_WIDTH), tile(RET_QK_WIDTH), tile(RET_V_WIDTH), tile(RET_V_WIDTH)),
        compiler_params=_params("parallel"),
        name="ret_proj",
    )(x, gain, w, cos, sin)


def _ret_decay_matrix(t, lg):
    row = lax.broadcasted_iota(jnp.int32, (t, t), 0)
    col = lax.broadcasted_iota(jnp.int32, (t, t), 1)
    dist = jnp.abs(row - col).astype(F32)
    visible = (col // CHUNK) <= (row // CHUNK)
    return jnp.where(visible, jnp.exp(dist * lg), 0.0)


def _ret_tile(q_ref, k_ref, v_ref, sg_ref, y_ref, state_in, state_out, decay, t):
    lgs = _ret_log_gammas()
    pos = lax.broadcasted_iota(jnp.int32, (t, 1), 0).astype(F32)
    for hd in range(RET_HEADS):
        lg = lgs[hd]
        qs = slice(hd * RET_QK_DIM, (hd + 1) * RET_QK_DIM)
        vs = slice(hd * RET_V_DIM, (hd + 1) * RET_V_DIM)
        qh = q_ref[:, qs]
        kh = k_ref[:, qs]
        vh = v_ref[:, vs]
        state = state_in(hd)
        sc = lax.dot_general(qh, kh, _NT, preferred_element_type=F32) * decay(hd)
        o = _dot(sc.astype(BF16), vh) + _dot(qh, state.astype(BF16)) * jnp.exp(pos * lg)
        kd = (kh.astype(F32) * jnp.exp((t - pos) * lg)).astype(BF16)
        state_out(hd, state * math.exp(t * lg) + lax.dot_general(kd, vh, _TN, preferred_element_type=F32))
        mu = jnp.mean(o, axis=-1, keepdims=True)
        dev = o - mu
        var = jnp.mean(dev * dev, axis=-1, keepdims=True)
        y_ref[:, vs] = (sg_ref[:, vs].astype(F32) * (dev * lax.rsqrt(var + GN_EPS))).astype(BF16)


def _ret_prompt_body(q_ref, k_ref, v_ref, sg_ref, y_ref, st_ref, decay_ref):
    t = TOKEN_TILE

    @pl.when((pl.program_id(0) == 0) & (pl.program_id(1) == 0))
    def _():
        for hd, lg in enumerate(_ret_log_gammas()):
            decay_ref[hd] = _ret_decay_matrix(t, lg)

    @pl.when(pl.program_id(1) == 0)
    def _():
        st_ref[...] = jnp.zeros_like(st_ref)

    def state_out(hd, val):
        st_ref[0, hd] = val

    _ret_tile(q_ref, k_ref, v_ref, sg_ref, y_ref, lambda hd: st_ref[0, hd], state_out,
              lambda hd: decay_ref[hd], t)


def _ret_prompt(q, k, v, sg, batch, seq):
    nt = seq // TOKEN_TILE
    tile = lambda width: pl.BlockSpec((TOKEN_TILE, width), lambda b, i: (b * nt + i, 0))
    return pl.pallas_call(
        _ret_prompt_body,
        out_shape=(jax.ShapeDtypeStruct((q.shape[0], RET_V_WIDTH), BF16),
                   jax.ShapeDtypeStruct((batch, RET_HEADS, RET_QK_DIM, RET_V_DIM), F32)),
        grid=(batch, nt),
        in_specs=[tile(RET_QK_WIDTH), tile(RET_QK_WIDTH), tile(RET_V_WIDTH), tile(RET_V_WIDTH)],
        out_specs=(tile(RET_V_WIDTH),
                   pl.BlockSpec((1, RET_HEADS, RET_QK_DIM, RET_V_DIM), lambda b, i: (b, 0, 0, 0))),
        scratch_shapes=[pltpu.VMEM((RET_HEADS, TOKEN_TILE, TOKEN_TILE), F32)],
        compiler_params=_params("arbitrary", "arbitrary"),
        name="ret_prompt",
    )(q, k, v, sg)


def _ret_sample_body(q_ref, k_ref, v_ref, sg_ref, st_in_ref, y_any, y_ref, st_out_ref):
    del y_any
    t = q_ref.shape[0]
    lgs = _ret_log_gammas()

    def state_out(hd, val):
        st_out_ref[0, hd] = val

    _ret_tile(q_ref, k_ref, v_ref, sg_ref, y_ref, lambda hd: st_in_ref[0, hd], state_out,
              lambda hd: _ret_decay_matrix(t, lgs[hd]), t)


def _ret_sample(q, k, v, sg, state, y_all, n_new, row0):
    dec_batch = state.shape[0]
    blk0 = row0 // n_new
    tile = lambda width: pl.BlockSpec((n_new, width), lambda b: (blk0 + b, 0))
    st = pl.BlockSpec((1, RET_HEADS, RET_QK_DIM, RET_V_DIM), lambda b: (b, 0, 0, 0))
    return pl.pallas_call(
        _ret_sample_body,
        out_shape=(jax.ShapeDtypeStruct(y_all.shape, BF16), jax.ShapeDtypeStruct(state.shape, F32)),
        grid=(dec_batch,),
        in_specs=[tile(RET_QK_WIDTH), tile(RET_QK_WIDTH), tile(RET_V_WIDTH), tile(RET_V_WIDTH), st,
                  pl.BlockSpec(memory_space=pl.ANY)],
        out_specs=(tile(RET_V_WIDTH), st),
        input_output_aliases={5: 0},
        compiler_params=_params("parallel"),
        name="ret_sample",
    )(q, k, v, sg, state, y_all)


def _rotary_tables(pos):
    half = RET_QK_DIM // 2
    inv_freq = ROPE_BASE ** (-jnp.linspace(0.0, 1.0, half, dtype=F32))
    ang = pos[:, None] * inv_freq[None, :]
    cos, sin = jnp.cos(ang), jnp.sin(ang)
    return jnp.concatenate([cos, cos], axis=1), jnp.concatenate([-sin, sin], axis=1)


def _head_tiled(gain):
    return jnp.tile(gain.astype(F32), FOX_HEADS)


def kernel(x_prompt, x_sample, cache_fox_k, cache_fox_v, cache_fox_logf, state_ret, norm_gains, w_ffn_in,
           w_ffn_out, fox_w_in, fox_b_f, fox_q_gain, fox_k_gain, fox_w_out, ret_w_in, ret_w_out, final_gain):
    batch, seq, _ = x_prompt.shape
    dec_batch, n_new, _ = x_sample.shape
    past = cache_fox_k.shape[2]
    depth = norm_gains.shape[0]
    n_prompt = batch * seq
    n_sample = dec_batch * n_new
    assert seq % TOKEN_TILE == 0 and n_sample == TOKEN_TILE and TOKEN_TILE % n_new == 0
    assert seq % ATTN_Q == 0 and ATTN_Q % ATTN_K == 0 and past % MXU_DIM == 0
    assert TOKEN_TILE % CHUNK == 0 and n_new <= CHUNK
    n_fox = (depth + 1) // 2

    x = jnp.concatenate([x_prompt.reshape(n_prompt, D_MODEL), x_sample.reshape(n_sample, D_MODEL)], axis=0)
    pos = jnp.concatenate([jnp.tile(jnp.arange(seq, dtype=F32), batch),
                           jnp.tile(past + jnp.arange(n_new, dtype=F32), dec_batch)])
    cos, sin = _rotary_tables(pos)
    grp_idx = np.arange(MXU_DIM) // FOX_HEAD_DIM
    grp = jnp.asarray(grp_idx[:, None] == grp_idx[None, :], dtype=BF16)
    sel_k, sel_q, const_k, const_q = _aug_tables()
    selk, ck = jnp.asarray(sel_k, BF16), jnp.asarray(const_k, F32)
    selqt, cqt = jnp.asarray(sel_q.T, BF16), jnp.asarray(const_q.T, F32)
    fgain = final_gain.reshape(1, D_MODEL)

    fl_p, rs_p, fl_s, rs_s = [], [], [], []
    fox_stacked = ()
    for layer in range(depth):
        g = norm_gains[layer].reshape(3, 1, D_MODEL)
        x = _ffn(x, g[0], w_ffn_in[layer, 0].astype(BF16), w_ffn_out[layer, 0].astype(BF16), fgain, False)
        j = layer // 2
        if layer % 2 == 0:
            w = fox_w_in[j]
            wf = jnp.pad(w[:, 4 * D_MODEL:], ((0, 0), (0, LANES - FOX_HEADS)))
            bf = jnp.pad(fox_b_f[j], (0, LANES - FOX_HEADS))
            wt = jnp.concatenate([w[:, 0:D_MODEL], w[:, 2 * D_MODEL:3 * D_MODEL]], axis=1).T
            qg, kg = _head_tiled(fox_q_gain[j]), _head_tiled(fox_k_gain[j])
            token_major = (x, g[1], w[:, :4 * D_MODEL].astype(BF16), wf.astype(BF16), bf.reshape(1, LANES))
            kg2, qg2 = kg.reshape(1, D_MODEL), qg.reshape(1, D_MODEL)
            proj_ins = token_major + (
                kg2, grp, selk, ck, wt.astype(BF16), wf.T.astype(BF16), bf.reshape(LANES, 1),
                (qg * (FOX_Q_SCALE * LOG2E)).reshape(D_MODEL, 1), selqt, cqt)
            qt, aqt, vt, kb, ak, gate, lf, d, *stk_p = _fox_proj(
                proj_ins, fox_stacked[:2], j, n_fox, n_prompt, seq // TOKEN_TILE)
            qs, ks, vs, gate_s, lf_s, *stk_s = _fox_proj_sample(
                token_major + (qg2, kg2, grp), fox_stacked[2:], j, n_fox, n_prompt // TOKEN_TILE)
            fox_stacked = (*stk_p, *stk_s)
            dh = d[:, :FOX_HEADS].reshape(batch, seq, FOX_HEADS)
            dfirst = dh[:, 0::ATTN_Q, :].transpose(0, 2, 1).reshape(-1)
            dlast = dh[:, ATTN_K - 1::ATTN_K, :].transpose(0, 2, 1).reshape(-1)
            bound = 1.01 * (FOX_HEAD_DIM ** 0.5) * jnp.max(jnp.abs(fox_q_gain[j])) * jnp.max(jnp.abs(fox_k_gain[j]))
            thr = (2.0 * bound + SKIP_MARGIN).reshape(1).astype(F32)
            y = _fox_attn_prompt(dfirst, dlast, thr, qt, aqt, kb, ak, vt, gate, batch, seq, n_prompt + n_sample)
            cl_pad = jnp.pad(cache_fox_logf[j], ((0, 0), (0, 0), (0, LANES - FOX_HEADS)))
            rt, nc, nt = _sample_bias(cl_pad, lf_s, n_new, 0)
            y = _fox_attn_sample(
                qs, vs, ks, gate_s, cache_fox_k[j].reshape(dec_batch, past, D_MODEL),
                cache_fox_v[j].reshape(dec_batch, past, D_MODEL), rt.reshape(dec_batch, FOX_PAIRS, 2, past),
                nc[:, :, :FOX_HEADS].reshape(dec_batch, n_new, FOX_PAIRS, 2).transpose(0, 2, 1, 3),
                nt.reshape(dec_batch, FOX_PAIRS, 2, n_new), y, n_new, n_prompt)
            x = _proj_res(y, fox_w_out[j].astype(BF16), x)
            fl_p.append(lf[:, :FOX_HEADS].reshape(batch, seq, FOX_HEADS))
            fl_s.append(lf_s[:, :FOX_HEADS].reshape(dec_batch, n_new, FOX_HEADS))
        else:
            q, k, v, sg = _ret_proj(x, g[1], ret_w_in[j].astype(BF16), cos, sin)
            y, st_p = _ret_prompt(q, k, v, sg, batch, seq)
            y, st_s = _ret_sample(q, k, v, sg, state_ret[j], y, n_new, n_prompt)
            x = _proj_res(y, ret_w_out[j].astype(BF16), x)
            rs_p.append(st_p)
            rs_s.append(st_s)
        x = _ffn(x, g[2], w_ffn_in[layer, 1].astype(BF16), w_ffn_out[layer, 1].astype(BF16), fgain,
                 layer == depth - 1)
    kf_p, vf_p, kf_s, vf_s = fox_stacked
    heads_p = (n_fox, batch, seq, FOX_HEADS, FOX_HEAD_DIM)
    heads_s = (n_fox, dec_batch, n_new, FOX_HEADS, FOX_HEAD_DIM)
    return (x[:n_prompt].reshape(batch, seq, D_MODEL), x[n_prompt:].reshape(dec_batch, n_new, D_MODEL),
            kf_p.reshape(heads_p), vf_p.reshape(heads_p), jnp.stack(fl_p), jnp.stack(rs_p),
            kf_s.reshape(heads_s), vf_s.reshape(heads_s), jnp.stack(fl_s), jnp.stack(rs_s))
```

```python
import functools
import math

import numpy as np
import jax
import jax.numpy as jnp
from jax import lax
from jax.experimental import pallas as pl
from jax.experimental.pallas import tpu as pltpu

F32 = jnp.float32
BF16 = jnp.bfloat16

D_MODEL = 1024
CHUNK = 64
FOX_HEADS = 16
FOX_HEAD_DIM = 64
FOX_PAIRS = FOX_HEADS // 2
RET_HEADS = 8
RET_QK_DIM = 128
RET_V_DIM = 256
RET_QK_WIDTH = RET_HEADS * RET_QK_DIM
RET_V_WIDTH = RET_HEADS * RET_V_DIM
D_FF = 2816
RMS_EPS = 1e-6
GN_EPS = 1e-5
ROPE_BASE = 10000.0

LANES = 128
MXU_DIM = 256
TOKEN_TILE = 512
FFN_CHUNK = MXU_DIM
ATTN_Q = 512
ATTN_K = 256
ATTN_GROUP = 4
VMEM_LIMIT = 56 * 1024 * 1024
NEG_BIG = -1e30
LOG2E = 1.4426950408889634
FOX_Q_SCALE = FOX_HEAD_DIM ** -0.5
AUG_PIECES = 3
SKIP_MARGIN = 32.0
FROZEN_MAX_RANGE = 100.0

_NT = (((1,), (1,)), ((), ()))
_TN = (((0,), (0,)), ((), ()))


def _params(*sem):
    return pltpu.CompilerParams(dimension_semantics=sem, vmem_limit_bytes=VMEM_LIMIT)


def _resident(shape):
    nd = len(shape)
    return pl.BlockSpec(shape, lambda *_: (0,) * nd, pipeline_mode=pl.Buffered(1))


def _rmsnorm(x, gain):
    ms = jnp.mean(x * x, axis=-1, keepdims=True)
    return x * lax.rsqrt(ms + RMS_EPS) * gain


def _dot(a, b):
    return jnp.dot(a, b, preferred_element_type=F32)


def _split3(x):
    hi = x.astype(BF16)
    r = x - hi.astype(F32)
    mid = r.astype(BF16)
    lo = (r - mid.astype(F32)).astype(BF16)
    return hi, mid, lo


def _tri(n, upper=False):
    row = lax.broadcasted_iota(jnp.int32, (n, n), 0)
    col = lax.broadcasted_iota(jnp.int32, (n, n), 1)
    return jnp.where((row <= col) if upper else (row >= col), 1.0, 0.0).astype(BF16)


def _cumsum_rows(tri, x):
    hi, mid, lo = _split3(x)
    return _dot(tri, hi) + _dot(tri, mid) + _dot(tri, lo)


def _cumsum_cols(tri_upper, x):
    hi, mid, lo = _split3(x)
    return _dot(hi, tri_upper) + _dot(mid, tri_upper) + _dot(lo, tri_upper)


def _log_sigmoid(z):
    return jnp.minimum(z, 0.0) - jnp.log1p(jnp.exp(-jnp.abs(z)))


def _ffn_body(*refs, first, mixed, final, n_prompt_tiles):
    refs = list(refs)
    x_refs = [refs.pop(0) for _ in range(2 if first else 1)]
    y_ref, wmix_ref = (refs.pop(0), refs.pop(0)) if mixed else (None, None)
    g_ref, win_ref, wout_ref = refs.pop(0), refs.pop(0), refs.pop(0)
    fg_ref = refs.pop(0) if final else None
    o_refs = [refs.pop(0) for _ in range(2 if final else 1)]
    (acc_ref,) = refs
    is_prompt = pl.program_id(0) < n_prompt_tiles
    x = jnp.where(is_prompt, x_refs[0][...], x_refs[1][...]) if first else x_refs[0][...]
    if mixed:
        x = x + _dot(y_ref[...], wmix_ref[...])
    h = _rmsnorm(x, g_ref[...]).astype(BF16)
    for c in range(D_FF // FFN_CHUNK):
        lo, hi = c * FFN_CHUNK, (c + 1) * FFN_CHUNK
        gate = _dot(h, win_ref[:, lo:hi])
        up = _dot(h, win_ref[:, D_FF + lo:D_FF + hi])
        act = (gate * jax.nn.sigmoid(gate) * up).astype(BF16)
        part = _dot(act, wout_ref[lo:hi, :])
        if c == 0:
            acc_ref[...] = part
        else:
            acc_ref[...] += part
    y = x + 0.5 * acc_ref[...]
    if not final:
        o_refs[0][...] = y
        return
    y = _rmsnorm(y, fg_ref[...])

    @pl.when(is_prompt)
    def _():
        o_refs[0][...] = y

    @pl.when(jnp.logical_not(is_prompt))
    def _():
        o_refs[1][...] = y


def _ffn(xs, gain, w_in, w_out, n_prompt, n_sample, mix=None, final_gain=None):
    first, mixed, final = len(xs) == 2, mix is not None, final_gain is not None
    n = n_prompt + n_sample
    n_prompt_tiles = n_prompt // TOKEN_TILE
    tile = pl.BlockSpec((TOKEN_TILE, D_MODEL), lambda i: (i, 0))
    prompt_tile = pl.BlockSpec((TOKEN_TILE, D_MODEL), lambda i: (jnp.minimum(i, n_prompt_tiles - 1), 0))
    sample_tile = pl.BlockSpec((TOKEN_TILE, D_MODEL), lambda i: (jnp.maximum(i - n_prompt_tiles, 0), 0))
    ins, in_specs = list(xs), ([prompt_tile, sample_tile] if first else [tile])
    if mixed:
        y, w_mix = mix
        ins += [y, w_mix]
        in_specs += [pl.BlockSpec((TOKEN_TILE, y.shape[1]), lambda i: (i, 0)), _resident(w_mix.shape)]
    ins += [gain, w_in, w_out]
    in_specs += [_resident((1, D_MODEL)), _resident((D_MODEL, 2 * D_FF)), _resident((D_FF, D_MODEL))]
    if final:
        ins.append(final_gain)
        in_specs.append(_resident((1, D_MODEL)))
        out_shape = (jax.ShapeDtypeStruct((n_prompt, D_MODEL), F32), jax.ShapeDtypeStruct((n_sample, D_MODEL), F32))
        out_specs = (prompt_tile, sample_tile)
    else:
        out_shape, out_specs = jax.ShapeDtypeStruct((n, D_MODEL), F32), tile
    return pl.pallas_call(
        functools.partial(_ffn_body, first=first, mixed=mixed, final=final, n_prompt_tiles=n_prompt_tiles),
        out_shape=out_shape,
        grid=(n // TOKEN_TILE,),
        in_specs=in_specs,
        out_specs=out_specs,
        scratch_shapes=[pltpu.VMEM((TOKEN_TILE, D_MODEL), F32)],
        compiler_params=_params("arbitrary"),
        name="ffn_final" if final else ("ffn_mixed" if mixed else "ffn"),
    )(*ins)


def _aug_tables():
    sel_k = np.zeros((AUG_PIECES * LANES, D_MODEL), np.float32)
    sel_q = np.zeros((AUG_PIECES * LANES, D_MODEL), np.float32)
    const_k = np.zeros((1, D_MODEL), np.float32)
    const_q = np.zeros((1, D_MODEL), np.float32)
    for p in range(FOX_PAIRS):
        for hh in range(2):
            for piece in range(AUG_PIECES):
                src = piece * LANES + 2 * p + hh
                sel_k[src, p * LANES + 3 * hh + piece] = 1.0
                sel_q[src, p * LANES + 6 + 3 * hh + piece] = 1.0
                const_q[0, p * LANES + 3 * hh + piece] = -1.0
                const_k[0, p * LANES + 6 + 3 * hh + piece] = 1.0
    return sel_k, sel_q, const_k, const_q


def _head_rmsnorm(t, grp, gain):
    outs = []
    for c in range(D_MODEL // MXU_DIM):
        blk = t[:, c * MXU_DIM:(c + 1) * MXU_DIM]
        sq = blk * blk
        hi = sq.astype(BF16)
        lo = (sq - hi.astype(F32)).astype(BF16)
        ss = _dot(hi, grp) + _dot(lo, grp)
        outs.append(blk * lax.rsqrt(ss * (1.0 / FOX_HEAD_DIM) + RMS_EPS))
    return jnp.concatenate(outs, axis=1) * gain


def _head_rmsnorm_t(t, grp, gain):
    outs = []
    for c in range(D_MODEL // MXU_DIM):
        blk = t[c * MXU_DIM:(c + 1) * MXU_DIM, :]
        sq = blk * blk
        hi = sq.astype(BF16)
        lo = (sq - hi.astype(F32)).astype(BF16)
        ss = _dot(grp, hi) + _dot(grp, lo)
        outs.append(blk * lax.rsqrt(ss * (1.0 / FOX_HEAD_DIM) + RMS_EPS))
    return jnp.concatenate(outs, axis=0) * gain


N_FOX_PROJ_INPUTS = 15
N_FOX_SAMPLE_INPUTS = 8


def _fox_token_major(h, w_ref, wf_ref, bf_ref, kg_ref, grp, kb_o, gate_o, lf_o):
    w = D_MODEL
    k = _head_rmsnorm(_dot(h, w_ref[:, w:2 * w]), grp, kg_ref[...])
    kb_o[...] = k.astype(BF16)
    v = _dot(h, w_ref[:, 2 * w:3 * w])
    gate_o[...] = jax.nn.sigmoid(_dot(h, w_ref[:, 3 * w:4 * w])).astype(BF16)
    logf = _log_sigmoid(_dot(h, wf_ref[...]) + bf_ref[...])
    lf_o[...] = logf
    return k, v, logf


def _fox_proj_sample_body(*refs, n_stacked):
    x_ref, g_ref, w_ref, wf_ref, bf_ref, qg_ref, kg_ref, grp_ref = refs[:N_FOX_SAMPLE_INPUTS]
    qs_o, kb_o, vs_o, gate_o, lf_o, kf_o, vf_o = refs[N_FOX_SAMPLE_INPUTS + n_stacked:]
    h = _rmsnorm(x_ref[...], g_ref[...]).astype(BF16)
    grp = grp_ref[...]
    k, v, _ = _fox_token_major(h, w_ref, wf_ref, bf_ref, kg_ref, grp, kb_o, gate_o, lf_o)
    for hd in range(FOX_HEADS):
        cols = slice(hd * FOX_HEAD_DIM, (hd + 1) * FOX_HEAD_DIM)
        rows = pl.ds(hd, TOKEN_TILE, stride=FOX_HEADS)
        kf_o.at[0][rows, :] = k[:, cols]
        vf_o.at[0][rows, :] = v[:, cols]
    vs_o[...] = v.astype(BF16)
    q = _head_rmsnorm(_dot(h, w_ref[:, 0:D_MODEL]), grp, qg_ref[...])
    qs_o[...] = (q * FOX_Q_SCALE).astype(BF16)


def _fox_proj_sample(ins, stacked, layer, n_layers, tile0):
    assert len(ins) == N_FOX_SAMPLE_INPUTS
    n = ins[0].shape[0] - tile0 * TOKEN_TILE
    tile = pl.BlockSpec((TOKEN_TILE, D_MODEL), lambda i: (i, 0))
    slab = pl.BlockSpec((1, TOKEN_TILE * FOX_HEADS, FOX_HEAD_DIM), lambda i: (layer, i, 0))
    wide = jax.ShapeDtypeStruct((n, D_MODEL), BF16)
    stack = jax.ShapeDtypeStruct((n_layers, n * FOX_HEADS, FOX_HEAD_DIM), F32)
    n_out_before_stacks = 5
    return pl.pallas_call(
        functools.partial(_fox_proj_sample_body, n_stacked=len(stacked)),
        out_shape=(wide, wide, wide, wide, jax.ShapeDtypeStruct((n, LANES), F32), stack, stack),
        grid=(n // TOKEN_TILE,),
        in_specs=([pl.BlockSpec((TOKEN_TILE, D_MODEL), lambda i: (tile0 + i, 0))]
                  + [_resident(a.shape) for a in ins[1:]] + [pl.BlockSpec(memory_space=pl.ANY)] * len(stacked)),
        out_specs=(tile, tile, tile, tile, pl.BlockSpec((TOKEN_TILE, LANES), lambda i: (i, 0)), slab, slab),
        input_output_aliases={N_FOX_SAMPLE_INPUTS + s: n_out_before_stacks + s for s in range(len(stacked))},
        compiler_params=_params("parallel"),
        name="fox_proj_sample",
    )(*ins, *stacked)


def _fox_proj_body(*refs, tiles_per_seq, n_stacked):
    (x_ref, g_ref, w_ref, wf_ref, bf_ref, kg_ref, grp_ref, selk_ref, ck_ref,
     wt_ref, wft_ref, bft_ref, qgt_ref, selqt_ref, cqt_ref) = refs[:N_FOX_PROJ_INPUTS]
    (qt_o, aqt_o, vt_o, kb_o, ak_o, gate_o, lf_o, d_o, kft_o, vft_o, lft_o,
     carry_ref, carryt_ref) = refs[N_FOX_PROJ_INPUTS + n_stacked:]
    i = pl.program_id(0)
    tm, w = TOKEN_TILE, D_MODEL
    hf = _rmsnorm(x_ref[...], g_ref[...])
    h = hf.astype(BF16)
    ht = hf.T.astype(BF16)
    grp = grp_ref[...]

    @pl.when(i % tiles_per_seq == 0)
    def _():
        carry_ref[...] = jnp.zeros_like(carry_ref)
        carryt_ref[...] = jnp.zeros_like(carryt_ref)

    k, _, logf = _fox_token_major(h, w_ref, wf_ref, bf_ref, kg_ref, grp, kb_o, gate_o, lf_o)
    kft_o[0, 0] = k.T
    d = _cumsum_rows(_tri(tm), logf) + carry_ref[0:1, :]
    carry_ref[...] = jnp.broadcast_to(d[tm - 1:tm, :], carry_ref.shape)
    d_o[...] = d
    pieces = jnp.concatenate(_split3(d * LOG2E), axis=1)
    ak_o[...] = (_dot(pieces, selk_ref[...]) + ck_ref[...]).astype(BF16)

    qt = _head_rmsnorm_t(_dot(wt_ref[0:w, :], ht), grp, qgt_ref[...])
    qt_o[...] = qt.astype(BF16)
    vt = _dot(wt_ref[w:2 * w, :], ht)
    vft_o[0, 0] = vt
    vt_o[...] = vt.astype(BF16)
    logft = _log_sigmoid(_dot(wft_ref[...], ht) + bft_ref[...])
    lft_o[0, 0] = logft[0:FOX_HEADS, :]
    dt = _cumsum_cols(_tri(tm, upper=True), logft) + carryt_ref[:, 0:1]
    carryt_ref[...] = jnp.broadcast_to(dt[:, tm - 1:tm], carryt_ref.shape)
    piecest = jnp.concatenate(_split3(dt * LOG2E), axis=0)
    aqt_o[...] = (_dot(selqt_ref[...], piecest) + cqt_ref[...]).astype(BF16)


def _fox_proj(ins, stacked, layer, n_layers, n, tiles_per_seq):
    assert len(ins) == N_FOX_PROJ_INPUTS
    tps = tiles_per_seq
    batch, seq = n // (tps * TOKEN_TILE), tps * TOKEN_TILE
    tile = pl.BlockSpec((TOKEN_TILE, D_MODEL), lambda i: (i, 0))
    tile_t = pl.BlockSpec((D_MODEL, TOKEN_TILE), lambda i: (0, i))
    nar = pl.BlockSpec((TOKEN_TILE, LANES), lambda i: (i, 0))
    slab = pl.BlockSpec((1, 1, D_MODEL, TOKEN_TILE), lambda i: (layer, i // tps, 0, i % tps))
    slab_lf = pl.BlockSpec((1, 1, FOX_HEADS, TOKEN_TILE), lambda i: (layer, i // tps, 0, i % tps))
    wide = jax.ShapeDtypeStruct((n, D_MODEL), BF16)
    wide_t = jax.ShapeDtypeStruct((D_MODEL, n), BF16)
    narrow = jax.ShapeDtypeStruct((n, LANES), F32)
    stack = jax.ShapeDtypeStruct((n_layers, batch, D_MODEL, seq), F32)
    stack_lf = jax.ShapeDtypeStruct((n_layers, batch, FOX_HEADS, seq), F32)
    n_out_before_stacks = 8
    return pl.pallas_call(
        functools.partial(_fox_proj_body, tiles_per_seq=tiles_per_seq, n_stacked=len(stacked)),
        out_shape=(wide_t, wide_t, wide_t, wide, wide, wide, narrow, narrow, stack, stack, stack_lf),
        grid=(n // TOKEN_TILE,),
        in_specs=([tile] + [_resident(a.shape) for a in ins[1:]]
                  + [pl.BlockSpec(memory_space=pl.ANY)] * len(stacked)),
        out_specs=(tile_t, tile_t, tile_t, tile, tile, tile, nar, nar, slab, slab, slab_lf),
        input_output_aliases={N_FOX_PROJ_INPUTS + s: n_out_before_stacks + s for s in range(len(stacked))},
        scratch_shapes=[pltpu.VMEM((8, LANES), F32), pltpu.VMEM((LANES, LANES), F32)],
        compiler_params=_params("arbitrary"),
        name="fox_proj",
    )(*ins, *stacked)


def _fox_attn_body(dfirst_ref, dlast_ref, thr_ref, frozen_ref, qt_ref, aqt_ref, k_ref, ak_ref, vt_ref, gate_ref,
                   o_ref, qh_ref, m_ref, acc_ref, *, nq, nkb):
    tq, tk = ATTN_Q, ATTN_K
    half = FOX_HEAD_DIM
    per_q = tq // tk
    b, p, i = pl.program_id(0), pl.program_id(1), pl.program_id(2)

    qt = jnp.concatenate([qt_ref[...], aqt_ref[...]], axis=0)
    sub = lax.broadcasted_iota(jnp.int32, qt.shape, 0)
    slot = sub - LANES
    keep_a = (sub < half) | ((slot >= 0) & (slot < 3)) | ((slot >= 6) & (slot < 9))
    zero = jnp.zeros_like(qt)
    qh_ref[0] = jnp.where(keep_a, qt, zero)
    keep_b = ((sub >= half) & (sub < LANES)) | ((slot >= 3) & (slot < 6)) | ((slot >= 9) & (slot < 12))
    qh_ref[1] = jnp.where(keep_b, qt, zero)
    m_ref[...] = jnp.full(m_ref.shape, NEG_BIG, F32)
    acc_ref[...] = jnp.zeros_like(acc_ref)

    def step(j0, nblocks, hh, masked, online=True):
        rows = nblocks * tk
        off = pl.multiple_of(j0 * tk, tk)
        kk = jnp.concatenate([k_ref[pl.ds(off, rows), :], ak_ref[pl.ds(off, rows), :]], axis=1)
        vt = vt_ref[:, pl.ds(off, rows)]
        top = lax.broadcasted_iota(jnp.int32, (LANES, rows), 0) < half
        ones = jnp.ones((LANES, rows), BF16)
        vsel = jnp.where(top, vt, ones) if hh == 0 else jnp.where(top, ones, vt)
        s = _dot(kk, qh_ref[hh])
        if masked:
            assert rows == tq
            kpos = lax.broadcasted_iota(jnp.int32, (rows, tq), 0)
            qpos = lax.broadcasted_iota(jnp.int32, (rows, tq), 1)
            s = jnp.where(kpos <= qpos, s, NEG_BIG)
        m_old = m_ref[hh, 0:1, :]
        if not online:
            acc_ref[hh] += _dot(vsel, jnp.exp2(s - m_old).astype(BF16))
            return
        m_new = jnp.maximum(m_old, jnp.max(s, axis=0, keepdims=True))
        alpha = jnp.exp2(m_old - m_new)
        pr = jnp.exp2(s - m_new).astype(BF16)
        acc_ref[hh] = alpha * acc_ref[hh] + _dot(vsel, pr)
        m_ref[hh] = jnp.broadcast_to(m_new, m_ref.shape[1:])

    thr = thr_ref[0]
    frozen_ok = frozen_ref[0] != 0
    j_top = i * per_q - 1
    group = ATTN_GROUP

    for hh in range(2):
        step(i * per_q, per_q, hh, True)

        head = 2 * p + hh
        first = dfirst_ref[(b * FOX_HEADS + head) * nq + i]
        base = (b * FOX_HEADS + head) * nkb

        def near(n, first=first, base=base):
            j = jnp.maximum(j_top - n, 0)
            return (n <= j_top) & (first - dlast_ref[base + j] >= -thr)

        n_live = lax.while_loop(near, lambda n: n + 1, jnp.int32(0))

        def earlier_blocks(online, hh=hh, n_live=n_live):
            n_groups = n_live // group

            def group_body(g, carry):
                step(j_top - group * (g + 1) + 1, group, hh, False, online)
                return carry

            lax.fori_loop(0, n_groups, group_body, 0)
            done = n_groups * group
            size = group // 2
            while size >= 1:
                take = ((n_live - done) & size) != 0

                @pl.when(take)
                def _(done=done, size=size):
                    step(j_top - done - size + 1, size, hh, False, online)

                done = done + jnp.where(take, size, 0)
                size //= 2

        @pl.when(frozen_ok)
        def _():
            earlier_blocks(False)

        @pl.when(jnp.logical_not(frozen_ok))
        def _():
            earlier_blocks(True)

    acca, accb = acc_ref[0], acc_ref[1]
    ot = jnp.concatenate([acca[0:half, :] / acca[half:half + 1, :],
                          accb[half:LANES, :] / accb[0:1, :]], axis=0)
    o_ref[...] = (ot.T * gate_ref[...].astype(F32)).astype(BF16)


def _fox_attn_prompt(dfirst, dlast, thr, frozen, qt, aqt, kb, ak, vt, gate, batch, seq, n):
    nq = seq // ATTN_Q
    qspec_t = pl.BlockSpec((LANES, ATTN_Q), lambda b, p, i, *_: (p, b * nq + i))
    tok = pl.BlockSpec((ATTN_Q, LANES), lambda b, p, i, *_: (b * nq + i, p))
    kspec = pl.BlockSpec((seq, LANES), lambda b, p, i, *_: (b, p))
    vspec_t = pl.BlockSpec((LANES, seq), lambda b, p, i, *_: (p, b))
    return pl.pallas_call(
        functools.partial(_fox_attn_body, nq=nq, nkb=seq // ATTN_K),
        out_shape=jax.ShapeDtypeStruct((n, D_MODEL), BF16),
        grid_spec=pltpu.PrefetchScalarGridSpec(
            num_scalar_prefetch=4,
            grid=(batch, FOX_PAIRS, nq),
            in_specs=[qspec_t, qspec_t, kspec, kspec, vspec_t, tok],
            out_specs=tok,
            scratch_shapes=[pltpu.VMEM((2, 2 * LANES, ATTN_Q), BF16), pltpu.VMEM((2, 8, ATTN_Q), F32),
                            pltpu.VMEM((2, LANES, ATTN_Q), F32)]),
        compiler_params=_params("parallel", "parallel", "arbitrary"),
        name="fox_attn_prompt",
    )(dfirst, dlast, thr, frozen, qt, aqt, kb, ak, vt, gate)


def _pair_split(q2):
    lane = lax.broadcasted_iota(jnp.int32, q2.shape, 1)
    left = lane < FOX_HEAD_DIM
    zero = jnp.zeros_like(q2)
    return left, (jnp.where(left, q2, zero), jnp.where(left, zero, q2))


def _sample_bias_body(cl_ref, ln_ref, rt_o, ncol_o, nt_o):
    past = cl_ref.shape[2]
    n_new = ln_ref.shape[0]
    tri = _tri(MXU_DIM, upper=True)
    carry = jnp.zeros((FOX_HEADS, 1), F32)
    prefix = []
    for b in range(past // MXU_DIM):
        p = _cumsum_cols(tri, cl_ref[0, :, b * MXU_DIM:(b + 1) * MXU_DIM]) + carry
        carry = p[:, MXU_DIM - 1:MXU_DIM]
        prefix.append(p)
    for b, p in enumerate(prefix):
        rt_o[0, :, b * MXU_DIM:(b + 1) * MXU_DIM] = carry - p
    ln =jnp.concatenate([ln_ref[...], jnp.zeros((LANES - n_new, LANES), F32)], axis=0)
    nn = _cumsum_rows(_tri(LANES), ln)
    ncol_o[0] = nn[0:n_new, :]
    nt_o[0] = nn.T[0:FOX_HEADS, 0:n_new]


def _sample_bias(cl_t, lf, n_new, row0, dec_batch, layer):
    past = cl_t.shape[2]
    blk0 = row0 // n_new
    return pl.pallas_call(
        _sample_bias_body,
        out_shape=(jax.ShapeDtypeStruct((dec_batch, FOX_HEADS, past), F32),
                   jax.ShapeDtypeStruct((dec_batch, n_new, LANES), F32),
                   jax.ShapeDtypeStruct((dec_batch, FOX_HEADS, n_new), F32)),
        grid=(dec_batch,),
        in_specs=[pl.BlockSpec((1, FOX_HEADS, past), lambda b: (layer * dec_batch + b, 0, 0)),
                  pl.BlockSpec((n_new, LANES), lambda b: (blk0 + b, 0))],
        out_specs=(pl.BlockSpec((1, FOX_HEADS, past), lambda b: (b, 0, 0)),
                   pl.BlockSpec((1, n_new, LANES), lambda b: (b, 0, 0)),
                   pl.BlockSpec((1, FOX_HEADS, n_new), lambda b: (b, 0, 0))),
        compiler_params=_params("parallel"),
        name="fox_sample_bias",
    )(cl_t, lf)


def _fox_sample_body(q_ref, vn_ref, kn_ref, gate_ref, kc_ref, vc_ref, rt_ref, nc_ref, nt_ref,
                     y_any, o_ref):
    del y_any
    n_new = q_ref.shape[0]
    left, qs = _pair_split(q_ref[...])
    kc = kc_ref[0].astype(BF16)
    vc = vc_ref[0].astype(BF16)
    kn = kn_ref[...]
    vn = vn_ref[...]
    rt = rt_ref[0, 0]
    nc = nc_ref[0, 0]
    nt = nt_ref[0, 0]
    row = lax.broadcasted_iota(jnp.int32, (n_new, n_new), 0)
    col = lax.broadcasted_iota(jnp.int32, (n_new, n_new), 1)
    outs = []
    for hh in range(2):
        nq = nc[:, hh:hh + 1]
        s1 = _dot(qs[hh], kc) + (nq + rt[hh:hh + 1, :])
        s2 = lax.dot_general(qs[hh], kn, _NT, preferred_element_type=F32) + (nq - nt[hh:hh + 1, :])
        s2 = jnp.where(col <= row, s2, NEG_BIG)
        m = jnp.maximum(jnp.max(s1, axis=1, keepdims=True), jnp.max(s2, axis=1, keepdims=True))
        p1 = jnp.exp(s1 - m)
        p2 = jnp.exp(s2 - m)
        l = jnp.sum(p1, axis=1, keepdims=True) + jnp.sum(p2, axis=1, keepdims=True)
        pv1 = lax.dot_general(p1.astype(BF16), vc, _NT, preferred_element_type=F32)
        outs.append((pv1 + _dot(p2.astype(BF16), vn)) / l)
    o = jnp.where(left, outs[0], outs[1])
    o_ref[...] = (o * gate_ref[...].astype(F32)).astype(BF16)


def _fox_attn_sample(qs, vs, kb, gate, kc, vc, rt, nc, nt, y_all, n_new, row0, layer):
    dec_batch, _, _, past = rt.shape
    blk0 = row0 // n_new
    own = pl.BlockSpec((n_new, LANES), lambda b, p: (b, p))
    new = pl.BlockSpec((n_new, LANES), lambda b, p: (blk0 + b, p))
    cache = pl.BlockSpec((1, LANES, past), lambda b, p: (layer * dec_batch + b, p, 0))
    return pl.pallas_call(
        _fox_sample_body,
        out_shape=jax.ShapeDtypeStruct(y_all.shape, BF16),
        grid=(dec_batch, FOX_PAIRS),
        in_specs=[own, own, own, own, cache, cache,
                  pl.BlockSpec((1, 1, 2, past), lambda b, p: (b, p, 0, 0)),
                  pl.BlockSpec((1, 1, n_new, 2), lambda b, p: (b, p, 0, 0)),
                  pl.BlockSpec((1, 1, 2, n_new), lambda b, p: (b, p, 0, 0)),
                  pl.BlockSpec(memory_space=pl.ANY)],
        out_specs=new,
        input_output_aliases={9: 0},
        compiler_params=_params("parallel", "parallel"),
        name="fox_attn_sample",
    )(qs, vs, kb, gate, kc, vc, rt, nc, nt, y_all)


def _ret_log_gammas():
    return [float(v) for v in np.log1p(-np.exp2(-5.0 - np.arange(RET_HEADS, dtype=np.float32)))]


def _ret_proj_body(x_ref, g_ref, w_ref, cos_ref, sin_ref, q_o, k_o, v_o, sg_o):
    h = _rmsnorm(x_ref[...], g_ref[...]).astype(BF16)
    cos = cos_ref[...]
    sin = sin_ref[...]
    a, b2 = RET_QK_WIDTH, 2 * RET_QK_WIDTH

    def rotary(t, scale):
        outs = []
        for hd in range(RET_HEADS):
            th = t[:, hd * RET_QK_DIM:(hd + 1) * RET_QK_DIM]
            outs.append((th * cos + pltpu.roll(th, RET_QK_DIM // 2, 1) * sin) * scale)
        return jnp.concatenate(outs, axis=1).astype(BF16)

    q_o[...] = rotary(_dot(h, w_ref[:, 0:a]), 1.0)
    k_o[...] = rotary(_dot(h, w_ref[:, a:b2]), RET_QK_DIM ** -0.5)
    v_o[...] = _dot(h, w_ref[:, b2:b2 + RET_V_WIDTH]).astype(BF16)
    g = _dot(h, w_ref[:, b2 + RET_V_WIDTH:b2 + 2 * RET_V_WIDTH])
    sg_o[...] = (g * jax.nn.sigmoid(g)).astype(BF16)


def _ret_proj(x, gain, w, cos, sin):
    n = x.shape[0]
    tile = lambda width: pl.BlockSpec((TOKEN_TILE, width), lambda i: (i, 0))
    return pl.pallas_call(
        _ret_proj_body,
        out_shape=(jax.ShapeDtypeStruct((n, RET_QK_WIDTH), BF16), jax.ShapeDtypeStruct((n, RET_QK_WIDTH), BF16),
                   jax.ShapeDtypeStruct((n, RET_V_WIDTH), BF16), jax.ShapeDtypeStruct((n, RET_V_WIDTH), BF16)),
        grid=(n // TOKEN_TILE,),
        in_specs=[tile(D_MODEL), _resident((1, D_MODEL)), _resident(w.shape), tile(LANES), tile(LANES)],
        out_specs=(tile(RET_QK_WIDTH), tile(RET_QK_WIDTH), tile(RET_V_WIDTH), tile(RET_V_WIDTH)),
        compiler_params=_params("parallel"),
        name="ret_proj",
    )(x, gain, w, cos, sin)


def _ret_decay_matrix(t, lg):
    row = lax.broadcasted_iota(jnp.int32, (t, t), 0)
    col = lax.broadcasted_iota(jnp.int32, (t, t), 1)
    dist = jnp.abs(row - col).astype(F32)
    visible = (col // CHUNK) <= (row // CHUNK)
    return jnp.where(visible, jnp.exp(dist * lg), 0.0)


def _ret_tile(q_ref, k_ref, v_ref, sg_ref, y_ref, state_in, state_out, decay, t):
    lgs = _ret_log_gammas()
    pos = lax.broadcasted_iota(jnp.int32, (t, 1), 0).astype(F32)
    for hd in range(RET_HEADS):
        lg = lgs[hd]
        qs = slice(hd * RET_QK_DIM, (hd + 1) * RET_QK_DIM)
        vs = slice(hd * RET_V_DIM, (hd + 1) * RET_V_DIM)
        qh = q_ref[:, qs]
        kh = k_ref[:, qs]
        vh = v_ref[:, vs]
        state = state_in(hd)
        sc = lax.dot_general(qh, kh, _NT, preferred_element_type=F32) * decay(hd)
        o = _dot(sc.astype(BF16), vh) + _dot(qh, state.astype(BF16)) * jnp.exp(pos * lg)
        kd = (kh.astype(F32) * jnp.exp((t - pos) * lg)).astype(BF16)
        state_out(hd, state * math.exp(t * lg) + lax.dot_general(kd, vh, _TN, preferred_element_type=F32))
        mu = jnp.mean(o, axis=-1, keepdims=True)
        dev = o - mu
        var = jnp.mean(dev * dev, axis=-1, keepdims=True)
        y_ref[:, vs] = (sg_ref[:, vs].astype(F32) * (dev * lax.rsqrt(var + GN_EPS))).astype(BF16)


def _ret_prompt_body(q_ref, k_ref, v_ref, sg_ref, y_ref, st_ref, decay_ref):
    t = TOKEN_TILE

    @pl.when((pl.program_id(0) == 0) & (pl.program_id(1) == 0))
    def _():
        for hd, lg in enumerate(_ret_log_gammas()):
            decay_ref[hd] = _ret_decay_matrix(t, lg)

    @pl.when(pl.program_id(1) == 0)
    def _():
        st_ref[...] = jnp.zeros_like(st_ref)

    def state_out(hd, val):
        st_ref[0, hd] = val

    _ret_tile(q_ref, k_ref, v_ref, sg_ref, y_ref, lambda hd: st_ref[0, hd], state_out,
              lambda hd: decay_ref[hd], t)


def _ret_prompt(q, k, v, sg, batch, seq):
    nt = seq // TOKEN_TILE
    tile = lambda width: pl.BlockSpec((TOKEN_TILE, width), lambda b, i: (b * nt + i, 0))
    return pl.pallas_call(
        _ret_prompt_body,
        out_shape=(jax.ShapeDtypeStruct((q.shape[0], RET_V_WIDTH), BF16),
                   jax.ShapeDtypeStruct((batch, RET_HEADS, RET_QK_DIM, RET_V_DIM), F32)),
        grid=(batch, nt),
        in_specs=[tile(RET_QK_WIDTH), tile(RET_QK_WIDTH), tile(RET_V_WIDTH), tile(RET_V_WIDTH)],
        out_specs=(tile(RET_V_WIDTH),
                   pl.BlockSpec((1, RET_HEADS, RET_QK_DIM, RET_V_DIM), lambda b, i: (b, 0, 0, 0))),
        scratch_shapes=[pltpu.VMEM((RET_HEADS, TOKEN_TILE, TOKEN_TILE), F32)],
        compiler_params=_params("arbitrary", "arbitrary"),
        name="ret_prompt",
    )(q, k, v, sg)


def _ret_sample_body(q_ref, k_ref, v_ref, sg_ref, st_in_ref, y_any, y_ref, st_out_ref):
    del y_any
    t = q_ref.shape[0]
    lgs = _ret_log_gammas()

    def state_out(hd, val):
        st_out_ref[0, hd] = val

    _ret_tile(q_ref, k_ref, v_ref, sg_ref, y_ref, lambda hd: st_in_ref[0, hd], state_out,
              lambda hd: _ret_decay_matrix(t, lgs[hd]), t)


def _ret_sample(q, k, v, sg, state, y_all, n_new, row0):
    dec_batch = state.shape[0]
    blk0 = row0 // n_new
    tile = lambda width: pl.BlockSpec((n_new, width), lambda b: (blk0 + b, 0))
    st = pl.BlockSpec((1, RET_HEADS, RET_QK_DIM, RET_V_DIM), lambda b: (b, 0, 0, 0))
    return pl.pallas_call(
        _ret_sample_body,
        out_shape=(jax.ShapeDtypeStruct(y_all.shape, BF16), jax.ShapeDtypeStruct(state.shape, F32)),
        grid=(dec_batch,),
        in_specs=[tile(RET_QK_WIDTH), tile(RET_QK_WIDTH), tile(RET_V_WIDTH), tile(RET_V_WIDTH), st,
                  pl.BlockSpec(memory_space=pl.ANY)],
        out_specs=(tile(RET_V_WIDTH), st),
        input_output_aliases={5: 0},
        compiler_params=_params("parallel"),
        name="ret_sample",
    )(q, k, v, sg, state, y_all)


def _rotary_tables(pos):
    half = RET_QK_DIM // 2
    inv_freq = ROPE_BASE ** (-jnp.linspace(0.0, 1.0, half, dtype=F32))
    ang = pos[:, None] * inv_freq[None, :]
    cos, sin = jnp.cos(ang), jnp.sin(ang)
    return jnp.concatenate([cos, cos], axis=1), jnp.concatenate([-sin, sin], axis=1)


def _head_tiled(gain):
    return jnp.tile(gain.astype(F32), FOX_HEADS)


def kernel(x_prompt, x_sample, cache_fox_k, cache_fox_v, cache_fox_logf, state_ret, norm_gains, w_ffn_in,
           w_ffn_out, fox_w_in, fox_b_f, fox_q_gain, fox_k_gain, fox_w_out, ret_w_in, ret_w_out, final_gain):
    batch, seq, _ = x_prompt.shape
    dec_batch, n_new, _ = x_sample.shape
    past = cache_fox_k.shape[2]
    depth = norm_gains.shape[0]
    n_prompt = batch * seq
    n_sample = dec_batch * n_new
    assert seq % TOKEN_TILE == 0 and n_sample == TOKEN_TILE and TOKEN_TILE % n_new == 0
    assert seq % ATTN_Q == 0 and ATTN_Q % ATTN_K == 0 and past % MXU_DIM == 0
    assert TOKEN_TILE % CHUNK == 0 and n_new <= CHUNK
    n_fox = (depth + 1) // 2

    xs = (x_prompt.reshape(n_prompt, D_MODEL), x_sample.reshape(n_sample, D_MODEL))
    pos =jnp.concatenate([jnp.tile(jnp.arange(seq, dtype=F32), batch),
                           jnp.tile(past + jnp.arange(n_new, dtype=F32), dec_batch)])
    cos, sin = _rotary_tables(pos)
    grp_idx = np.arange(MXU_DIM) // FOX_HEAD_DIM
    grp = jnp.asarray(grp_idx[:, None] == grp_idx[None, :], dtype=BF16)
    sel_k, sel_q, const_k, const_q = _aug_tables()
    selk, ck = jnp.asarray(sel_k, BF16), jnp.asarray(const_k, F32)
    selqt, cqt = jnp.asarray(sel_q.T, BF16), jnp.asarray(const_q.T, F32)
    fgain = final_gain.reshape(1, D_MODEL)

    feature_major = lambda c: c.transpose(0, 1, 3, 4, 2).reshape(n_fox * dec_batch, D_MODEL, past)
    cache_k_t, cache_v_t = feature_major(cache_fox_k), feature_major(cache_fox_v)
    cache_lf_t = cache_fox_logf.transpose(0, 1, 3, 2).reshape(n_fox * dec_batch, FOX_HEADS, past)
    rs_p, fl_s, rs_s = [], [], []
    fox_stacked = ()
    for layer in range(depth):
        g = norm_gains[layer].reshape(3, 1, D_MODEL)
        x = _ffn(xs, g[0], w_ffn_in[layer, 0].astype(BF16), w_ffn_out[layer, 0].astype(BF16), n_prompt, n_sample)
        j = layer // 2
        if layer % 2 == 0:
            w = fox_w_in[j]
            wf = jnp.pad(w[:, 4 * D_MODEL:], ((0, 0), (0, LANES - FOX_HEADS)))
            bf = jnp.pad(fox_b_f[j], (0, LANES - FOX_HEADS))
            wt = jnp.concatenate([w[:, 0:D_MODEL], w[:, 2 * D_MODEL:3 * D_MODEL]], axis=1).T
            qg, kg = _head_tiled(fox_q_gain[j]), _head_tiled(fox_k_gain[j])
            token_major = (x, g[1], w[:, :4 * D_MODEL].astype(BF16), wf.astype(BF16), bf.reshape(1, LANES))
            kg2, qg2 = kg.reshape(1, D_MODEL), qg.reshape(1, D_MODEL)
            proj_ins = token_major + (
                kg2, grp, selk, ck, wt.astype(BF16), wf.T.astype(BF16), bf.reshape(LANES, 1),
                (qg * (FOX_Q_SCALE * LOG2E)).reshape(D_MODEL, 1), selqt, cqt)
            qt, aqt, vt, kb, ak, gate, _, d, *stk_p = _fox_proj(
                proj_ins, fox_stacked[:3], j, n_fox, n_prompt, seq // TOKEN_TILE)
            qs, ks, vs, gate_s, lf_s, *stk_s = _fox_proj_sample(
                token_major + (qg2, kg2, grp), fox_stacked[3:], j, n_fox, n_prompt // TOKEN_TILE)
            fox_stacked = (*stk_p, *stk_s)
            dh = d[:, :FOX_HEADS].reshape(batch, seq, FOX_HEADS)
            dfirst = dh[:, 0::ATTN_Q, :].transpose(0, 2, 1).reshape(-1)
            dlast = dh[:, ATTN_K - 1::ATTN_K, :].transpose(0, 2, 1).reshape(-1)
            bound = 1.01 * (FOX_HEAD_DIM ** 0.5) * jnp.max(jnp.abs(fox_q_gain[j])) * jnp.max(jnp.abs(fox_k_gain[j]))
            thr = (2.0 * bound + SKIP_MARGIN).reshape(1).astype(F32)
            frozen = (2.0 * LOG2E * bound < FROZEN_MAX_RANGE).reshape(1).astype(jnp.int32)
            y = _fox_attn_prompt(dfirst, dlast, thr, frozen, qt, aqt, kb, ak, vt, gate, batch, seq,
                                 n_prompt + n_sample)
            rt, nc, nt = _sample_bias(cache_lf_t, lf_s, n_new, 0, dec_batch, j)
            y = _fox_attn_sample(
                qs, vs, ks, gate_s, cache_k_t, cache_v_t, rt.reshape(dec_batch, FOX_PAIRS, 2, past),
                nc[:, :, :FOX_HEADS].reshape(dec_batch, n_new, FOX_PAIRS, 2).transpose(0, 2, 1, 3),
                nt.reshape(dec_batch, FOX_PAIRS, 2, n_new), y, n_new, n_prompt, j)
            mix = (y, fox_w_out[j].astype(BF16))
            fl_s.append(lf_s[:, :FOX_HEADS].reshape(dec_batch, n_new, FOX_HEADS))
        else:
            q, k, v, sg = _ret_proj(x, g[1], ret_w_in[j].astype(BF16), cos, sin)
            y, st_p = _ret_prompt(q, k, v, sg, batch, seq)
            y, st_s = _ret_sample(q, k, v, sg, state_ret[j], y, n_new, n_prompt)
            mix = (y, ret_w_out[j].astype(BF16))
            rs_p.append(st_p)
            rs_s.append(st_s)
        x = _ffn((x,), g[2], w_ffn_in[layer, 1].astype(BF16), w_ffn_out[layer, 1].astype(BF16), n_prompt, n_sample,
                 mix=mix, final_gain=fgain if layer == depth - 1 else None)
        xs = (x,)
    y_prompt, y_sample = x
    kft_p, vft_p, lft_p, kf_s, vf_s = fox_stacked
    per_head = lambda a: a.reshape(n_fox, batch, FOX_HEADS, FOX_HEAD_DIM, seq).transpose(0, 1, 4, 2, 3)
    heads_s = (n_fox, dec_batch, n_new, FOX_HEADS, FOX_HEAD_DIM)
    return (y_prompt.reshape(batch, seq, D_MODEL), y_sample.reshape(dec_batch, n_new, D_MODEL),
            per_head(kft_p), per_head(vft_p), lft_p.transpose(0, 1, 3, 2), jnp.stack(rs_p),
            kf_s.reshape(heads_s), vf_s.reshape(heads_s), jnp.stack(fl_s), jnp.stack(rs_s))
```

```python
import functools
import math

import numpy as np
import jax
import jax.numpy as jnp
from jax import lax
from jax.experimental import pallas as pl
from jax.experimental.pallas import tpu as pltpu

F32 = jnp.float32
BF16 = jnp.bfloat16

D_MODEL = 1024
CHUNK = 64
FOX_HEADS = 16
FOX_HEAD_DIM = 64
FOX_PAIRS = FOX_HEADS // 2
RET_HEADS = 8
RET_QK_DIM = 128
RET_V_DIM = 256
RET_QK_WIDTH = RET_HEADS * RET_QK_DIM
RET_V_WIDTH = RET_HEADS * RET_V_DIM
D_FF = 2816
RMS_EPS = 1e-6
GN_EPS = 1e-5
ROPE_BASE = 10000.0

LANES = 128
MXU_DIM = 256
TOKEN_TILE = 512
FFN_CHUNK = MXU_DIM
ATTN_Q = 512
ATTN_K = 256
ATTN_GROUP = 4
VMEM_LIMIT = 56 * 1024 * 1024
NEG_BIG = -1e30
LOG2E = 1.4426950408889634
FOX_Q_SCALE = FOX_HEAD_DIM ** -0.5
AUG_PIECES = 3
SKIP_MARGIN = 32.0
FROZEN_MAX_RANGE = 100.0

_NT = (((1,), (1,)), ((), ()))
_TN = (((0,), (0,)), ((), ()))


def _params(*sem):
    return pltpu.CompilerParams(dimension_semantics=sem, vmem_limit_bytes=VMEM_LIMIT)


def _resident(shape):
    nd = len(shape)
    return pl.BlockSpec(shape, lambda *_: (0,) * nd, pipeline_mode=pl.Buffered(1))


def _rmsnorm(x, gain):
    ms = jnp.mean(x * x, axis=-1, keepdims=True)
    return x * lax.rsqrt(ms + RMS_EPS) * gain


def _dot(a, b):
    return jnp.dot(a, b, preferred_element_type=F32)


def _split3(x):
    hi = x.astype(BF16)
    r = x - hi.astype(F32)
    mid = r.astype(BF16)
    lo = (r - mid.astype(F32)).astype(BF16)
    return hi, mid, lo


def _tri(n, upper=False):
    row = lax.broadcasted_iota(jnp.int32, (n, n), 0)
    col = lax.broadcasted_iota(jnp.int32, (n, n), 1)
    return jnp.where((row <= col) if upper else (row >= col), 1.0, 0.0).astype(BF16)


def _cumsum_rows(tri, x):
    hi, mid, lo = _split3(x)
    return _dot(tri, hi) + _dot(tri, mid) + _dot(tri, lo)


def _cumsum_cols(tri_upper, x):
    hi, mid, lo = _split3(x)
    return _dot(hi, tri_upper) + _dot(mid, tri_upper) + _dot(lo, tri_upper)


def _log_sigmoid(z):
    return jnp.minimum(z, 0.0) - jnp.log1p(jnp.exp(-jnp.abs(z)))


def _ffn_body(*refs, first, mixed, final, n_prompt_tiles):
    refs = list(refs)
    x_refs = [refs.pop(0) for _ in range(2 if first else 1)]
    y_ref, wmix_ref = (refs.pop(0), refs.pop(0)) if mixed else (None, None)
    g_ref, win_ref, wout_ref = refs.pop(0), refs.pop(0), refs.pop(0)
    fg_ref = refs.pop(0) if final else None
    o_refs = [refs.pop(0) for _ in range(2 if final else 1)]
    (acc_ref,) = refs
    is_prompt = pl.program_id(0) < n_prompt_tiles
    x = jnp.where(is_prompt, x_refs[0][...], x_refs[1][...]) if first else x_refs[0][...]
    if mixed:
        x = x + _dot(y_ref[...], wmix_ref[...])
    h = _rmsnorm(x, g_ref[...]).astype(BF16)
    for c in range(D_FF // FFN_CHUNK):
        lo, hi = c * FFN_CHUNK, (c + 1) * FFN_CHUNK
        gate = _dot(h, win_ref[0, 0, :, lo:hi])
        up = _dot(h, win_ref[0, 0, :, D_FF + lo:D_FF + hi])
        act = (gate * jax.nn.sigmoid(gate) * up).astype(BF16)
        part = _dot(act, wout_ref[0, 0, lo:hi, :])
        if c == 0:
            acc_ref[...] = part
        else:
            acc_ref[...] += part
    y = x + 0.5 * acc_ref[...]
    if not final:
        o_refs[0][...] = y
        return
    y = _rmsnorm(y, fg_ref[...])

    @pl.when(is_prompt)
    def _():
        o_refs[0][...] = y

    @pl.when(jnp.logical_not(is_prompt))
    def _():
        o_refs[1][...] = y


def _ffn(xs, gain, w_in, w_out, which, n_prompt, n_sample, mix=None, final_gain=None):
    first, mixed, final = len(xs) == 2, mix is not None, final_gain is not None
    n = n_prompt + n_sample
    n_prompt_tiles = n_prompt // TOKEN_TILE
    tile = pl.BlockSpec((TOKEN_TILE, D_MODEL), lambda i: (i, 0))
    prompt_tile = pl.BlockSpec((TOKEN_TILE, D_MODEL), lambda i: (jnp.minimum(i, n_prompt_tiles - 1), 0))
    sample_tile = pl.BlockSpec((TOKEN_TILE, D_MODEL), lambda i: (jnp.maximum(i - n_prompt_tiles, 0), 0))
    ins, in_specs = list(xs), ([prompt_tile, sample_tile] if first else [tile])
    if mixed:
        y, w_mix = mix
        ins += [y, w_mix]
        in_specs += [pl.BlockSpec((TOKEN_TILE, y.shape[1]), lambda i: (i, 0)), _resident(w_mix.shape)]
    ins += [gain, w_in, w_out]
    pick = lambda *_: (*which, 0, 0)
    in_specs += [_resident((1, D_MODEL)),
                 pl.BlockSpec((1, 1, D_MODEL, 2 * D_FF), pick, pipeline_mode=pl.Buffered(1)),
                 pl.BlockSpec((1, 1, D_FF, D_MODEL), pick, pipeline_mode=pl.Buffered(1))]
    if final:
        ins.append(final_gain)
        in_specs.append(_resident((1, D_MODEL)))
        out_shape = (jax.ShapeDtypeStruct((n_prompt, D_MODEL), F32), jax.ShapeDtypeStruct((n_sample, D_MODEL), F32))
        out_specs = (prompt_tile, sample_tile)
    else:
        out_shape, out_specs = jax.ShapeDtypeStruct((n, D_MODEL), F32), tile
    return pl.pallas_call(
        functools.partial(_ffn_body, first=first, mixed=mixed, final=final, n_prompt_tiles=n_prompt_tiles),
        out_shape=out_shape,
        grid=(n // TOKEN_TILE,),
        in_specs=in_specs,
        out_specs=out_specs,
        scratch_shapes=[pltpu.VMEM((TOKEN_TILE, D_MODEL), F32)],
        compiler_params=_params("arbitrary"),
        name="ffn_final" if final else ("ffn_mixed" if mixed else "ffn"),
    )(*ins)


def _aug_tables():
    sel_k = np.zeros((AUG_PIECES * LANES, D_MODEL), np.float32)
    sel_q = np.zeros((AUG_PIECES * LANES, D_MODEL), np.float32)
    const_k = np.zeros((1, D_MODEL), np.float32)
    const_q = np.zeros((1, D_MODEL), np.float32)
    for p in range(FOX_PAIRS):
        for hh in range(2):
            for piece in range(AUG_PIECES):
                src = piece * LANES + 2 * p + hh
                sel_k[src, p * LANES + 3 * hh + piece] = 1.0
                sel_q[src, p * LANES + 6 + 3 * hh + piece] = 1.0
                const_q[0, p * LANES + 3 * hh + piece] = -1.0
                const_k[0, p * LANES + 6 + 3 * hh + piece] = 1.0
    return sel_k, sel_q, const_k, const_q


def _head_rmsnorm(t, grp, gain):
    outs = []
    for c in range(D_MODEL // MXU_DIM):
        blk = t[:, c * MXU_DIM:(c + 1) * MXU_DIM]
        sq = blk * blk
        hi = sq.astype(BF16)
        lo = (sq - hi.astype(F32)).astype(BF16)
        ss = _dot(hi, grp) + _dot(lo, grp)
        outs.append(blk * lax.rsqrt(ss * (1.0 / FOX_HEAD_DIM) + RMS_EPS))
    return jnp.concatenate(outs, axis=1) * gain


def _head_rmsnorm_t(t, grp, gain):
    outs = []
    for c in range(D_MODEL // MXU_DIM):
        blk = t[c * MXU_DIM:(c + 1) * MXU_DIM, :]
        sq = blk * blk
        hi = sq.astype(BF16)
        lo = (sq - hi.astype(F32)).astype(BF16)
        ss = _dot(grp, hi) + _dot(grp, lo)
        outs.append(blk * lax.rsqrt(ss * (1.0 / FOX_HEAD_DIM) + RMS_EPS))
    return jnp.concatenate(outs, axis=0) * gain


N_FOX_PROJ_INPUTS = 15
N_FOX_SAMPLE_INPUTS = 8


def _fox_token_major(h, w_ref, wf_ref, bf_ref, kg_ref, grp, kb_o, gate_o, lf_o):
    w = D_MODEL
    k = _head_rmsnorm(_dot(h, w_ref[:, w:2 * w]), grp, kg_ref[...])
    kb_o[...] = k.astype(BF16)
    v = _dot(h, w_ref[:, 2 * w:3 * w])
    gate_o[...] = jax.nn.sigmoid(_dot(h, w_ref[:, 3 * w:4 * w])).astype(BF16)
    logf = _log_sigmoid(_dot(h, wf_ref[...]) + bf_ref[...])
    lf_o[...] = logf
    return k, v, logf


def _fox_proj_sample_body(*refs, n_stacked):
    x_ref, g_ref, w_ref, wf_ref, bf_ref, qg_ref, kg_ref, grp_ref = refs[:N_FOX_SAMPLE_INPUTS]
    qs_o, kb_o, vs_o, gate_o, lf_o, kf_o, vf_o = refs[N_FOX_SAMPLE_INPUTS + n_stacked:]
    h = _rmsnorm(x_ref[...], g_ref[...]).astype(BF16)
    grp = grp_ref[...]
    k, v, _ = _fox_token_major(h, w_ref, wf_ref, bf_ref, kg_ref, grp, kb_o, gate_o, lf_o)
    for hd in range(FOX_HEADS):
        cols = slice(hd * FOX_HEAD_DIM, (hd + 1) * FOX_HEAD_DIM)
        rows = pl.ds(hd, TOKEN_TILE, stride=FOX_HEADS)
        kf_o.at[0][rows, :] = k[:, cols]
        vf_o.at[0][rows, :] = v[:, cols]
    vs_o[...] = v.astype(BF16)
    q = _head_rmsnorm(_dot(h, w_ref[:, 0:D_MODEL]), grp, qg_ref[...])
    qs_o[...] = (q * FOX_Q_SCALE).astype(BF16)


def _fox_proj_sample(ins, stacked, layer, n_layers, tile0):
    assert len(ins) == N_FOX_SAMPLE_INPUTS
    n = ins[0].shape[0] - tile0 * TOKEN_TILE
    tile = pl.BlockSpec((TOKEN_TILE, D_MODEL), lambda i: (i, 0))
    slab = pl.BlockSpec((1, TOKEN_TILE * FOX_HEADS, FOX_HEAD_DIM), lambda i: (layer, i, 0))
    wide = jax.ShapeDtypeStruct((n, D_MODEL), BF16)
    stack = jax.ShapeDtypeStruct((n_layers, n * FOX_HEADS, FOX_HEAD_DIM), F32)
    n_out_before_stacks = 5
    return pl.pallas_call(
        functools.partial(_fox_proj_sample_body, n_stacked=len(stacked)),
        out_shape=(wide, wide, wide, wide, jax.ShapeDtypeStruct((n, LANES), F32), stack, stack),
        grid=(n // TOKEN_TILE,),
        in_specs=([pl.BlockSpec((TOKEN_TILE, D_MODEL), lambda i: (tile0 + i, 0))]
                  + [_resident(a.shape) for a in ins[1:]] + [pl.BlockSpec(memory_space=pl.ANY)] * len(stacked)),
        out_specs=(tile, tile, tile, tile, pl.BlockSpec((TOKEN_TILE, LANES), lambda i: (i, 0)), slab, slab),
        input_output_aliases={N_FOX_SAMPLE_INPUTS + s: n_out_before_stacks + s for s in range(len(stacked))},
        compiler_params=_params("parallel"),
        name="fox_proj_sample",
    )(*ins, *stacked)


def _fox_proj_body(*refs, tiles_per_seq, n_stacked):
    (x_ref, g_ref, w_ref, wf_ref, bf_ref, kg_ref, grp_ref, selk_ref, ck_ref,
     wt_ref, wft_ref, bft_ref, qgt_ref, selqt_ref, cqt_ref) = refs[:N_FOX_PROJ_INPUTS]
    (qt_o, aqt_o, vt_o, kb_o, ak_o, gate_o, lf_o, d_o, kft_o, vft_o, lft_o,
     carry_ref, carryt_ref) = refs[N_FOX_PROJ_INPUTS + n_stacked:]
    i = pl.program_id(0)
    tm, w = TOKEN_TILE, D_MODEL
    hf = _rmsnorm(x_ref[...], g_ref[...])
    h = hf.astype(BF16)
    ht = hf.T.astype(BF16)
    grp = grp_ref[...]

    @pl.when(i % tiles_per_seq == 0)
    def _():
        carry_ref[...] = jnp.zeros_like(carry_ref)
        carryt_ref[...] = jnp.zeros_like(carryt_ref)

    k, _, logf = _fox_token_major(h, w_ref, wf_ref, bf_ref, kg_ref, grp, kb_o, gate_o, lf_o)
    kft_o[0, 0] = k.T
    d = _cumsum_rows(_tri(tm), logf) + carry_ref[0:1, :]
    carry_ref[...] = jnp.broadcast_to(d[tm - 1:tm, :], carry_ref.shape)
    d_o[...] = d
    pieces = jnp.concatenate(_split3(d * LOG2E), axis=1)
    ak_o[...] = (_dot(pieces, selk_ref[...]) + ck_ref[...]).astype(BF16)

    qt = _head_rmsnorm_t(_dot(wt_ref[0:w, :], ht), grp, qgt_ref[...])
    qt_o[...] = qt.astype(BF16)
    vt = _dot(wt_ref[w:2 * w, :], ht)
    vft_o[0, 0] = vt
    vt_o[...] = vt.astype(BF16)
    logft = _log_sigmoid(_dot(wft_ref[...], ht) + bft_ref[...])
    lft_o[0, 0] = logft[0:FOX_HEADS, :]
    dt = _cumsum_cols(_tri(tm, upper=True), logft) + carryt_ref[:, 0:1]
    carryt_ref[...] = jnp.broadcast_to(dt[:, tm - 1:tm], carryt_ref.shape)
    piecest = jnp.concatenate(_split3(dt * LOG2E), axis=0)
    aqt_o[...] = (_dot(selqt_ref[...], piecest) + cqt_ref[...]).astype(BF16)


def _fox_proj(ins, stacked, layer, n_layers, n, tiles_per_seq):
    assert len(ins) == N_FOX_PROJ_INPUTS
    tps = tiles_per_seq
    batch, seq = n // (tps * TOKEN_TILE), tps * TOKEN_TILE
    tile = pl.BlockSpec((TOKEN_TILE, D_MODEL), lambda i: (i, 0))
    tile_t = pl.BlockSpec((D_MODEL, TOKEN_TILE), lambda i: (0, i))
    nar = pl.BlockSpec((TOKEN_TILE, LANES), lambda i: (i, 0))
    slab = pl.BlockSpec((1, 1, D_MODEL, TOKEN_TILE), lambda i: (layer, i // tps, 0, i % tps))
    slab_lf = pl.BlockSpec((1, 1, FOX_HEADS, TOKEN_TILE), lambda i: (layer, i // tps, 0, i % tps))
    wide = jax.ShapeDtypeStruct((n, D_MODEL), BF16)
    wide_t = jax.ShapeDtypeStruct((D_MODEL, n), BF16)
    narrow = jax.ShapeDtypeStruct((n, LANES), F32)
    stack = jax.ShapeDtypeStruct((n_layers, batch, D_MODEL, seq), F32)
    stack_lf = jax.ShapeDtypeStruct((n_layers, batch, FOX_HEADS, seq), F32)
    n_out_before_stacks = 8
    return pl.pallas_call(
        functools.partial(_fox_proj_body, tiles_per_seq=tiles_per_seq, n_stacked=len(stacked)),
        out_shape=(wide_t, wide_t, wide_t, wide, wide, wide, narrow, narrow, stack, stack, stack_lf),
        grid=(n // TOKEN_TILE,),
        in_specs=([tile] + [_resident(a.shape) for a in ins[1:]]
                  + [pl.BlockSpec(memory_space=pl.ANY)] * len(stacked)),
        out_specs=(tile_t, tile_t, tile_t, tile, tile, tile, nar, nar, slab, slab, slab_lf),
        input_output_aliases={N_FOX_PROJ_INPUTS + s: n_out_before_stacks + s for s in range(len(stacked))},
        scratch_shapes=[pltpu.VMEM((8, LANES), F32), pltpu.VMEM((LANES, LANES), F32)],
        compiler_params=_params("arbitrary"),
        name="fox_proj",
    )(*ins, *stacked)


def _fox_attn_body(dfirst_ref, dlast_ref, thr_ref, frozen_ref, qt_ref, aqt_ref, k_ref, ak_ref, vt_ref, gate_ref,
                   o_ref, qh_ref, m_ref, acc_ref, *, nq, nkb):
    tq, tk = ATTN_Q, ATTN_K
    half = FOX_HEAD_DIM
    per_q = tq // tk
    b, p, i = pl.program_id(0), pl.program_id(1), pl.program_id(2)

    qt = jnp.concatenate([qt_ref[...], aqt_ref[...]], axis=0)
    sub = lax.broadcasted_iota(jnp.int32, qt.shape, 0)
    slot = sub - LANES
    keep_a = (sub < half) | ((slot >= 0) & (slot < 3)) | ((slot >= 6) & (slot < 9))
    zero = jnp.zeros_like(qt)
    qh_ref[0] = jnp.where(keep_a, qt, zero)
    keep_b = ((sub >= half) & (sub < LANES)) | ((slot >= 3) & (slot < 6)) | ((slot >= 9) & (slot < 12))
    qh_ref[1] = jnp.where(keep_b, qt, zero)
    def step(j0, nblocks, heads, mode):
        rows = nblocks * tk
        off = pl.multiple_of(j0 * tk, tk)
        kk = jnp.concatenate([k_ref[pl.ds(off, rows), :], ak_ref[pl.ds(off, rows), :]], axis=1)
        vt = vt_ref[:, pl.ds(off, rows)]
        top = lax.broadcasted_iota(jnp.int32, (LANES, rows), 0) < half
        ones = jnp.ones((LANES, rows), BF16)
        for hh in heads:
            vsel = jnp.where(top, vt, ones) if hh == 0 else jnp.where(top, ones, vt)
            s = _dot(kk, qh_ref[hh])
            if mode == "own":
                assert rows == tq
                kpos = lax.broadcasted_iota(jnp.int32, (rows, tq), 0)
                qpos = lax.broadcasted_iota(jnp.int32, (rows, tq), 1)
                s = jnp.where(kpos <= qpos, s, NEG_BIG)
                m_new = jnp.max(s, axis=0, keepdims=True)
                acc_ref[hh] = _dot(vsel, jnp.exp2(s - m_new).astype(BF16))
                m_ref[hh] = jnp.broadcast_to(m_new, m_ref.shape[1:])
                continue
            m_old = m_ref[hh, 0:1, :]
            if mode == "frozen":
                acc_ref[hh] += _dot(vsel, jnp.exp2(s - m_old).astype(BF16))
                continue
            m_new = jnp.maximum(m_old, jnp.max(s, axis=0, keepdims=True))
            alpha = jnp.exp2(m_old - m_new)
            pr = jnp.exp2(s - m_new).astype(BF16)
            acc_ref[hh] = alpha * acc_ref[hh] + _dot(vsel, pr)
            m_ref[hh] = jnp.broadcast_to(m_new, m_ref.shape[1:])

    thr = thr_ref[0]
    frozen_ok = frozen_ref[0] != 0
    j_top = i * per_q - 1
    group = ATTN_GROUP

    step(i * per_q, per_q, (0, 1), "own")

    def live_blocks(head):
        first = dfirst_ref[(b * FOX_HEADS + head) * nq + i]
        base = (b * FOX_HEADS + head) * nkb

        def near(n):
            j = jnp.maximum(j_top - n, 0)
            return (n <= j_top) & (first - dlast_ref[base + j] >= -thr)

        return lax.while_loop(near, lambda n: n + 1, jnp.int32(0))

    n_live = (live_blocks(2 * p), live_blocks(2 * p + 1))

    def earlier_blocks(mode):
        n_groups = (n_live[0] // group, n_live[1] // group)
        n_joint = jnp.minimum(n_groups[0], n_groups[1])

        def run_groups(lo, hi, heads):
            def body(g, carry):
                step(j_top - group * (g + 1) + 1, group, heads, mode)
                return carry
            lax.fori_loop(lo, hi, body, 0)

        run_groups(0, n_joint, (0, 1))
        for hh in range(2):
            run_groups(n_joint, n_groups[hh], (hh,))
            done = n_groups[hh] * group
            size = group // 2
            while size >= 1:
                take = ((n_live[hh] - done) & size) != 0

                @pl.when(take)
                def _(done=done, size=size, hh=hh):
                    step(j_top - done - size + 1, size, (hh,), mode)

                done = done + jnp.where(take, size, 0)
                size //= 2

    @pl.when(frozen_ok)
    def _():
        earlier_blocks("frozen")

    @pl.when(jnp.logical_not(frozen_ok))
    def _():
        earlier_blocks("online")

    acca, accb = acc_ref[0], acc_ref[1]
    ot = jnp.concatenate([acca[0:half, :] / acca[half:half + 1, :],
                          accb[half:LANES, :] / accb[0:1, :]], axis=0)
    o_ref[...] = (ot.T * gate_ref[...].astype(F32)).astype(BF16)


def _fox_attn_prompt(dfirst, dlast, thr, frozen, qt, aqt, kb, ak, vt, gate, batch, seq, n):
    nq = seq // ATTN_Q
    qspec_t = pl.BlockSpec((LANES, ATTN_Q), lambda b, p, i, *_: (p, b * nq + i))
    tok = pl.BlockSpec((ATTN_Q, LANES), lambda b, p, i, *_: (b * nq + i, p))
    kspec = pl.BlockSpec((seq, LANES), lambda b, p, i, *_: (b, p))
    vspec_t = pl.BlockSpec((LANES, seq), lambda b, p, i, *_: (p, b))
    return pl.pallas_call(
        functools.partial(_fox_attn_body, nq=nq, nkb=seq // ATTN_K),
        out_shape=jax.ShapeDtypeStruct((n, D_MODEL), BF16),
        grid_spec=pltpu.PrefetchScalarGridSpec(
            num_scalar_prefetch=4,
            grid=(batch, FOX_PAIRS, nq),
            in_specs=[qspec_t, qspec_t, kspec, kspec, vspec_t, tok],
            out_specs=tok,
            scratch_shapes=[pltpu.VMEM((2, 2 * LANES, ATTN_Q), BF16), pltpu.VMEM((2, 8, ATTN_Q), F32),
                            pltpu.VMEM((2, LANES, ATTN_Q), F32)]),
        compiler_params=_params("parallel", "parallel", "arbitrary"),
        name="fox_attn_prompt",
    )(dfirst, dlast, thr, frozen, qt, aqt, kb, ak, vt, gate)


def _pair_split(q2):
    lane = lax.broadcasted_iota(jnp.int32, q2.shape, 1)
    left = lane < FOX_HEAD_DIM
    zero = jnp.zeros_like(q2)
    return left, (jnp.where(left, q2, zero), jnp.where(left, zero, q2))


def _sample_bias_body(cl_ref, ln_ref, rt_o, ncol_o, nt_o):
    past = cl_ref.shape[2]
    n_new = ln_ref.shape[0]
    tri = _tri(MXU_DIM, upper=True)
    carry = jnp.zeros((FOX_HEADS, 1), F32)
    prefix = []
    for b in range(past // MXU_DIM):
        p = _cumsum_cols(tri, cl_ref[0, :, b * MXU_DIM:(b + 1) * MXU_DIM]) + carry
        carry = p[:, MXU_DIM - 1:MXU_DIM]
        prefix.append(p)
    for b, p in enumerate(prefix):
        rt_o[0, :, b * MXU_DIM:(b + 1) * MXU_DIM] = carry - p
    ln =jnp.concatenate([ln_ref[...], jnp.zeros((LANES - n_new, LANES), F32)], axis=0)
    nn = _cumsum_rows(_tri(LANES), ln)
    ncol_o[0] = nn[0:n_new, :]
    nt_o[0] = nn.T[0:FOX_HEADS, 0:n_new]


def _sample_bias(cl_t, lf, n_new, row0, dec_batch, layer):
    past = cl_t.shape[2]
    blk0 = row0 // n_new
    return pl.pallas_call(
        _sample_bias_body,
        out_shape=(jax.ShapeDtypeStruct((dec_batch, FOX_HEADS, past), F32),
                   jax.ShapeDtypeStruct((dec_batch, n_new, LANES), F32),
                   jax.ShapeDtypeStruct((dec_batch, FOX_HEADS, n_new), F32)),
        grid=(dec_batch,),
        in_specs=[pl.BlockSpec((1, FOX_HEADS, past), lambda b: (layer * dec_batch + b, 0, 0)),
                  pl.BlockSpec((n_new, LANES), lambda b: (blk0 + b, 0))],
        out_specs=(pl.BlockSpec((1, FOX_HEADS, past), lambda b: (b, 0, 0)),
                   pl.BlockSpec((1, n_new, LANES), lambda b: (b, 0, 0)),
                   pl.BlockSpec((1, FOX_HEADS, n_new), lambda b: (b, 0, 0))),
        compiler_params=_params("parallel"),
        name="fox_sample_bias",
    )(cl_t, lf)


def _fox_sample_body(q_ref, vn_ref, kn_ref, gate_ref, kc_ref, vc_ref, rt_ref, nc_ref, nt_ref,
                     y_any, o_ref):
    del y_any
    n_new = q_ref.shape[0]
    left, qs = _pair_split(q_ref[...])
    kc = kc_ref[0].astype(BF16)
    vc = vc_ref[0].astype(BF16)
    kn = kn_ref[...]
    vn = vn_ref[...]
    rt = rt_ref[0, 0]
    nc = nc_ref[0, 0]
    nt = nt_ref[0, 0]
    row = lax.broadcasted_iota(jnp.int32, (n_new, n_new), 0)
    col = lax.broadcasted_iota(jnp.int32, (n_new, n_new), 1)
    outs = []
    for hh in range(2):
        nq = nc[:, hh:hh + 1]
        s1 = _dot(qs[hh], kc) + (nq + rt[hh:hh + 1, :])
        s2 = lax.dot_general(qs[hh], kn, _NT, preferred_element_type=F32) + (nq - nt[hh:hh + 1, :])
        s2 = jnp.where(col <= row, s2, NEG_BIG)
        m = jnp.maximum(jnp.max(s1, axis=1, keepdims=True), jnp.max(s2, axis=1, keepdims=True))
        p1 = jnp.exp(s1 - m)
        p2 = jnp.exp(s2 - m)
        l = jnp.sum(p1, axis=1, keepdims=True) + jnp.sum(p2, axis=1, keepdims=True)
        pv1 = lax.dot_general(p1.astype(BF16), vc, _NT, preferred_element_type=F32)
        outs.append((pv1 + _dot(p2.astype(BF16), vn)) / l)
    o = jnp.where(left, outs[0], outs[1])
    o_ref[...] = (o * gate_ref[...].astype(F32)).astype(BF16)


def _fox_attn_sample(qs, vs, kb, gate, kc, vc, rt, nc, nt, y_all, n_new, row0, layer):
    dec_batch, _, _, past = rt.shape
    blk0 = row0 // n_new
    own = pl.BlockSpec((n_new, LANES), lambda b, p: (b, p))
    new = pl.BlockSpec((n_new, LANES), lambda b, p: (blk0 + b, p))
    cache = pl.BlockSpec((1, LANES, past), lambda b, p: (layer * dec_batch + b, p, 0))
    return pl.pallas_call(
        _fox_sample_body,
        out_shape=jax.ShapeDtypeStruct(y_all.shape, BF16),
        grid=(dec_batch, FOX_PAIRS),
        in_specs=[own, own, own, own, cache, cache,
                  pl.BlockSpec((1, 1, 2, past), lambda b, p: (b, p, 0, 0)),
                  pl.BlockSpec((1, 1, n_new, 2), lambda b, p: (b, p, 0, 0)),
                  pl.BlockSpec((1, 1, 2, n_new), lambda b, p: (b, p, 0, 0)),
                  pl.BlockSpec(memory_space=pl.ANY)],
        out_specs=new,
        input_output_aliases={9: 0},
        compiler_params=_params("parallel", "parallel"),
        name="fox_attn_sample",
    )(qs, vs, kb, gate, kc, vc, rt, nc, nt, y_all)


def _ret_log_gammas():
    return [float(v) for v in np.log1p(-np.exp2(-5.0 - np.arange(RET_HEADS, dtype=np.float32)))]


def _ret_proj_body(x_ref, g_ref, w_ref, cos_ref, sin_ref, q_o, k_o, v_o, sg_o):
    h = _rmsnorm(x_ref[...], g_ref[...]).astype(BF16)
    cos = cos_ref[...]
    sin = sin_ref[...]
    a, b2 = RET_QK_WIDTH, 2 * RET_QK_WIDTH

    def rotary(t, scale):
        outs = []
        for hd in range(RET_HEADS):
            th = t[:, hd * RET_QK_DIM:(hd + 1) * RET_QK_DIM]
            outs.append((th * cos + pltpu.roll(th, RET_QK_DIM // 2, 1) * sin) * scale)
        return jnp.concatenate(outs, axis=1).astype(BF16)

    q_o[...] = rotary(_dot(h, w_ref[:, 0:a]), 1.0)
    k_o[...] = rotary(_dot(h, w_ref[:, a:b2]), RET_QK_DIM ** -0.5)
    v_o[...] = _dot(h, w_ref[:, b2:b2 + RET_V_WIDTH]).astype(BF16)
    g = _dot(h, w_ref[:, b2 + RET_V_WIDTH:b2 + 2 * RET_V_WIDTH])
    sg_o[...] = (g * jax.nn.sigmoid(g)).astype(BF16)


def _ret_proj(x, gain, w, cos, sin):
    n = x.shape[0]
    tile = lambda width: pl.BlockSpec((TOKEN_TILE, width), lambda i: (i, 0))
    return pl.pallas_call(
        _ret_proj_body,
        out_shape=(jax.ShapeDtypeStruct((n, RET_QK_WIDTH), BF16), jax.ShapeDtypeStruct((n, RET_QK_WIDTH), BF16),
                   jax.ShapeDtypeStruct((n, RET_V_WIDTH), BF16), jax.ShapeDtypeStruct((n, RET_V_WIDTH), BF16)),
        grid=(n // TOKEN_TILE,),
        in_specs=[tile(D_MODEL), _resident((1, D_MODEL)), _resident(w.shape), tile(LANES), tile(LANES)],
        out_specs=(tile(RET_QK_WIDTH), tile(RET_QK_WIDTH), tile(RET_V_WIDTH), tile(RET_V_WIDTH)),
        compiler_params=_params("parallel"),
        name="ret_proj",
    )(x, gain, w, cos, sin)


def _ret_decay_matrix(t, lg):
    row = lax.broadcasted_iota(jnp.int32, (t, t), 0)
    col = lax.broadcasted_iota(jnp.int32, (t, t), 1)
    dist = jnp.abs(row - col).astype(F32)
    visible = (col // CHUNK) <= (row // CHUNK)
    return jnp.where(visible, jnp.exp(dist * lg), 0.0)


def _ret_tile(q_ref, k_ref, v_ref, sg_ref, y_ref, state_in, state_out, decay, t):
    lgs = _ret_log_gammas()
    pos = lax.broadcasted_iota(jnp.int32, (t, 1), 0).astype(F32)
    for hd in range(RET_HEADS):
        lg = lgs[hd]
        qs = slice(hd * RET_QK_DIM, (hd + 1) * RET_QK_DIM)
        vs = slice(hd * RET_V_DIM, (hd + 1) * RET_V_DIM)
        qh = q_ref[:, qs]
        kh = k_ref[:, qs]
        vh = v_ref[:, vs]
        state = state_in(hd)
        sc = lax.dot_general(qh, kh, _NT, preferred_element_type=F32) * decay(hd)
        o = _dot(sc.astype(BF16), vh) + _dot(qh, state.astype(BF16)) * jnp.exp(pos * lg)
        kd = (kh.astype(F32) * jnp.exp((t - pos) * lg)).astype(BF16)
        state_out(hd, state * math.exp(t * lg) + lax.dot_general(kd, vh, _TN, preferred_element_type=F32))
        mu = jnp.mean(o, axis=-1, keepdims=True)
        dev = o - mu
        var = jnp.mean(dev * dev, axis=-1, keepdims=True)
        y_ref[:, vs] = (sg_ref[:, vs].astype(F32) * (dev * lax.rsqrt(var + GN_EPS))).astype(BF16)


def _ret_prompt_body(q_ref, k_ref, v_ref, sg_ref, y_ref, st_ref, decay_ref):
    t = TOKEN_TILE

    @pl.when((pl.program_id(0) == 0) & (pl.program_id(1) == 0))
    def _():
        for hd, lg in enumerate(_ret_log_gammas()):
            decay_ref[hd] = _ret_decay_matrix(t, lg)

    @pl.when(pl.program_id(1) == 0)
    def _():
        st_ref[...] = jnp.zeros_like(st_ref)

    def state_out(hd, val):
        st_ref[0, hd] = val

    _ret_tile(q_ref, k_ref, v_ref, sg_ref, y_ref, lambda hd: st_ref[0, hd], state_out,
              lambda hd: decay_ref[hd], t)


def _ret_prompt(q, k, v, sg, batch, seq):
    nt = seq // TOKEN_TILE
    tile = lambda width: pl.BlockSpec((TOKEN_TILE, width), lambda b, i: (b * nt + i, 0))
    return pl.pallas_call(
        _ret_prompt_body,
        out_shape=(jax.ShapeDtypeStruct((q.shape[0], RET_V_WIDTH), BF16),
                   jax.ShapeDtypeStruct((batch, RET_HEADS, RET_QK_DIM, RET_V_DIM), F32)),
        grid=(batch, nt),
        in_specs=[tile(RET_QK_WIDTH), tile(RET_QK_WIDTH), tile(RET_V_WIDTH), tile(RET_V_WIDTH)],
        out_specs=(tile(RET_V_WIDTH),
                   pl.BlockSpec((1, RET_HEADS, RET_QK_DIM, RET_V_DIM), lambda b, i: (b, 0, 0, 0))),
        scratch_shapes=[pltpu.VMEM((RET_HEADS, TOKEN_TILE, TOKEN_TILE), F32)],
        compiler_params=_params("arbitrary", "arbitrary"),
        name="ret_prompt",
    )(q, k, v, sg)


def _ret_sample_body(q_ref, k_ref, v_ref, sg_ref, st_in_ref, y_any, y_ref, st_out_ref):
    del y_any
    t = q_ref.shape[0]
    lgs = _ret_log_gammas()

    def state_out(hd, val):
        st_out_ref[0, hd] = val

    _ret_tile(q_ref, k_ref, v_ref, sg_ref, y_ref, lambda hd: st_in_ref[0, hd], state_out,
              lambda hd: _ret_decay_matrix(t, lgs[hd]), t)


def _ret_sample(q, k, v, sg, state, y_all, n_new, row0):
    dec_batch = state.shape[0]
    blk0 = row0 // n_new
    tile = lambda width: pl.BlockSpec((n_new, width), lambda b: (blk0 + b, 0))
    st = pl.BlockSpec((1, RET_HEADS, RET_QK_DIM, RET_V_DIM), lambda b: (b, 0, 0, 0))
    return pl.pallas_call(
        _ret_sample_body,
        out_shape=(jax.ShapeDtypeStruct(y_all.shape, BF16), jax.ShapeDtypeStruct(state.shape, F32)),
        grid=(dec_batch,),
        in_specs=[tile(RET_QK_WIDTH), tile(RET_QK_WIDTH), tile(RET_V_WIDTH), tile(RET_V_WIDTH), st,
                  pl.BlockSpec(memory_space=pl.ANY)],
        out_specs=(tile(RET_V_WIDTH), st),
        input_output_aliases={5: 0},
        compiler_params=_params("parallel"),
        name="ret_sample",
    )(q, k, v, sg, state, y_all)


def _rotary_tables(pos):
    half = RET_QK_DIM // 2
    inv_freq = ROPE_BASE ** (-jnp.linspace(0.0, 1.0, half, dtype=F32))
    ang = pos[:, None] * inv_freq[None, :]
    cos, sin = jnp.cos(ang), jnp.sin(ang)
    return jnp.concatenate([cos, cos], axis=1), jnp.concatenate([-sin, sin], axis=1)


def _head_tiled(gain):
    return jnp.tile(gain.astype(F32), FOX_HEADS)


def kernel(x_prompt, x_sample, cache_fox_k, cache_fox_v, cache_fox_logf, state_ret, norm_gains, w_ffn_in,
           w_ffn_out, fox_w_in, fox_b_f, fox_q_gain, fox_k_gain, fox_w_out, ret_w_in, ret_w_out, final_gain):
    batch, seq, _ = x_prompt.shape
    dec_batch, n_new, _ = x_sample.shape
    past = cache_fox_k.shape[2]
    depth = norm_gains.shape[0]
    n_prompt = batch * seq
    n_sample = dec_batch * n_new
    assert seq % TOKEN_TILE == 0 and n_sample == TOKEN_TILE and TOKEN_TILE % n_new == 0
    assert seq % ATTN_Q == 0 and ATTN_Q % ATTN_K == 0 and past % MXU_DIM == 0
    assert TOKEN_TILE % CHUNK == 0 and n_new <= CHUNK
    n_fox = (depth + 1) // 2

    xs = (x_prompt.reshape(n_prompt, D_MODEL), x_sample.reshape(n_sample, D_MODEL))
    pos =jnp.concatenate([jnp.tile(jnp.arange(seq, dtype=F32), batch),
                           jnp.tile(past + jnp.arange(n_new, dtype=F32), dec_batch)])
    cos, sin = _rotary_tables(pos)
    grp_idx = np.arange(MXU_DIM) // FOX_HEAD_DIM
    grp = jnp.asarray(grp_idx[:, None] == grp_idx[None, :], dtype=BF16)
    sel_k, sel_q, const_k, const_q = _aug_tables()
    selk, ck = jnp.asarray(sel_k, BF16), jnp.asarray(const_k, F32)
    selqt, cqt = jnp.asarray(sel_q.T, BF16), jnp.asarray(const_q.T, F32)
    fgain = final_gain.reshape(1, D_MODEL)

    feature_major = lambda c: c.transpose(0, 1, 3, 4, 2).reshape(n_fox * dec_batch, D_MODEL, past)
    cache_k_t, cache_v_t = feature_major(cache_fox_k), feature_major(cache_fox_v)
    cache_lf_t = cache_fox_logf.transpose(0, 1, 3, 2).reshape(n_fox * dec_batch, FOX_HEADS, past)
    w_in_bf, w_out_bf = w_ffn_in.astype(BF16), w_ffn_out.astype(BF16)
    rs_p, fl_s, rs_s = [], [], []
    fox_stacked = ()
    for layer in range(depth):
        g = norm_gains[layer].reshape(3, 1, D_MODEL)
        x = _ffn(xs, g[0], w_in_bf, w_out_bf, (layer, 0), n_prompt, n_sample)
        j = layer // 2
        if layer % 2 == 0:
            w = fox_w_in[j]
            wf = jnp.pad(w[:, 4 * D_MODEL:], ((0, 0), (0, LANES - FOX_HEADS)))
            bf = jnp.pad(fox_b_f[j], (0, LANES - FOX_HEADS))
            wt = jnp.concatenate([w[:, 0:D_MODEL], w[:, 2 * D_MODEL:3 * D_MODEL]], axis=1).T
            qg, kg = _head_tiled(fox_q_gain[j]), _head_tiled(fox_k_gain[j])
            token_major = (x, g[1], w[:, :4 * D_MODEL].astype(BF16), wf.astype(BF16), bf.reshape(1, LANES))
            kg2, qg2 = kg.reshape(1, D_MODEL), qg.reshape(1, D_MODEL)
            proj_ins = token_major + (
                kg2, grp, selk, ck, wt.astype(BF16), wf.T.astype(BF16), bf.reshape(LANES, 1),
                (qg * (FOX_Q_SCALE * LOG2E)).reshape(D_MODEL, 1), selqt, cqt)
            qt, aqt, vt, kb, ak, gate, _, d, *stk_p = _fox_proj(
                proj_ins, fox_stacked[:3], j, n_fox, n_prompt, seq // TOKEN_TILE)
            qs, ks, vs, gate_s, lf_s, *stk_s = _fox_proj_sample(
                token_major + (qg2, kg2, grp), fox_stacked[3:], j, n_fox, n_prompt // TOKEN_TILE)
            fox_stacked = (*stk_p, *stk_s)
            dh = d[:, :FOX_HEADS].reshape(batch, seq, FOX_HEADS)
            dfirst = dh[:, 0::ATTN_Q, :].transpose(0, 2, 1).reshape(-1)
            dlast = dh[:, ATTN_K - 1::ATTN_K, :].transpose(0, 2, 1).reshape(-1)
            bound = 1.01 * (FOX_HEAD_DIM ** 0.5) * jnp.max(jnp.abs(fox_q_gain[j])) * jnp.max(jnp.abs(fox_k_gain[j]))
            thr = (2.0 * bound + SKIP_MARGIN).reshape(1).astype(F32)
            frozen = (2.0 * LOG2E * bound < FROZEN_MAX_RANGE).reshape(1).astype(jnp.int32)
            y = _fox_attn_prompt(dfirst, dlast, thr, frozen, qt, aqt, kb, ak, vt, gate, batch, seq,
                                 n_prompt + n_sample)
            rt, nc, nt = _sample_bias(cache_lf_t, lf_s, n_new, 0, dec_batch, j)
            y = _fox_attn_sample(
                qs, vs, ks, gate_s, cache_k_t, cache_v_t, rt.reshape(dec_batch, FOX_PAIRS, 2, past),
                nc[:, :, :FOX_HEADS].reshape(dec_batch, n_new, FOX_PAIRS, 2).transpose(0, 2, 1, 3),
                nt.reshape(dec_batch, FOX_PAIRS, 2, n_new), y, n_new, n_prompt, j)
            mix = (y, fox_w_out[j].astype(BF16))
            fl_s.append(lf_s[:, :FOX_HEADS].reshape(dec_batch, n_new, FOX_HEADS))
        else:
            q, k, v, sg = _ret_proj(x, g[1], ret_w_in[j].astype(BF16), cos, sin)
            y, st_p = _ret_prompt(q, k, v, sg, batch, seq)
            y, st_s = _ret_sample(q, k, v, sg, state_ret[j], y, n_new, n_prompt)
            mix = (y, ret_w_out[j].astype(BF16))
            rs_p.append(st_p)
            rs_s.append(st_s)
        x = _ffn((x,), g[2], w_in_bf, w_out_bf, (layer, 1), n_prompt, n_sample,
                 mix=mix, final_gain=fgain if layer == depth - 1 else None)
        xs = (x,)
    y_prompt, y_sample = x
    kft_p, vft_p, lft_p, kf_s, vf_s = fox_stacked
    per_head = lambda a: a.reshape(n_fox, batch, FOX_HEADS, FOX_HEAD_DIM, seq).transpose(0, 1, 4, 2, 3)
    heads_s = (n_fox, dec_batch, n_new, FOX_HEADS, FOX_HEAD_DIM)
    return (y_prompt.reshape(batch, seq, D_MODEL), y_sample.reshape(dec_batch, n_new, D_MODEL),
            per_head(kft_p), per_head(vft_p), lft_p.transpose(0, 1, 3, 2), jnp.stack(rs_p),
            kf_s.reshape(heads_s), vf_s.reshape(heads_s), jnp.stack(fl_s), jnp.stack(rs_s))
```

```python
import functools
import math

import numpy as np
import jax
import jax.numpy as jnp
from jax import lax
from jax.experimental import pallas as pl
from jax.experimental.pallas import tpu as pltpu

F32 = jnp.float32
BF16 = jnp.bfloat16

D_MODEL = 1024
CHUNK = 64
FOX_HEADS = 16
FOX_HEAD_DIM = 64
FOX_PAIRS = FOX_HEADS // 2
RET_HEADS = 8
RET_QK_DIM = 128
RET_V_DIM = 256
RET_QK_WIDTH = RET_HEADS * RET_QK_DIM
RET_V_WIDTH = RET_HEADS * RET_V_DIM
D_FF = 2816
RMS_EPS = 1e-6
GN_EPS = 1e-5
ROPE_BASE = 10000.0

LANES = 128
MXU_DIM = 256
TOKEN_TILE = 512
FFN_CHUNK = MXU_DIM
ATTN_Q = 512
ATTN_K = 256
ATTN_GROUP = 4
VMEM_LIMIT = 56 * 1024 * 1024
NEG_BIG = -1e30
LOG2E = 1.4426950408889634
FOX_Q_SCALE = FOX_HEAD_DIM ** -0.5
AUG_PIECES = 3
SKIP_MARGIN = 32.0
FROZEN_MAX_RANGE = 100.0

_NT = (((1,), (1,)), ((), ()))
_TN = (((0,), (0,)), ((), ()))


def _params(*sem):
    return pltpu.CompilerParams(dimension_semantics=sem, vmem_limit_bytes=VMEM_LIMIT)


def _resident(shape):
    nd = len(shape)
    return pl.BlockSpec(shape, lambda *_: (0,) * nd, pipeline_mode=pl.Buffered(1))


def _rmsnorm(x, gain):
    ms = jnp.mean(x * x, axis=-1, keepdims=True)
    return x * lax.rsqrt(ms + RMS_EPS) * gain


def _dot(a, b):
    return jnp.dot(a, b, preferred_element_type=F32)


def _split3(x):
    hi = x.astype(BF16)
    r = x - hi.astype(F32)
    mid = r.astype(BF16)
    lo = (r - mid.astype(F32)).astype(BF16)
    return hi, mid, lo


def _tri(n, upper=False):
    row = lax.broadcasted_iota(jnp.int32, (n, n), 0)
    col = lax.broadcasted_iota(jnp.int32, (n, n), 1)
    return jnp.where((row <= col) if upper else (row >= col), 1.0, 0.0).astype(BF16)


def _cumsum_rows(tri, x):
    hi, mid, lo = _split3(x)
    return _dot(tri, hi) + _dot(tri, mid) + _dot(tri, lo)


def _cumsum_cols(tri_upper, x):
    hi, mid, lo = _split3(x)
    return _dot(hi, tri_upper) + _dot(mid, tri_upper) + _dot(lo, tri_upper)


def _log_sigmoid(z):
    return jnp.minimum(z, 0.0) - jnp.log1p(jnp.exp(-jnp.abs(z)))


def _ffn_body(*refs, first, mixed, final, n_prompt_tiles):
    refs = list(refs)
    x_refs = [refs.pop(0) for _ in range(2 if first else 1)]
    y_ref, wmix_ref = (refs.pop(0), refs.pop(0)) if mixed else (None, None)
    g_ref, win_ref, wout_ref = refs.pop(0), refs.pop(0), refs.pop(0)
    fg_ref = refs.pop(0) if final else None
    o_refs = [refs.pop(0) for _ in range(2 if final else 1)]
    (acc_ref,) = refs
    is_prompt = pl.program_id(0) < n_prompt_tiles
    x = jnp.where(is_prompt, x_refs[0][...], x_refs[1][...]) if first else x_refs[0][...]
    if mixed:
        x = x + _dot(y_ref[...], wmix_ref[...])
    h = _rmsnorm(x, g_ref[...]).astype(BF16)
    for c in range(D_FF // FFN_CHUNK):
        lo, hi = c * FFN_CHUNK, (c + 1) * FFN_CHUNK
        gate = _dot(h, win_ref[0, 0, :, lo:hi])
        up = _dot(h, win_ref[0, 0, :, D_FF + lo:D_FF + hi])
        act = (gate * jax.nn.sigmoid(gate) * up).astype(BF16)
        part = _dot(act, wout_ref[0, 0, lo:hi, :])
        if c == 0:
            acc_ref[...] = part
        else:
            acc_ref[...] += part
    y = x + 0.5 * acc_ref[...]
    if not final:
        o_refs[0][...] = y
        return
    y = _rmsnorm(y, fg_ref[...])

    @pl.when(is_prompt)
    def _():
        o_refs[0][...] = y

    @pl.when(jnp.logical_not(is_prompt))
    def _():
        o_refs[1][...] = y


def _ffn(xs, gain, w_in, w_out, which, n_prompt, n_sample, mix=None, final_gain=None):
    first, mixed, final = len(xs) == 2, mix is not None, final_gain is not None
    n = n_prompt + n_sample
    n_prompt_tiles = n_prompt // TOKEN_TILE
    tile = pl.BlockSpec((TOKEN_TILE, D_MODEL), lambda i: (i, 0))
    prompt_tile = pl.BlockSpec((TOKEN_TILE, D_MODEL), lambda i: (jnp.minimum(i, n_prompt_tiles - 1), 0))
    sample_tile = pl.BlockSpec((TOKEN_TILE, D_MODEL), lambda i: (jnp.maximum(i - n_prompt_tiles, 0), 0))
    ins, in_specs = list(xs), ([prompt_tile, sample_tile] if first else [tile])
    if mixed:
        y, w_mix = mix
        ins += [y, w_mix]
        in_specs += [pl.BlockSpec((TOKEN_TILE, y.shape[1]), lambda i: (i, 0)), _resident(w_mix.shape)]
    ins += [gain, w_in, w_out]
    pick = lambda *_: (*which, 0, 0)
    in_specs += [_resident((1, D_MODEL)),
                 pl.BlockSpec((1, 1, D_MODEL, 2 * D_FF), pick, pipeline_mode=pl.Buffered(1)),
                 pl.BlockSpec((1, 1, D_FF, D_MODEL), pick, pipeline_mode=pl.Buffered(1))]
    if final:
        ins.append(final_gain)
        in_specs.append(_resident((1, D_MODEL)))
        out_shape = (jax.ShapeDtypeStruct((n_prompt, D_MODEL), F32), jax.ShapeDtypeStruct((n_sample, D_MODEL), F32))
        out_specs = (prompt_tile, sample_tile)
    else:
        out_shape, out_specs = jax.ShapeDtypeStruct((n, D_MODEL), F32), tile
    return pl.pallas_call(
        functools.partial(_ffn_body, first=first, mixed=mixed, final=final, n_prompt_tiles=n_prompt_tiles),
        out_shape=out_shape,
        grid=(n // TOKEN_TILE,),
        in_specs=in_specs,
        out_specs=out_specs,
        scratch_shapes=[pltpu.VMEM((TOKEN_TILE, D_MODEL), F32)],
        compiler_params=_params("arbitrary"),
        name="ffn_final" if final else ("ffn_mixed" if mixed else "ffn"),
    )(*ins)


def _aug_tables():
    sel_k = np.zeros((AUG_PIECES * LANES, D_MODEL), np.float32)
    sel_q = np.zeros((AUG_PIECES * LANES, D_MODEL), np.float32)
    const_k = np.zeros((1, D_MODEL), np.float32)
    const_q = np.zeros((1, D_MODEL), np.float32)
    for p in range(FOX_PAIRS):
        for hh in range(2):
            for piece in range(AUG_PIECES):
                src = piece * LANES + 2 * p + hh
                sel_k[src, p * LANES + 3 * hh + piece] = 1.0
                sel_q[src, p * LANES + 6 + 3 * hh + piece] = 1.0
                const_q[0, p * LANES + 3 * hh + piece] = -1.0
                const_k[0, p * LANES + 6 + 3 * hh + piece] = 1.0
    return sel_k, sel_q, const_k, const_q


def _head_rmsnorm(t, grp, gain):
    outs = []
    for c in range(D_MODEL // MXU_DIM):
        blk = t[:, c * MXU_DIM:(c + 1) * MXU_DIM]
        sq = blk * blk
        hi = sq.astype(BF16)
        lo = (sq - hi.astype(F32)).astype(BF16)
        ss = _dot(hi, grp) + _dot(lo, grp)
        outs.append(blk * lax.rsqrt(ss * (1.0 / FOX_HEAD_DIM) + RMS_EPS))
    return jnp.concatenate(outs, axis=1) * gain


def _head_rmsnorm_t(t, grp, gain):
    outs = []
    for c in range(D_MODEL // MXU_DIM):
        blk = t[c * MXU_DIM:(c + 1) * MXU_DIM, :]
        sq = blk * blk
        hi = sq.astype(BF16)
        lo = (sq - hi.astype(F32)).astype(BF16)
        ss = _dot(grp, hi) + _dot(grp, lo)
        outs.append(blk * lax.rsqrt(ss * (1.0 / FOX_HEAD_DIM) + RMS_EPS))
    return jnp.concatenate(outs, axis=0) * gain


N_FOX_PROJ_INPUTS = 16
N_FOX_SAMPLE_INPUTS = 8


def _fox_token_major(h, w_ref, wf_ref, bf_ref, kg_ref, grp, kb_o, gate_o, lf_o):
    w = D_MODEL
    k = _head_rmsnorm(_dot(h, w_ref[:, w:2 * w]), grp, kg_ref[...])
    kb_o[...] = k.astype(BF16)
    v = _dot(h, w_ref[:, 2 * w:3 * w])
    gate_o[...] = jax.nn.sigmoid(_dot(h, w_ref[:, 3 * w:4 * w])).astype(BF16)
    logf = _log_sigmoid(_dot(h, wf_ref[...]) + bf_ref[...])
    lf_o[...] = logf
    return k, v, logf


def _fox_proj_sample_body(*refs, n_stacked):
    x_ref, g_ref, w_ref, wf_ref, bf_ref, qg_ref, kg_ref, grp_ref = refs[:N_FOX_SAMPLE_INPUTS]
    qs_o, kb_o, vs_o, gate_o, lf_o, kf_o, vf_o = refs[N_FOX_SAMPLE_INPUTS + n_stacked:]
    h = _rmsnorm(x_ref[...], g_ref[...]).astype(BF16)
    grp = grp_ref[...]
    k, v, _ = _fox_token_major(h, w_ref, wf_ref, bf_ref, kg_ref, grp, kb_o, gate_o, lf_o)
    for hd in range(FOX_HEADS):
        cols = slice(hd * FOX_HEAD_DIM, (hd + 1) * FOX_HEAD_DIM)
        rows = pl.ds(hd, TOKEN_TILE, stride=FOX_HEADS)
        kf_o.at[0][rows, :] = k[:, cols]
        vf_o.at[0][rows, :] = v[:, cols]
    vs_o[...] = v.astype(BF16)
    q = _head_rmsnorm(_dot(h, w_ref[:, 0:D_MODEL]), grp, qg_ref[...])
    qs_o[...] = (q * FOX_Q_SCALE).astype(BF16)


def _fox_proj_sample(ins, stacked, layer, n_layers, tile0):
    assert len(ins) == N_FOX_SAMPLE_INPUTS
    n = ins[0].shape[0] - tile0 * TOKEN_TILE
    tile = pl.BlockSpec((TOKEN_TILE, D_MODEL), lambda i: (i, 0))
    slab = pl.BlockSpec((1, TOKEN_TILE * FOX_HEADS, FOX_HEAD_DIM), lambda i: (layer, i, 0))
    wide = jax.ShapeDtypeStruct((n, D_MODEL), BF16)
    stack = jax.ShapeDtypeStruct((n_layers, n * FOX_HEADS, FOX_HEAD_DIM), F32)
    n_out_before_stacks = 5
    return pl.pallas_call(
        functools.partial(_fox_proj_sample_body, n_stacked=len(stacked)),
        out_shape=(wide, wide, wide, wide, jax.ShapeDtypeStruct((n, LANES), F32), stack, stack),
        grid=(n // TOKEN_TILE,),
        in_specs=([pl.BlockSpec((TOKEN_TILE, D_MODEL), lambda i: (tile0 + i, 0))]
                  + [_resident(a.shape) for a in ins[1:]] + [pl.BlockSpec(memory_space=pl.ANY)] * len(stacked)),
        out_specs=(tile, tile, tile, tile, pl.BlockSpec((TOKEN_TILE, LANES), lambda i: (i, 0)), slab, slab),
        input_output_aliases={N_FOX_SAMPLE_INPUTS + s: n_out_before_stacks + s for s in range(len(stacked))},
        compiler_params=_params("parallel"),
        name="fox_proj_sample",
    )(*ins, *stacked)


def _fox_proj_body(*refs, tiles_per_seq, n_stacked):
    (x_ref, g_ref, w_ref, wf_ref, bf_ref, kg_ref, grp_ref, selk_ref, ck_ref,
     wt_ref, wft_ref, bft_ref, qgt_ref, selqt_ref, cqt_ref, shift_ref) = refs[:N_FOX_PROJ_INPUTS]
    (qt_o, aqt_o, vt_o, kb_o, ak_o, gate_o, lf_o, d_o, kft_o, vft_o, lft_o,
     carry_ref, carryt_ref) = refs[N_FOX_PROJ_INPUTS + n_stacked:]
    i = pl.program_id(0)
    tm, w = TOKEN_TILE, D_MODEL
    hf = _rmsnorm(x_ref[...], g_ref[...])
    h = hf.astype(BF16)
    ht = hf.T.astype(BF16)
    grp = grp_ref[...]

    @pl.when(i % tiles_per_seq == 0)
    def _():
        carry_ref[...] = jnp.zeros_like(carry_ref)
        carryt_ref[...] = jnp.zeros_like(carryt_ref)

    k, _, logf = _fox_token_major(h, w_ref, wf_ref, bf_ref, kg_ref, grp, kb_o, gate_o, lf_o)
    kft_o[0, 0] = k.T
    d = _cumsum_rows(_tri(tm), logf) + carry_ref[0:1, :]
    carry_ref[...] = jnp.broadcast_to(d[tm - 1:tm, :], carry_ref.shape)
    d_o[...] = d
    pieces = jnp.concatenate(_split3(d * LOG2E), axis=1)
    ak_o[...] = (_dot(pieces, selk_ref[...]) + ck_ref[...]).astype(BF16)

    qt = _head_rmsnorm_t(_dot(wt_ref[0:w, :], ht), grp, qgt_ref[...])
    qt_o[...] = qt.astype(BF16)
    vt = _dot(wt_ref[w:2 * w, :], ht)
    vft_o[0, 0] = vt
    vt_o[...] = vt.astype(BF16)
    logft = _log_sigmoid(_dot(wft_ref[...], ht) + bft_ref[...])
    lft_o[0, 0] = logft[0:FOX_HEADS, :]
    dt = _cumsum_cols(_tri(tm, upper=True), logft) + carryt_ref[:, 0:1]
    carryt_ref[...] = jnp.broadcast_to(dt[:, tm - 1:tm], carryt_ref.shape)
    piecest = jnp.concatenate(_split3(dt * LOG2E - shift_ref[...]), axis=0)
    aqt_o[...] = (_dot(selqt_ref[...], piecest) + cqt_ref[...]).astype(BF16)


def _fox_proj(ins, stacked, layer, n_layers, n, tiles_per_seq):
    assert len(ins) == N_FOX_PROJ_INPUTS
    tps = tiles_per_seq
    batch, seq = n // (tps * TOKEN_TILE), tps * TOKEN_TILE
    tile = pl.BlockSpec((TOKEN_TILE, D_MODEL), lambda i: (i, 0))
    tile_t = pl.BlockSpec((D_MODEL, TOKEN_TILE), lambda i: (0, i))
    nar = pl.BlockSpec((TOKEN_TILE, LANES), lambda i: (i, 0))
    slab = pl.BlockSpec((1, 1, D_MODEL, TOKEN_TILE), lambda i: (layer, i // tps, 0, i % tps))
    slab_lf = pl.BlockSpec((1, 1, FOX_HEADS, TOKEN_TILE), lambda i: (layer, i // tps, 0, i % tps))
    wide = jax.ShapeDtypeStruct((n, D_MODEL), BF16)
    wide_t = jax.ShapeDtypeStruct((D_MODEL, n), BF16)
    narrow = jax.ShapeDtypeStruct((n, LANES), F32)
    stack = jax.ShapeDtypeStruct((n_layers, batch, D_MODEL, seq), F32)
    stack_lf = jax.ShapeDtypeStruct((n_layers, batch, FOX_HEADS, seq), F32)
    n_out_before_stacks = 8
    return pl.pallas_call(
        functools.partial(_fox_proj_body, tiles_per_seq=tiles_per_seq, n_stacked=len(stacked)),
        out_shape=(wide_t, wide_t, wide_t, wide, wide, wide, narrow, narrow, stack, stack, stack_lf),
        grid=(n // TOKEN_TILE,),
        in_specs=([tile] + [_resident(a.shape) for a in ins[1:]]
                  + [pl.BlockSpec(memory_space=pl.ANY)] * len(stacked)),
        out_specs=(tile_t, tile_t, tile_t, tile, tile, tile, nar, nar, slab, slab, slab_lf),
        input_output_aliases={N_FOX_PROJ_INPUTS + s: n_out_before_stacks + s for s in range(len(stacked))},
        scratch_shapes=[pltpu.VMEM((8, LANES), F32), pltpu.VMEM((LANES, LANES), F32)],
        compiler_params=_params("arbitrary"),
        name="fox_proj",
    )(*ins, *stacked)


def _fox_attn_body(dfirst_ref, dlast_ref, thr_ref, frozen_ref, qt_ref, aqt_ref, k_ref, ak_ref, vt_ref, gate_ref,
                   o_ref, qh_ref, m_ref, acc_ref, *, nq, nkb):
    tq, tk = ATTN_Q, ATTN_K
    half = FOX_HEAD_DIM
    per_q = tq // tk
    b, p, i = pl.program_id(0), pl.program_id(1), pl.program_id(2)

    qt = jnp.concatenate([qt_ref[...], aqt_ref[...]], axis=0)
    sub = lax.broadcasted_iota(jnp.int32, qt.shape, 0)
    slot = sub - LANES
    keep_a = (sub < half) | ((slot >= 0) & (slot < 3)) | ((slot >= 6) & (slot < 9))
    zero = jnp.zeros_like(qt)
    qh_ref[0] = jnp.where(keep_a, qt, zero)
    keep_b = ((sub >= half) & (sub < LANES)) | ((slot >= 3) & (slot < 6)) | ((slot >= 9) & (slot < 12))
    qh_ref[1] = jnp.where(keep_b, qt, zero)
    def step(j0, nblocks, heads, online, own=False):
        rows = nblocks * tk
        off = pl.multiple_of(j0 * tk, tk)
        kk = jnp.concatenate([k_ref[pl.ds(off, rows), :], ak_ref[pl.ds(off, rows), :]], axis=1)
        vt = vt_ref[:, pl.ds(off, rows)]
        top = lax.broadcasted_iota(jnp.int32, (LANES, rows), 0) < half
        ones = jnp.ones((LANES, rows), BF16)
        for hh in heads:
            vsel = jnp.where(top, vt, ones) if hh == 0 else jnp.where(top, ones, vt)
            s = _dot(kk, qh_ref[hh])
            if own:
                assert rows == tq
                kpos = lax.broadcasted_iota(jnp.int32, (rows, tq), 0)
                qpos = lax.broadcasted_iota(jnp.int32, (rows, tq), 1)
                s = jnp.where(kpos <= qpos, s, NEG_BIG)
            if not online:
                pv = _dot(vsel, jnp.exp2(s).astype(BF16))
                acc_ref[hh] = pv if own else acc_ref[hh] + pv
                continue
            if own:
                m_new = jnp.max(s, axis=0, keepdims=True)
                acc_ref[hh] = _dot(vsel, jnp.exp2(s - m_new).astype(BF16))
                m_ref[hh] = jnp.broadcast_to(m_new, m_ref.shape[1:])
                continue
            m_old = m_ref[hh, 0:1, :]
            m_new = jnp.maximum(m_old, jnp.max(s, axis=0, keepdims=True))
            alpha = jnp.exp2(m_old - m_new)
            pr = jnp.exp2(s - m_new).astype(BF16)
            acc_ref[hh] = alpha * acc_ref[hh] + _dot(vsel, pr)
            m_ref[hh] = jnp.broadcast_to(m_new, m_ref.shape[1:])

    thr = thr_ref[0]
    frozen_ok = frozen_ref[0] != 0
    j_top = i * per_q - 1
    group = ATTN_GROUP

    def live_blocks(head):
        first = dfirst_ref[(b * FOX_HEADS + head) * nq + i]
        base = (b * FOX_HEADS + head) * nkb

        def near(n):
            j = jnp.maximum(j_top - n, 0)
            return (n <= j_top) & (first - dlast_ref[base + j] >= -thr)

        return lax.while_loop(near, lambda n: n + 1, jnp.int32(0))

    n_live = (live_blocks(2 * p), live_blocks(2 * p + 1))

    def all_blocks(online):
        step(i * per_q, per_q, (0, 1), online, own=True)
        n_groups = (n_live[0] // group, n_live[1] // group)
        n_joint = jnp.minimum(n_groups[0], n_groups[1])

        def run_groups(lo, hi, heads):
            def body(g, carry):
                step(j_top - group * (g + 1) + 1, group, heads, online)
                return carry
            lax.fori_loop(lo, hi, body, 0)

        run_groups(0, n_joint, (0, 1))
        for hh in range(2):
            run_groups(n_joint, n_groups[hh], (hh,))
            done = n_groups[hh] * group
            size = group // 2
            while size >= 1:
                take = ((n_live[hh] - done) & size) != 0

                @pl.when(take)
                def _(done=done, size=size, hh=hh):
                    step(j_top - done - size + 1, size, (hh,), online)

                done = done + jnp.where(take, size, 0)
                size //= 2

    @pl.when(frozen_ok)
    def _():
        all_blocks(online=False)

    @pl.when(jnp.logical_not(frozen_ok))
    def _():
        all_blocks(online=True)

    acca, accb = acc_ref[0], acc_ref[1]
    ot = jnp.concatenate([acca[0:half, :] / acca[half:half + 1, :],
                          accb[half:LANES, :] / accb[0:1, :]], axis=0)
    o_ref[...] = (ot.T * gate_ref[...].astype(F32)).astype(BF16)


def _fox_attn_prompt(dfirst, dlast, thr, frozen, qt, aqt, kb, ak, vt, gate, batch, seq, n):
    nq = seq // ATTN_Q
    qspec_t = pl.BlockSpec((LANES, ATTN_Q), lambda b, p, i, *_: (p, b * nq + i))
    tok = pl.BlockSpec((ATTN_Q, LANES), lambda b, p, i, *_: (b * nq + i, p))
    kspec = pl.BlockSpec((seq, LANES), lambda b, p, i, *_: (b, p))
    vspec_t = pl.BlockSpec((LANES, seq), lambda b, p, i, *_: (p, b))
    return pl.pallas_call(
        functools.partial(_fox_attn_body, nq=nq, nkb=seq // ATTN_K),
        out_shape=jax.ShapeDtypeStruct((n, D_MODEL), BF16),
        grid_spec=pltpu.PrefetchScalarGridSpec(
            num_scalar_prefetch=4,
            grid=(batch, FOX_PAIRS, nq),
            in_specs=[qspec_t, qspec_t, kspec, kspec, vspec_t, tok],
            out_specs=tok,
            scratch_shapes=[pltpu.VMEM((2, 2 * LANES, ATTN_Q), BF16), pltpu.VMEM((2, 8, ATTN_Q), F32),
                            pltpu.VMEM((2, LANES, ATTN_Q), F32)]),
        compiler_params=_params("parallel", "parallel", "arbitrary"),
        name="fox_attn_prompt",
    )(dfirst, dlast, thr, frozen, qt, aqt, kb, ak, vt, gate)


def _pair_split(q2):
    lane = lax.broadcasted_iota(jnp.int32, q2.shape, 1)
    left = lane < FOX_HEAD_DIM
    zero = jnp.zeros_like(q2)
    return left, (jnp.where(left, q2, zero), jnp.where(left, zero, q2))


def _sample_bias_body(cl_ref, ln_ref, rt_o, ncol_o, nt_o):
    past = cl_ref.shape[2]
    n_new = ln_ref.shape[0]
    tri = _tri(MXU_DIM, upper=True)
    carry = jnp.zeros((FOX_HEADS, 1), F32)
    prefix = []
    for b in range(past // MXU_DIM):
        p = _cumsum_cols(tri, cl_ref[0, :, b * MXU_DIM:(b + 1) * MXU_DIM]) + carry
        carry = p[:, MXU_DIM - 1:MXU_DIM]
        prefix.append(p)
    for b, p in enumerate(prefix):
        rt_o[0, :, b * MXU_DIM:(b + 1) * MXU_DIM] = carry - p
    ln =jnp.concatenate([ln_ref[...], jnp.zeros((LANES - n_new, LANES), F32)], axis=0)
    nn = _cumsum_rows(_tri(LANES), ln)
    ncol_o[0] = nn[0:n_new, :]
    nt_o[0] = nn.T[0:FOX_HEADS, 0:n_new]


def _sample_bias(cl_t, lf, n_new, row0, dec_batch, layer):
    past = cl_t.shape[2]
    blk0 = row0 // n_new
    return pl.pallas_call(
        _sample_bias_body,
        out_shape=(jax.ShapeDtypeStruct((dec_batch, FOX_HEADS, past), F32),
                   jax.ShapeDtypeStruct((dec_batch, n_new, LANES), F32),
                   jax.ShapeDtypeStruct((dec_batch, FOX_HEADS, n_new), F32)),
        grid=(dec_batch,),
        in_specs=[pl.BlockSpec((1, FOX_HEADS, past), lambda b: (layer * dec_batch + b, 0, 0)),
                  pl.BlockSpec((n_new, LANES), lambda b: (blk0 + b, 0))],
        out_specs=(pl.BlockSpec((1, FOX_HEADS, past), lambda b: (b, 0, 0)),
                   pl.BlockSpec((1, n_new, LANES), lambda b: (b, 0, 0)),
                   pl.BlockSpec((1, FOX_HEADS, n_new), lambda b: (b, 0, 0))),
        compiler_params=_params("parallel"),
        name="fox_sample_bias",
    )(cl_t, lf)


def _fox_sample_body(q_ref, vn_ref, kn_ref, gate_ref, kc_ref, vc_ref, rt_ref, nc_ref, nt_ref,
                     y_any, o_ref):
    del y_any
    n_new = q_ref.shape[0]
    left, qs = _pair_split(q_ref[...])
    kc = kc_ref[0].astype(BF16)
    vc = vc_ref[0].astype(BF16)
    kn = kn_ref[...]
    vn = vn_ref[...]
    rt = rt_ref[0, 0]
    nc = nc_ref[0, 0]
    nt = nt_ref[0, 0]
    row = lax.broadcasted_iota(jnp.int32, (n_new, n_new), 0)
    col = lax.broadcasted_iota(jnp.int32, (n_new, n_new), 1)
    outs = []
    for hh in range(2):
        nq = nc[:, hh:hh + 1]
        s1 = _dot(qs[hh], kc) + (nq + rt[hh:hh + 1, :])
        s2 = lax.dot_general(qs[hh], kn, _NT, preferred_element_type=F32) + (nq - nt[hh:hh + 1, :])
        s2 = jnp.where(col <= row, s2, NEG_BIG)
        m = jnp.maximum(jnp.max(s1, axis=1, keepdims=True), jnp.max(s2, axis=1, keepdims=True))
        p1 = jnp.exp(s1 - m)
        p2 = jnp.exp(s2 - m)
        l = jnp.sum(p1, axis=1, keepdims=True) + jnp.sum(p2, axis=1, keepdims=True)
        pv1 = lax.dot_general(p1.astype(BF16), vc, _NT, preferred_element_type=F32)
        outs.append((pv1 + _dot(p2.astype(BF16), vn)) / l)
    o = jnp.where(left, outs[0], outs[1])
    o_ref[...] = (o * gate_ref[...].astype(F32)).astype(BF16)


def _fox_attn_sample(qs, vs, kb, gate, kc, vc, rt, nc, nt, y_all, n_new, row0, layer):
    dec_batch, _, _, past = rt.shape
    blk0 = row0 // n_new
    own = pl.BlockSpec((n_new, LANES), lambda b, p: (b, p))
    new = pl.BlockSpec((n_new, LANES), lambda b, p: (blk0 + b, p))
    cache = pl.BlockSpec((1, LANES, past), lambda b, p: (layer * dec_batch + b, p, 0))
    return pl.pallas_call(
        _fox_sample_body,
        out_shape=jax.ShapeDtypeStruct(y_all.shape, BF16),
        grid=(dec_batch, FOX_PAIRS),
        in_specs=[own, own, own, own, cache, cache,
                  pl.BlockSpec((1, 1, 2, past), lambda b, p: (b, p, 0, 0)),
                  pl.BlockSpec((1, 1, n_new, 2), lambda b, p: (b, p, 0, 0)),
                  pl.BlockSpec((1, 1, 2, n_new), lambda b, p: (b, p, 0, 0)),
                  pl.BlockSpec(memory_space=pl.ANY)],
        out_specs=new,
        input_output_aliases={9: 0},
        compiler_params=_params("parallel", "parallel"),
        name="fox_attn_sample",
    )(qs, vs, kb, gate, kc, vc, rt, nc, nt, y_all)


def _ret_log_gammas():
    return [float(v) for v in np.log1p(-np.exp2(-5.0 - np.arange(RET_HEADS, dtype=np.float32)))]


def _ret_proj_body(x_ref, g_ref, w_ref, cos_ref, sin_ref, q_o, k_o, v_o, sg_o):
    h = _rmsnorm(x_ref[...], g_ref[...]).astype(BF16)
    cos = cos_ref[...]
    sin = sin_ref[...]
    a, b2 = RET_QK_WIDTH, 2 * RET_QK_WIDTH

    def rotary(t, scale):
        outs = []
        for hd in range(RET_HEADS):
            th = t[:, hd * RET_QK_DIM:(hd + 1) * RET_QK_DIM]
            outs.append((th * cos + pltpu.roll(th, RET_QK_DIM // 2, 1) * sin) * scale)
        return jnp.concatenate(outs, axis=1).astype(BF16)

    q_o[...] = rotary(_dot(h, w_ref[:, 0:a]), 1.0)
    k_o[...] = rotary(_dot(h, w_ref[:, a:b2]), RET_QK_DIM ** -0.5)
    v_o[...] = _dot(h, w_ref[:, b2:b2 + RET_V_WIDTH]).astype(BF16)
    g = _dot(h, w_ref[:, b2 + RET_V_WIDTH:b2 + 2 * RET_V_WIDTH])
    sg_o[...] = (g * jax.nn.sigmoid(g)).astype(BF16)


def _ret_proj(x, gain, w, cos, sin):
    n = x.shape[0]
    tile = lambda width: pl.BlockSpec((TOKEN_TILE, width), lambda i: (i, 0))
    return pl.pallas_call(
        _ret_proj_body,
        out_shape=(jax.ShapeDtypeStruct((n, RET_QK_WIDTH), BF16), jax.ShapeDtypeStruct((n, RET_QK_WIDTH), BF16),
                   jax.ShapeDtypeStruct((n, RET_V_WIDTH), BF16), jax.ShapeDtypeStruct((n, RET_V_WIDTH), BF16)),
        grid=(n // TOKEN_TILE,),
        in_specs=[tile(D_MODEL), _resident((1, D_MODEL)), _resident(w.shape), tile(LANES), tile(LANES)],
        out_specs=(tile(RET_QK_WIDTH), tile(RET_QK_WIDTH), tile(RET_V_WIDTH), tile(RET_V_WIDTH)),
        compiler_params=_params("parallel"),
        name="ret_proj",
    )(x, gain, w, cos, sin)


def _ret_decay_matrix(t, lg):
    row = lax.broadcasted_iota(jnp.int32, (t, t), 0)
    col = lax.broadcasted_iota(jnp.int32, (t, t), 1)
    dist = jnp.abs(row - col).astype(F32)
    visible = (col // CHUNK) <= (row // CHUNK)
    return jnp.where(visible, jnp.exp(dist * lg), 0.0)


def _ret_tile(q_ref, k_ref, v_ref, sg_ref, y_ref, state_in, state_out, decay, t):
    lgs = _ret_log_gammas()
    pos = lax.broadcasted_iota(jnp.int32, (t, 1), 0).astype(F32)
    for hd in range(RET_HEADS):
        lg = lgs[hd]
        qs = slice(hd * RET_QK_DIM, (hd + 1) * RET_QK_DIM)
        vs = slice(hd * RET_V_DIM, (hd + 1) * RET_V_DIM)
        qh = q_ref[:, qs]
        kh = k_ref[:, qs]
        vh = v_ref[:, vs]
        state = state_in(hd)
        sc = lax.dot_general(qh, kh, _NT, preferred_element_type=F32) * decay(hd)
        o = _dot(sc.astype(BF16), vh) + _dot(qh, state.astype(BF16)) * jnp.exp(pos * lg)
        kd = (kh.astype(F32) * jnp.exp((t - pos) * lg)).astype(BF16)
        state_out(hd, state * math.exp(t * lg) + lax.dot_general(kd, vh, _TN, preferred_element_type=F32))
        mu = jnp.mean(o, axis=-1, keepdims=True)
        dev = o - mu
        var = jnp.mean(dev * dev, axis=-1, keepdims=True)
        y_ref[:, vs] = (sg_ref[:, vs].astype(F32) * (dev * lax.rsqrt(var + GN_EPS))).astype(BF16)


def _ret_prompt_body(q_ref, k_ref, v_ref, sg_ref, y_ref, st_ref, decay_ref):
    t = TOKEN_TILE

    @pl.when((pl.program_id(0) == 0) & (pl.program_id(1) == 0))
    def _():
        for hd, lg in enumerate(_ret_log_gammas()):
            decay_ref[hd] = _ret_decay_matrix(t, lg)

    @pl.when(pl.program_id(1) == 0)
    def _():
        st_ref[...] = jnp.zeros_like(st_ref)

    def state_out(hd, val):
        st_ref[0, hd] = val

    _ret_tile(q_ref, k_ref, v_ref, sg_ref, y_ref, lambda hd: st_ref[0, hd], state_out,
              lambda hd: decay_ref[hd], t)


def _ret_prompt(q, k, v, sg, batch, seq):
    nt = seq // TOKEN_TILE
    tile = lambda width: pl.BlockSpec((TOKEN_TILE, width), lambda b, i: (b * nt + i, 0))
    return pl.pallas_call(
        _ret_prompt_body,
        out_shape=(jax.ShapeDtypeStruct((q.shape[0], RET_V_WIDTH), BF16),
                   jax.ShapeDtypeStruct((batch, RET_HEADS, RET_QK_DIM, RET_V_DIM), F32)),
        grid=(batch, nt),
        in_specs=[tile(RET_QK_WIDTH), tile(RET_QK_WIDTH), tile(RET_V_WIDTH), tile(RET_V_WIDTH)],
        out_specs=(tile(RET_V_WIDTH),
                   pl.BlockSpec((1, RET_HEADS, RET_QK_DIM, RET_V_DIM), lambda b, i: (b, 0, 0, 0))),
        scratch_shapes=[pltpu.VMEM((RET_HEADS, TOKEN_TILE, TOKEN_TILE), F32)],
        compiler_params=_params("arbitrary", "arbitrary"),
        name="ret_prompt",
    )(q, k, v, sg)


def _ret_sample_body(q_ref, k_ref, v_ref, sg_ref, st_in_ref, y_any, y_ref, st_out_ref):
    del y_any
    t = q_ref.shape[0]
    lgs = _ret_log_gammas()

    def state_out(hd, val):
        st_out_ref[0, hd] = val

    _ret_tile(q_ref, k_ref, v_ref, sg_ref, y_ref, lambda hd: st_in_ref[0, hd], state_out,
              lambda hd: _ret_decay_matrix(t, lgs[hd]), t)


def _ret_sample(q, k, v, sg, state, y_all, n_new, row0):
    dec_batch = state.shape[0]
    blk0 = row0 // n_new
    tile = lambda width: pl.BlockSpec((n_new, width), lambda b: (blk0 + b, 0))
    st = pl.BlockSpec((1, RET_HEADS, RET_QK_DIM, RET_V_DIM), lambda b: (b, 0, 0, 0))
    return pl.pallas_call(
        _ret_sample_body,
        out_shape=(jax.ShapeDtypeStruct(y_all.shape, BF16), jax.ShapeDtypeStruct(state.shape, F32)),
        grid=(dec_batch,),
        in_specs=[tile(RET_QK_WIDTH), tile(RET_QK_WIDTH), tile(RET_V_WIDTH), tile(RET_V_WIDTH), st,
                  pl.BlockSpec(memory_space=pl.ANY)],
        out_specs=(tile(RET_V_WIDTH), st),
        input_output_aliases={5: 0},
        compiler_params=_params("parallel"),
        name="ret_sample",
    )(q, k, v, sg, state, y_all)


def _rotary_tables(pos):
    half = RET_QK_DIM // 2
    inv_freq = ROPE_BASE ** (-jnp.linspace(0.0, 1.0, half, dtype=F32))
    ang = pos[:, None] * inv_freq[None, :]
    cos, sin = jnp.cos(ang), jnp.sin(ang)
    return jnp.concatenate([cos, cos], axis=1), jnp.concatenate([-sin, sin], axis=1)


def _head_tiled(gain):
    return jnp.tile(gain.astype(F32), FOX_HEADS)


def kernel(x_prompt, x_sample, cache_fox_k, cache_fox_v, cache_fox_logf, state_ret, norm_gains, w_ffn_in,
           w_ffn_out, fox_w_in, fox_b_f, fox_q_gain, fox_k_gain, fox_w_out, ret_w_in, ret_w_out, final_gain):
    batch, seq, _ = x_prompt.shape
    dec_batch, n_new, _ = x_sample.shape
    past = cache_fox_k.shape[2]
    depth = norm_gains.shape[0]
    n_prompt = batch * seq
    n_sample = dec_batch * n_new
    assert seq % TOKEN_TILE == 0 and n_sample == TOKEN_TILE and TOKEN_TILE % n_new == 0
    assert seq % ATTN_Q == 0 and ATTN_Q % ATTN_K == 0 and past % MXU_DIM == 0
    assert TOKEN_TILE % CHUNK == 0 and n_new <= CHUNK
    n_fox = (depth + 1) // 2

    xs = (x_prompt.reshape(n_prompt, D_MODEL), x_sample.reshape(n_sample, D_MODEL))
    pos =jnp.concatenate([jnp.tile(jnp.arange(seq, dtype=F32), batch),
                           jnp.tile(past + jnp.arange(n_new, dtype=F32), dec_batch)])
    cos, sin = _rotary_tables(pos)
    grp_idx = np.arange(MXU_DIM) // FOX_HEAD_DIM
    grp = jnp.asarray(grp_idx[:, None] == grp_idx[None, :], dtype=BF16)
    sel_k, sel_q, const_k, const_q = _aug_tables()
    selk, ck = jnp.asarray(sel_k, BF16), jnp.asarray(const_k, F32)
    selqt, cqt = jnp.asarray(sel_q.T, BF16), jnp.asarray(const_q.T, F32)
    fgain = final_gain.reshape(1, D_MODEL)

    feature_major = lambda c: c.transpose(0, 1, 3, 4, 2).reshape(n_fox * dec_batch, D_MODEL, past)
    cache_k_t, cache_v_t = feature_major(cache_fox_k), feature_major(cache_fox_v)
    cache_lf_t = cache_fox_logf.transpose(0, 1, 3, 2).reshape(n_fox * dec_batch, FOX_HEADS, past)
    w_in_bf, w_out_bf = w_ffn_in.astype(BF16), w_ffn_out.astype(BF16)
    rs_p, fl_s, rs_s = [], [], []
    fox_stacked = ()
    for layer in range(depth):
        g = norm_gains[layer].reshape(3, 1, D_MODEL)
        x = _ffn(xs, g[0], w_in_bf, w_out_bf, (layer, 0), n_prompt, n_sample)
        j = layer // 2
        if layer % 2 == 0:
            w = fox_w_in[j]
            wf = jnp.pad(w[:, 4 * D_MODEL:], ((0, 0), (0, LANES - FOX_HEADS)))
            bf = jnp.pad(fox_b_f[j], (0, LANES - FOX_HEADS))
            wt = jnp.concatenate([w[:, 0:D_MODEL], w[:, 2 * D_MODEL:3 * D_MODEL]], axis=1).T
            qg, kg = _head_tiled(fox_q_gain[j]), _head_tiled(fox_k_gain[j])
            token_major = (x, g[1], w[:, :4 * D_MODEL].astype(BF16), wf.astype(BF16), bf.reshape(1, LANES))
            kg2, qg2 = kg.reshape(1, D_MODEL), qg.reshape(1, D_MODEL)
            bound = 1.01 * (FOX_HEAD_DIM ** 0.5) * jnp.max(jnp.abs(fox_q_gain[j])) * jnp.max(jnp.abs(fox_k_gain[j]))
            shift = jnp.full((LANES, 1), bound * LOG2E, F32)
            proj_ins = token_major + (
                kg2, grp, selk, ck, wt.astype(BF16), wf.T.astype(BF16), bf.reshape(LANES, 1),
                (qg * (FOX_Q_SCALE * LOG2E)).reshape(D_MODEL, 1), selqt, cqt, shift)
            qt, aqt, vt, kb, ak, gate, _, d, *stk_p = _fox_proj(
                proj_ins, fox_stacked[:3], j, n_fox, n_prompt, seq // TOKEN_TILE)
            qs, ks, vs, gate_s, lf_s, *stk_s = _fox_proj_sample(
                token_major + (qg2, kg2, grp), fox_stacked[3:], j, n_fox, n_prompt // TOKEN_TILE)
            fox_stacked = (*stk_p, *stk_s)
            dh = d[:, :FOX_HEADS].reshape(batch, seq, FOX_HEADS)
            dfirst = dh[:, 0::ATTN_Q, :].transpose(0, 2, 1).reshape(-1)
            dlast = dh[:, ATTN_K - 1::ATTN_K, :].transpose(0, 2, 1).reshape(-1)
            thr = (2.0 * bound + SKIP_MARGIN).reshape(1).astype(F32)
            frozen = (2.0 * LOG2E * bound < FROZEN_MAX_RANGE).reshape(1).astype(jnp.int32)
            y = _fox_attn_prompt(dfirst, dlast, thr, frozen, qt, aqt, kb, ak, vt, gate, batch, seq,
                                 n_prompt + n_sample)
            rt, nc, nt = _sample_bias(cache_lf_t, lf_s, n_new, 0, dec_batch, j)
            y = _fox_attn_sample(
                qs, vs, ks, gate_s, cache_k_t, cache_v_t, rt.reshape(dec_batch, FOX_PAIRS, 2, past),
                nc[:, :, :FOX_HEADS].reshape(dec_batch, n_new, FOX_PAIRS, 2).transpose(0, 2, 1, 3),
                nt.reshape(dec_batch, FOX_PAIRS, 2, n_new), y, n_new, n_prompt, j)
            mix = (y, fox_w_out[j].astype(BF16))
            fl_s.append(lf_s[:, :FOX_HEADS].reshape(dec_batch, n_new, FOX_HEADS))
        else:
            q, k, v, sg = _ret_proj(x, g[1], ret_w_in[j].astype(BF16), cos, sin)
            y, st_p = _ret_prompt(q, k, v, sg, batch, seq)
            y, st_s = _ret_sample(q, k, v, sg, state_ret[j], y, n_new, n_prompt)
            mix = (y, ret_w_out[j].astype(BF16))
            rs_p.append(st_p)
            rs_s.append(st_s)
        x = _ffn((x,), g[2], w_in_bf, w_out_bf, (layer, 1), n_prompt, n_sample,
                 mix=mix, final_gain=fgain if layer == depth - 1 else None)
        xs = (x,)
    y_prompt, y_sample = x
    kft_p, vft_p, lft_p, kf_s, vf_s = fox_stacked
    per_head = lambda a: a.reshape(n_fox, batch, FOX_HEADS, FOX_HEAD_DIM, seq).transpose(0, 1, 4, 2, 3)
    heads_s = (n_fox, dec_batch, n_new, FOX_HEADS, FOX_HEAD_DIM)
    return (y_prompt.reshape(batch, seq, D_MODEL), y_sample.reshape(dec_batch, n_new, D_MODEL),
            per_head(kft_p), per_head(vft_p), lft_p.transpose(0, 1, 3, 2), jnp.stack(rs_p),
            kf_s.reshape(heads_s), vf_s.reshape(heads_s), jnp.stack(fl_s), jnp.stack(rs_s))
```

```python
import functools
import math

import numpy as np
import jax
import jax.numpy as jnp
from jax import lax
from jax.experimental import pallas as pl
from jax.experimental.pallas import tpu as pltpu

F32 = jnp.float32
BF16 = jnp.bfloat16

D_MODEL = 1024
CHUNK = 64
FOX_HEADS = 16
FOX_HEAD_DIM = 64
FOX_PAIRS = FOX_HEADS // 2
RET_HEADS = 8
RET_QK_DIM = 128
RET_V_DIM = 256
RET_QK_WIDTH = RET_HEADS * RET_QK_DIM
RET_V_WIDTH = RET_HEADS * RET_V_DIM
D_FF = 2816
RMS_EPS = 1e-6
GN_EPS = 1e-5
ROPE_BASE = 10000.0

LANES = 128
MXU_DIM = 256
TOKEN_TILE = 512
FFN_CHUNK = MXU_DIM
ATTN_Q = 512
ATTN_K = 256
ATTN_GROUP = 4
OWN_EXTRA = 2
VMEM_LIMIT = 56 * 1024 * 1024
NEG_BIG = -1e30
LOG2E = 1.4426950408889634
FOX_Q_SCALE = FOX_HEAD_DIM ** -0.5
AUG_PIECES = 3
SKIP_MARGIN = 32.0
FROZEN_MAX_RANGE = 100.0

_NT = (((1,), (1,)), ((), ()))
_TN = (((0,), (0,)), ((), ()))


def _params(*sem):
    return pltpu.CompilerParams(dimension_semantics=sem, vmem_limit_bytes=VMEM_LIMIT)


def _resident(shape):
    nd = len(shape)
    return pl.BlockSpec(shape, lambda *_: (0,) * nd, pipeline_mode=pl.Buffered(1))


def _rmsnorm(x, gain):
    ms = jnp.mean(x * x, axis=-1, keepdims=True)
    return x * lax.rsqrt(ms + RMS_EPS) * gain


def _dot(a, b):
    return jnp.dot(a, b, preferred_element_type=F32)


def _split3(x):
    hi = x.astype(BF16)
    r = x - hi.astype(F32)
    mid = r.astype(BF16)
    lo = (r - mid.astype(F32)).astype(BF16)
    return hi, mid, lo


def _tri(n, upper=False):
    row = lax.broadcasted_iota(jnp.int32, (n, n), 0)
    col = lax.broadcasted_iota(jnp.int32, (n, n), 1)
    return jnp.where((row <= col) if upper else (row >= col), 1.0, 0.0).astype(BF16)


def _cumsum_rows(tri, x):
    hi, mid, lo = _split3(x)
    return _dot(tri, hi) + _dot(tri, mid) + _dot(tri, lo)


def _cumsum_cols(tri_upper, x):
    hi, mid, lo = _split3(x)
    return _dot(hi, tri_upper) + _dot(mid, tri_upper) + _dot(lo, tri_upper)


def _log_sigmoid(z):
    return jnp.minimum(z, 0.0) - jnp.log1p(jnp.exp(-jnp.abs(z)))


def _ffn_body(*refs, first, mixed, final, n_prompt_tiles):
    refs = list(refs)
    x_refs = [refs.pop(0) for _ in range(2 if first else 1)]
    y_ref, wmix_ref = (refs.pop(0), refs.pop(0)) if mixed else (None, None)
    g_ref, win_ref, wout_ref = refs.pop(0), refs.pop(0), refs.pop(0)
    fg_ref = refs.pop(0) if final else None
    o_refs = [refs.pop(0) for _ in range(2 if final else 1)]
    (acc_ref,) = refs
    is_prompt = pl.program_id(0) < n_prompt_tiles
    x = jnp.where(is_prompt, x_refs[0][...], x_refs[1][...]) if first else x_refs[0][...]
    if mixed:
        x = x + _dot(y_ref[...], wmix_ref[...])
    h = _rmsnorm(x, g_ref[...]).astype(BF16)
    for c in range(D_FF // FFN_CHUNK):
        lo, hi = c * FFN_CHUNK, (c + 1) * FFN_CHUNK
        gate = _dot(h, win_ref[0, 0, :, lo:hi])
        up = _dot(h, win_ref[0, 0, :, D_FF + lo:D_FF + hi])
        act = (gate * jax.nn.sigmoid(gate) * up).astype(BF16)
        part = _dot(act, wout_ref[0, 0, lo:hi, :])
        if c == 0:
            acc_ref[...] = part
        else:
            acc_ref[...] += part
    y = x + 0.5 * acc_ref[...]
    if not final:
        o_refs[0][...] = y
        return
    y = _rmsnorm(y, fg_ref[...])

    @pl.when(is_prompt)
    def _():
        o_refs[0][...] = y

    @pl.when(jnp.logical_not(is_prompt))
    def _():
        o_refs[1][...] = y


def _ffn(xs, gain, w_in, w_out, which, n_prompt, n_sample, mix=None, final_gain=None):
    first, mixed, final = len(xs) == 2, mix is not None, final_gain is not None
    n = n_prompt + n_sample
    n_prompt_tiles = n_prompt // TOKEN_TILE
    tile = pl.BlockSpec((TOKEN_TILE, D_MODEL), lambda i: (i, 0))
    prompt_tile = pl.BlockSpec((TOKEN_TILE, D_MODEL), lambda i: (jnp.minimum(i, n_prompt_tiles - 1), 0))
    sample_tile = pl.BlockSpec((TOKEN_TILE, D_MODEL), lambda i: (jnp.maximum(i - n_prompt_tiles, 0), 0))
    ins, in_specs = list(xs), ([prompt_tile, sample_tile] if first else [tile])
    if mixed:
        y, w_mix = mix
        ins += [y, w_mix]
        in_specs += [pl.BlockSpec((TOKEN_TILE, y.shape[1]), lambda i: (i, 0)), _resident(w_mix.shape)]
    ins += [gain, w_in, w_out]
    pick = lambda *_: (*which, 0, 0)
    in_specs += [_resident((1, D_MODEL)),
                 pl.BlockSpec((1, 1, D_MODEL, 2 * D_FF), pick, pipeline_mode=pl.Buffered(1)),
                 pl.BlockSpec((1, 1, D_FF, D_MODEL), pick, pipeline_mode=pl.Buffered(1))]
    if final:
        ins.append(final_gain)
        in_specs.append(_resident((1, D_MODEL)))
        out_shape = (jax.ShapeDtypeStruct((n_prompt, D_MODEL), F32), jax.ShapeDtypeStruct((n_sample, D_MODEL), F32))
        out_specs = (prompt_tile, sample_tile)
    else:
        out_shape, out_specs = jax.ShapeDtypeStruct((n, D_MODEL), F32), tile
    return pl.pallas_call(
        functools.partial(_ffn_body, first=first, mixed=mixed, final=final, n_prompt_tiles=n_prompt_tiles),
        out_shape=out_shape,
        grid=(n // TOKEN_TILE,),
        in_specs=in_specs,
        out_specs=out_specs,
        scratch_shapes=[pltpu.VMEM((TOKEN_TILE, D_MODEL), F32)],
        compiler_params=_params("arbitrary"),
        name="ffn_final" if final else ("ffn_mixed" if mixed else "ffn"),
    )(*ins)


def _aug_tables():
    sel_k = np.zeros((AUG_PIECES * LANES, D_MODEL), np.float32)
    sel_q = np.zeros((AUG_PIECES * LANES, D_MODEL), np.float32)
    const_k = np.zeros((1, D_MODEL), np.float32)
    const_q = np.zeros((1, D_MODEL), np.float32)
    for p in range(FOX_PAIRS):
        for hh in range(2):
            for piece in range(AUG_PIECES):
                src = piece * LANES + 2 * p + hh
                sel_k[src, p * LANES + 3 * hh + piece] = 1.0
                sel_q[src, p * LANES + 6 + 3 * hh + piece] = 1.0
                const_q[0, p * LANES + 3 * hh + piece] = -1.0
                const_k[0, p * LANES + 6 + 3 * hh + piece] = 1.0
    return sel_k, sel_q, const_k, const_q


def _head_rmsnorm(t, grp, gain):
    outs = []
    for c in range(D_MODEL // MXU_DIM):
        blk = t[:, c * MXU_DIM:(c + 1) * MXU_DIM]
        sq = blk * blk
        hi = sq.astype(BF16)
        lo = (sq - hi.astype(F32)).astype(BF16)
        ss = _dot(hi, grp) + _dot(lo, grp)
        outs.append(blk * lax.rsqrt(ss * (1.0 / FOX_HEAD_DIM) + RMS_EPS))
    return jnp.concatenate(outs, axis=1) * gain


def _head_rmsnorm_t(t, grp, gain):
    outs = []
    for c in range(D_MODEL // MXU_DIM):
        blk = t[c * MXU_DIM:(c + 1) * MXU_DIM, :]
        sq = blk * blk
        hi = sq.astype(BF16)
        lo = (sq - hi.astype(F32)).astype(BF16)
        ss = _dot(grp, hi) + _dot(grp, lo)
        outs.append(blk * lax.rsqrt(ss * (1.0 / FOX_HEAD_DIM) + RMS_EPS))
    return jnp.concatenate(outs, axis=0) * gain


N_FOX_PROJ_INPUTS = 16
N_FOX_SAMPLE_INPUTS = 8


def _fox_token_major(h, w_ref, wf_ref, bf_ref, kg_ref, grp, kb_o, gate_o, lf_o):
    w = D_MODEL
    k = _head_rmsnorm(_dot(h, w_ref[:, w:2 * w]), grp, kg_ref[...])
    kb_o[...] = k.astype(BF16)
    v = _dot(h, w_ref[:, 2 * w:3 * w])
    gate_o[...] = jax.nn.sigmoid(_dot(h, w_ref[:, 3 * w:4 * w])).astype(BF16)
    logf = _log_sigmoid(_dot(h, wf_ref[...]) + bf_ref[...])
    lf_o[...] = logf
    return k, v, logf


def _fox_proj_sample_body(*refs, n_stacked):
    x_ref, g_ref, w_ref, wf_ref, bf_ref, qg_ref, kg_ref, grp_ref = refs[:N_FOX_SAMPLE_INPUTS]
    qs_o, kb_o, vs_o, gate_o, lf_o, kf_o, vf_o = refs[N_FOX_SAMPLE_INPUTS + n_stacked:]
    h = _rmsnorm(x_ref[...], g_ref[...]).astype(BF16)
    grp = grp_ref[...]
    k, v, _ = _fox_token_major(h, w_ref, wf_ref, bf_ref, kg_ref, grp, kb_o, gate_o, lf_o)
    for hd in range(FOX_HEADS):
        cols = slice(hd * FOX_HEAD_DIM, (hd + 1) * FOX_HEAD_DIM)
        rows = pl.ds(hd, TOKEN_TILE, stride=FOX_HEADS)
        kf_o.at[0][rows, :] = k[:, cols]
        vf_o.at[0][rows, :] = v[:, cols]
    vs_o[...] = v.astype(BF16)
    q = _head_rmsnorm(_dot(h, w_ref[:, 0:D_MODEL]), grp, qg_ref[...])
    qs_o[...] = (q * FOX_Q_SCALE).astype(BF16)


def _fox_proj_sample(ins, stacked, layer, n_layers, tile0):
    assert len(ins) == N_FOX_SAMPLE_INPUTS
    n = ins[0].shape[0] - tile0 * TOKEN_TILE
    tile = pl.BlockSpec((TOKEN_TILE, D_MODEL), lambda i: (i, 0))
    slab = pl.BlockSpec((1, TOKEN_TILE * FOX_HEADS, FOX_HEAD_DIM), lambda i: (layer, i, 0))
    wide = jax.ShapeDtypeStruct((n, D_MODEL), BF16)
    stack = jax.ShapeDtypeStruct((n_layers, n * FOX_HEADS, FOX_HEAD_DIM), F32)
    n_out_before_stacks = 5
    return pl.pallas_call(
        functools.partial(_fox_proj_sample_body, n_stacked=len(stacked)),
        out_shape=(wide, wide, wide, wide, jax.ShapeDtypeStruct((n, LANES), F32), stack, stack),
        grid=(n // TOKEN_TILE,),
        in_specs=([pl.BlockSpec((TOKEN_TILE, D_MODEL), lambda i: (tile0 + i, 0))]
                  + [_resident(a.shape) for a in ins[1:]] + [pl.BlockSpec(memory_space=pl.ANY)] * len(stacked)),
        out_specs=(tile, tile, tile, tile, pl.BlockSpec((TOKEN_TILE, LANES), lambda i: (i, 0)), slab, slab),
        input_output_aliases={N_FOX_SAMPLE_INPUTS + s: n_out_before_stacks + s for s in range(len(stacked))},
        compiler_params=_params("parallel"),
        name="fox_proj_sample",
    )(*ins, *stacked)


def _fox_proj_body(*refs, tiles_per_seq, n_stacked):
    (x_ref, g_ref, w_ref, wf_ref, bf_ref, kg_ref, grp_ref, selk_ref, ck_ref,
     wt_ref, wft_ref, bft_ref, qgt_ref, selqt_ref, cqt_ref, shift_ref) = refs[:N_FOX_PROJ_INPUTS]
    (qt_o, aqt_o, vt_o, kb_o, ak_o, gate_o, lf_o, d_o, kft_o, vft_o, lft_o,
     carry_ref, carryt_ref) = refs[N_FOX_PROJ_INPUTS + n_stacked:]
    i = pl.program_id(0)
    tm, w = TOKEN_TILE, D_MODEL
    hf = _rmsnorm(x_ref[...], g_ref[...])
    h = hf.astype(BF16)
    ht = hf.T.astype(BF16)
    grp = grp_ref[...]

    @pl.when(i % tiles_per_seq == 0)
    def _():
        carry_ref[...] = jnp.zeros_like(carry_ref)
        carryt_ref[...] = jnp.zeros_like(carryt_ref)

    k, _, logf = _fox_token_major(h, w_ref, wf_ref, bf_ref, kg_ref, grp, kb_o, gate_o, lf_o)
    kft_o[0, 0] = k.T
    d = _cumsum_rows(_tri(tm), logf) + carry_ref[0:1, :]
    carry_ref[...] = jnp.broadcast_to(d[tm - 1:tm, :], carry_ref.shape)
    d_o[...] = d
    pieces = jnp.concatenate(_split3(d * LOG2E), axis=1)
    ak_o[...] = (_dot(pieces, selk_ref[...]) + ck_ref[...]).astype(BF16)

    qt = _head_rmsnorm_t(_dot(wt_ref[0:w, :], ht), grp, qgt_ref[...])
    qt_o[...] = qt.astype(BF16)
    vt = _dot(wt_ref[w:2 * w, :], ht)
    vft_o[0, 0] = vt
    vt_o[...] = vt.astype(BF16)
    logft = _log_sigmoid(_dot(wft_ref[...], ht) + bft_ref[...])
    lft_o[0, 0] = logft[0:FOX_HEADS, :]
    dt = _cumsum_cols(_tri(tm, upper=True), logft) + carryt_ref[:, 0:1]
    carryt_ref[...] = jnp.broadcast_to(dt[:, tm - 1:tm], carryt_ref.shape)
    piecest = jnp.concatenate(_split3(dt * LOG2E - shift_ref[...]), axis=0)
    aqt_o[...] = (_dot(selqt_ref[...], piecest) + cqt_ref[...]).astype(BF16)


def _fox_proj(ins, stacked, layer, n_layers, n, tiles_per_seq):
    assert len(ins) == N_FOX_PROJ_INPUTS
    tps = tiles_per_seq
    batch, seq = n // (tps * TOKEN_TILE), tps * TOKEN_TILE
    tile = pl.BlockSpec((TOKEN_TILE, D_MODEL), lambda i: (i, 0))
    tile_t = pl.BlockSpec((D_MODEL, TOKEN_TILE), lambda i: (0, i))
    nar = pl.BlockSpec((TOKEN_TILE, LANES), lambda i: (i, 0))
    slab = pl.BlockSpec((1, 1, D_MODEL, TOKEN_TILE), lambda i: (layer, i // tps, 0, i % tps))
    slab_lf = pl.BlockSpec((1, 1, FOX_HEADS, TOKEN_TILE), lambda i: (layer, i // tps, 0, i % tps))
    wide = jax.ShapeDtypeStruct((n, D_MODEL), BF16)
    wide_t = jax.ShapeDtypeStruct((D_MODEL, n), BF16)
    narrow = jax.ShapeDtypeStruct((n, LANES), F32)
    stack = jax.ShapeDtypeStruct((n_layers, batch, D_MODEL, seq), F32)
    stack_lf = jax.ShapeDtypeStruct((n_layers, batch, FOX_HEADS, seq), F32)
    n_out_before_stacks = 8
    return pl.pallas_call(
        functools.partial(_fox_proj_body, tiles_per_seq=tiles_per_seq, n_stacked=len(stacked)),
        out_shape=(wide_t, wide_t, wide_t, wide, wide, wide, narrow, narrow, stack, stack, stack_lf),
        grid=(n // TOKEN_TILE,),
        in_specs=([tile] + [_resident(a.shape) for a in ins[1:]]
                  + [pl.BlockSpec(memory_space=pl.ANY)] * len(stacked)),
        out_specs=(tile_t, tile_t, tile_t, tile, tile, tile, nar, nar, slab, slab, slab_lf),
        input_output_aliases={N_FOX_PROJ_INPUTS + s: n_out_before_stacks + s for s in range(len(stacked))},
        scratch_shapes=[pltpu.VMEM((8, LANES), F32), pltpu.VMEM((LANES, LANES), F32)],
        compiler_params=_params("arbitrary"),
        name="fox_proj",
    )(*ins, *stacked)


def _fox_attn_body(dfirst_ref, dlast_ref, thr_ref, frozen_ref, qt_ref, aqt_ref, k_ref, ak_ref, vt_ref, gate_ref,
                   o_ref, qh_ref, m_ref, acc_ref, *, nq, nkb):
    tq, tk = ATTN_Q, ATTN_K
    half = FOX_HEAD_DIM
    per_q = tq // tk
    b, p, i = pl.program_id(0), pl.program_id(1), pl.program_id(2)

    qt = jnp.concatenate([qt_ref[...], aqt_ref[...]], axis=0)
    sub = lax.broadcasted_iota(jnp.int32, qt.shape, 0)
    slot = sub - LANES
    keep_a = (sub < half) | ((slot >= 0) & (slot < 3)) | ((slot >= 6) & (slot < 9))
    zero = jnp.zeros_like(qt)
    qh_ref[0] = jnp.where(keep_a, qt, zero)
    keep_b = ((sub >= half) & (sub < LANES)) | ((slot >= 3) & (slot < 6)) | ((slot >= 9) & (slot < 12))
    qh_ref[1] = jnp.where(keep_b, qt, zero)
    def step(j0, nblocks, heads, online, own=False):
        rows = nblocks * tk
        first_own_row = rows - tq
        off = pl.multiple_of(j0 * tk, tk)
        kk = jnp.concatenate([k_ref[pl.ds(off, rows), :], ak_ref[pl.ds(off, rows), :]], axis=1)
        vt = vt_ref[:, pl.ds(off, rows)]
        top = lax.broadcasted_iota(jnp.int32, (LANES, rows), 0) < half
        ones = jnp.ones((LANES, rows), BF16)
        for hh in heads:
            vsel = jnp.where(top, vt, ones) if hh == 0 else jnp.where(top, ones, vt)
            s = _dot(kk, qh_ref[hh])
            if own:
                kpos = lax.broadcasted_iota(jnp.int32, (rows, tq), 0) - first_own_row
                qpos = lax.broadcasted_iota(jnp.int32, (rows, tq), 1)
                s = jnp.where(kpos <= qpos, s, NEG_BIG)
            if not online:
                pv = _dot(vsel, jnp.exp2(s).astype(BF16))
                acc_ref[hh] = pv if own else acc_ref[hh] + pv
                continue
            if own:
                m_new = jnp.max(s, axis=0, keepdims=True)
                acc_ref[hh] = _dot(vsel, jnp.exp2(s - m_new).astype(BF16))
                m_ref[hh] = jnp.broadcast_to(m_new, m_ref.shape[1:])
                continue
            m_old = m_ref[hh, 0:1, :]
            m_new = jnp.maximum(m_old, jnp.max(s, axis=0, keepdims=True))
            alpha = jnp.exp2(m_old - m_new)
            pr = jnp.exp2(s - m_new).astype(BF16)
            acc_ref[hh] = alpha * acc_ref[hh] + _dot(vsel, pr)
            m_ref[hh] = jnp.broadcast_to(m_new, m_ref.shape[1:])

    thr = thr_ref[0]
    frozen_ok = frozen_ref[0] != 0
    j_top = i * per_q - 1
    group = ATTN_GROUP

    def live_blocks(head):
        first = dfirst_ref[(b * FOX_HEADS + head) * nq + i]
        base = (b * FOX_HEADS + head) * nkb

        def near(n):
            j = jnp.maximum(j_top - n, 0)
            return (n <= j_top) & (first - dlast_ref[base + j] >= -thr)

        return lax.while_loop(near, lambda n: n + 1, jnp.int32(0))

    n_live = (live_blocks(2 * p), live_blocks(2 * p + 1))

    def all_blocks(online):
        @pl.when(i == 0)
        def _():
            step(0, per_q, (0, 1), online, own=True)

        @pl.when(i > 0)
        def _():
            step(i * per_q - OWN_EXTRA, per_q + OWN_EXTRA, (0, 1), online, own=True)

        taken = jnp.where(i > 0, OWN_EXTRA, 0)
        j_top = i * per_q - 1 - taken
        n_left = tuple(jnp.maximum(n - taken, 0) for n in n_live)
        n_groups = (n_left[0] // group, n_left[1] // group)
        n_joint = jnp.minimum(n_groups[0], n_groups[1])

        def run_groups(lo, hi, heads):
            def body(g, carry):
                step(j_top - group * (g + 1) + 1, group, heads, online)
                return carry
            lax.fori_loop(lo, hi, body, 0)

        run_groups(0, n_joint, (0, 1))
        for hh in range(2):
            run_groups(n_joint, n_groups[hh], (hh,))
            done = n_groups[hh] * group
            size = group // 2
            while size >= 1:
                take = ((n_left[hh] - done) & size) != 0

                @pl.when(take)
                def _(done=done, size=size, hh=hh):
                    step(j_top - done - size + 1, size, (hh,), online)

                done = done + jnp.where(take, size, 0)
                size //= 2

    @pl.when(frozen_ok)
    def _():
        all_blocks(online=False)

    @pl.when(jnp.logical_not(frozen_ok))
    def _():
        all_blocks(online=True)

    acca, accb = acc_ref[0], acc_ref[1]
    ot = jnp.concatenate([acca[0:half, :] / acca[half:half + 1, :],
                          accb[half:LANES, :] / accb[0:1, :]], axis=0)
    o_ref[...] = (ot.T * gate_ref[...].astype(F32)).astype(BF16)


def _fox_attn_prompt(dfirst, dlast, thr, frozen, qt, aqt, kb, ak, vt, gate, batch, seq, n):
    nq = seq // ATTN_Q
    qspec_t = pl.BlockSpec((LANES, ATTN_Q), lambda b, p, i, *_: (p, b * nq + i))
    tok = pl.BlockSpec((ATTN_Q, LANES), lambda b, p, i, *_: (b * nq + i, p))
    kspec = pl.BlockSpec((seq, LANES), lambda b, p, i, *_: (b, p))
    vspec_t = pl.BlockSpec((LANES, seq), lambda b, p, i, *_: (p, b))
    return pl.pallas_call(
        functools.partial(_fox_attn_body, nq=nq, nkb=seq // ATTN_K),
        out_shape=jax.ShapeDtypeStruct((n, D_MODEL), BF16),
        grid_spec=pltpu.PrefetchScalarGridSpec(
            num_scalar_prefetch=4,
            grid=(batch, FOX_PAIRS, nq),
            in_specs=[qspec_t, qspec_t, kspec, kspec, vspec_t, tok],
            out_specs=tok,
            scratch_shapes=[pltpu.VMEM((2, 2 * LANES, ATTN_Q), BF16), pltpu.VMEM((2, 8, ATTN_Q), F32),
                            pltpu.VMEM((2, LANES, ATTN_Q), F32)]),
        compiler_params=_params("parallel", "parallel", "arbitrary"),
        name="fox_attn_prompt",
    )(dfirst, dlast, thr, frozen, qt, aqt, kb, ak, vt, gate)


def _pair_split(q2):
    lane = lax.broadcasted_iota(jnp.int32, q2.shape, 1)
    left = lane < FOX_HEAD_DIM
    zero = jnp.zeros_like(q2)
    return left, (jnp.where(left, q2, zero), jnp.where(left, zero, q2))


def _sample_bias_body(cl_ref, ln_ref, rt_o, ncol_o, nt_o):
    past = cl_ref.shape[2]
    n_new = ln_ref.shape[0]
    tri = _tri(MXU_DIM, upper=True)
    carry = jnp.zeros((FOX_HEADS, 1), F32)
    prefix = []
    for b in range(past // MXU_DIM):
        p = _cumsum_cols(tri, cl_ref[0, :, b * MXU_DIM:(b + 1) * MXU_DIM]) + carry
        carry = p[:, MXU_DIM - 1:MXU_DIM]
        prefix.append(p)
    for b, p in enumerate(prefix):
        rt_o[0, :, b * MXU_DIM:(b + 1) * MXU_DIM] = carry - p
    ln =jnp.concatenate([ln_ref[...], jnp.zeros((LANES - n_new, LANES), F32)], axis=0)
    nn = _cumsum_rows(_tri(LANES), ln)
    ncol_o[0] = nn[0:n_new, :]
    nt_o[0] = nn.T[0:FOX_HEADS, 0:n_new]


def _sample_bias(cl_t, lf, n_new, row0, dec_batch, layer):
    past = cl_t.shape[2]
    blk0 = row0 // n_new
    return pl.pallas_call(
        _sample_bias_body,
        out_shape=(jax.ShapeDtypeStruct((dec_batch, FOX_HEADS, past), F32),
                   jax.ShapeDtypeStruct((dec_batch, n_new, LANES), F32),
                   jax.ShapeDtypeStruct((dec_batch, FOX_HEADS, n_new), F32)),
        grid=(dec_batch,),
        in_specs=[pl.BlockSpec((1, FOX_HEADS, past), lambda b: (layer * dec_batch + b, 0, 0)),
                  pl.BlockSpec((n_new, LANES), lambda b: (blk0 + b, 0))],
        out_specs=(pl.BlockSpec((1, FOX_HEADS, past), lambda b: (b, 0, 0)),
                   pl.BlockSpec((1, n_new, LANES), lambda b: (b, 0, 0)),
                   pl.BlockSpec((1, FOX_HEADS, n_new), lambda b: (b, 0, 0))),
        compiler_params=_params("parallel"),
        name="fox_sample_bias",
    )(cl_t, lf)


def _fox_sample_body(q_ref, vn_ref, kn_ref, gate_ref, kc_ref, vc_ref, rt_ref, nc_ref, nt_ref,
                     y_any, o_ref):
    del y_any
    n_new = q_ref.shape[0]
    left, qs = _pair_split(q_ref[...])
    kc = kc_ref[0].astype(BF16)
    vc = vc_ref[0].astype(BF16)
    kn = kn_ref[...]
    vn = vn_ref[...]
    rt = rt_ref[0, 0]
    nc = nc_ref[0, 0]
    nt = nt_ref[0, 0]
    row = lax.broadcasted_iota(jnp.int32, (n_new, n_new), 0)
    col = lax.broadcasted_iota(jnp.int32, (n_new, n_new), 1)
    outs = []
    for hh in range(2):
        nq = nc[:, hh:hh + 1]
        s1 = _dot(qs[hh], kc) + (nq + rt[hh:hh + 1, :])
        s2 = lax.dot_general(qs[hh], kn, _NT, preferred_element_type=F32) + (nq - nt[hh:hh + 1, :])
        s2 = jnp.where(col <= row, s2, NEG_BIG)
        m = jnp.maximum(jnp.max(s1, axis=1, keepdims=True), jnp.max(s2, axis=1, keepdims=True))
        p1 = jnp.exp(s1 - m)
        p2 = jnp.exp(s2 - m)
        l = jnp.sum(p1, axis=1, keepdims=True) + jnp.sum(p2, axis=1, keepdims=True)
        pv1 = lax.dot_general(p1.astype(BF16), vc, _NT, preferred_element_type=F32)
        outs.append((pv1 + _dot(p2.astype(BF16), vn)) / l)
    o = jnp.where(left, outs[0], outs[1])
    o_ref[...] = (o * gate_ref[...].astype(F32)).astype(BF16)


def _fox_attn_sample(qs, vs, kb, gate, kc, vc, rt, nc, nt, y_all, n_new, row0, layer):
    dec_batch, _, _, past = rt.shape
    blk0 = row0 // n_new
    own = pl.BlockSpec((n_new, LANES), lambda b, p: (b, p))
    new = pl.BlockSpec((n_new, LANES), lambda b, p: (blk0 + b, p))
    cache = pl.BlockSpec((1, LANES, past), lambda b, p: (layer * dec_batch + b, p, 0))
    return pl.pallas_call(
        _fox_sample_body,
        out_shape=jax.ShapeDtypeStruct(y_all.shape, BF16),
        grid=(dec_batch, FOX_PAIRS),
        in_specs=[own, own, own, own, cache, cache,
                  pl.BlockSpec((1, 1, 2, past), lambda b, p: (b, p, 0, 0)),
                  pl.BlockSpec((1, 1, n_new, 2), lambda b, p: (b, p, 0, 0)),
                  pl.BlockSpec((1, 1, 2, n_new), lambda b, p: (b, p, 0, 0)),
                  pl.BlockSpec(memory_space=pl.ANY)],
        out_specs=new,
        input_output_aliases={9: 0},
        compiler_params=_params("parallel", "parallel"),
        name="fox_attn_sample",
    )(qs, vs, kb, gate, kc, vc, rt, nc, nt, y_all)


def _ret_log_gammas():
    return [float(v) for v in np.log1p(-np.exp2(-5.0 - np.arange(RET_HEADS, dtype=np.float32)))]


def _ret_proj_body(x_ref, g_ref, w_ref, cos_ref, sin_ref, q_o, k_o, v_o, sg_o):
    h = _rmsnorm(x_ref[...], g_ref[...]).astype(BF16)
    cos = cos_ref[...]
    sin = sin_ref[...]
    a, b2 = RET_QK_WIDTH, 2 * RET_QK_WIDTH

    def rotary(t, scale):
        outs = []
        for hd in range(RET_HEADS):
            th = t[:, hd * RET_QK_DIM:(hd + 1) * RET_QK_DIM]
            outs.append((th * cos + pltpu.roll(th, RET_QK_DIM // 2, 1) * sin) * scale)
        return jnp.concatenate(outs, axis=1).astype(BF16)

    q_o[...] = rotary(_dot(h, w_ref[:, 0:a]), 1.0)
    k_o[...] = rotary(_dot(h, w_ref[:, a:b2]), RET_QK_DIM ** -0.5)
    v_o[...] = _dot(h, w_ref[:, b2:b2 + RET_V_WIDTH]).astype(BF16)
    g = _dot(h, w_ref[:, b2 + RET_V_WIDTH:b2 + 2 * RET_V_WIDTH])
    sg_o[...] = (g * jax.nn.sigmoid(g)).astype(BF16)


def _ret_proj(x, gain, w, cos, sin):
    n = x.shape[0]
    tile = lambda width: pl.BlockSpec((TOKEN_TILE, width), lambda i: (i, 0))
    return pl.pallas_call(
        _ret_proj_body,
        out_shape=(jax.ShapeDtypeStruct((n, RET_QK_WIDTH), BF16), jax.ShapeDtypeStruct((n, RET_QK_WIDTH), BF16),
                   jax.ShapeDtypeStruct((n, RET_V_WIDTH), BF16), jax.ShapeDtypeStruct((n, RET_V_WIDTH), BF16)),
        grid=(n // TOKEN_TILE,),
        in_specs=[tile(D_MODEL), _resident((1, D_MODEL)), _resident(w.shape), tile(LANES), tile(LANES)],
        out_specs=(tile(RET_QK_WIDTH), tile(RET_QK_WIDTH), tile(RET_V_WIDTH), tile(RET_V_WIDTH)),
        compiler_params=_params("parallel"),
        name="ret_proj",
    )(x, gain, w, cos, sin)


def _ret_decay_matrix(t, lg):
    row = lax.broadcasted_iota(jnp.int32, (t, t), 0)
    col = lax.broadcasted_iota(jnp.int32, (t, t), 1)
    dist = jnp.abs(row - col).astype(F32)
    visible = (col // CHUNK) <= (row // CHUNK)
    return jnp.where(visible, jnp.exp(dist * lg), 0.0)


def _ret_tile(q_ref, k_ref, v_ref, sg_ref, y_ref, state_in, state_out, decay, t):
    lgs = _ret_log_gammas()
    pos = lax.broadcasted_iota(jnp.int32, (t, 1), 0).astype(F32)
    for hd in range(RET_HEADS):
        lg = lgs[hd]
        qs = slice(hd * RET_QK_DIM, (hd + 1) * RET_QK_DIM)
        vs = slice(hd * RET_V_DIM, (hd + 1) * RET_V_DIM)
        qh = q_ref[:, qs]
        kh = k_ref[:, qs]
        vh = v_ref[:, vs]
        state = state_in(hd)
        sc = lax.dot_general(qh, kh, _NT, preferred_element_type=F32) * decay(hd)
        o = _dot(sc.astype(BF16), vh) + _dot(qh, state.astype(BF16)) * jnp.exp(pos * lg)
        kd = (kh.astype(F32) * jnp.exp((t - pos) * lg)).astype(BF16)
        state_out(hd, state * math.exp(t * lg) + lax.dot_general(kd, vh, _TN, preferred_element_type=F32))
        mu = jnp.mean(o, axis=-1, keepdims=True)
        dev = o - mu
        var = jnp.mean(dev * dev, axis=-1, keepdims=True)
        y_ref[:, vs] = (sg_ref[:, vs].astype(F32) * (dev * lax.rsqrt(var + GN_EPS))).astype(BF16)


def _ret_prompt_body(q_ref, k_ref, v_ref, sg_ref, y_ref, st_ref, decay_ref):
    t = TOKEN_TILE

    @pl.when((pl.program_id(0) == 0) & (pl.program_id(1) == 0))
    def _():
        for hd, lg in enumerate(_ret_log_gammas()):
            decay_ref[hd] = _ret_decay_matrix(t, lg)

    @pl.when(pl.program_id(1) == 0)
    def _():
        st_ref[...] = jnp.zeros_like(st_ref)

    def state_out(hd, val):
        st_ref[0, hd] = val

    _ret_tile(q_ref, k_ref, v_ref, sg_ref, y_ref, lambda hd: st_ref[0, hd], state_out,
              lambda hd: decay_ref[hd], t)


def _ret_prompt(q, k, v, sg, batch, seq):
    nt = seq // TOKEN_TILE
    tile = lambda width: pl.BlockSpec((TOKEN_TILE, width), lambda b, i: (b * nt + i, 0))
    return pl.pallas_call(
        _ret_prompt_body,
        out_shape=(jax.ShapeDtypeStruct((q.shape[0], RET_V_WIDTH), BF16),
                   jax.ShapeDtypeStruct((batch, RET_HEADS, RET_QK_DIM, RET_V_DIM), F32)),
        grid=(batch, nt),
        in_specs=[tile(RET_QK_WIDTH), tile(RET_QK_WIDTH), tile(RET_V_WIDTH), tile(RET_V_WIDTH)],
        out_specs=(tile(RET_V_WIDTH),
                   pl.BlockSpec((1, RET_HEADS, RET_QK_DIM, RET_V_DIM), lambda b, i: (b, 0, 0, 0))),
        scratch_shapes=[pltpu.VMEM((RET_HEADS, TOKEN_TILE, TOKEN_TILE), F32)],
        compiler_params=_params("arbitrary", "arbitrary"),
        name="ret_prompt",
    )(q, k, v, sg)


def _ret_sample_body(q_ref, k_ref, v_ref, sg_ref, st_in_ref, y_any, y_ref, st_out_ref):
    del y_any
    t = q_ref.shape[0]
    lgs = _ret_log_gammas()

    def state_out(hd, val):
        st_out_ref[0, hd] = val

    _ret_tile(q_ref, k_ref, v_ref, sg_ref, y_ref, lambda hd: st_in_ref[0, hd], state_out,
              lambda hd: _ret_decay_matrix(t, lgs[hd]), t)


def _ret_sample(q, k, v, sg, state, y_all, n_new, row0):
    dec_batch = state.shape[0]
    blk0 = row0 // n_new
    tile = lambda width: pl.BlockSpec((n_new, width), lambda b: (blk0 + b, 0))
    st = pl.BlockSpec((1, RET_HEADS, RET_QK_DIM, RET_V_DIM), lambda b: (b, 0, 0, 0))
    return pl.pallas_call(
        _ret_sample_body,
        out_shape=(jax.ShapeDtypeStruct(y_all.shape, BF16), jax.ShapeDtypeStruct(state.shape, F32)),
        grid=(dec_batch,),
        in_specs=[tile(RET_QK_WIDTH), tile(RET_QK_WIDTH), tile(RET_V_WIDTH), tile(RET_V_WIDTH), st,
                  pl.BlockSpec(memory_space=pl.ANY)],
        out_specs=(tile(RET_V_WIDTH), st),
        input_output_aliases={5: 0},
        compiler_params=_params("parallel"),
        name="ret_sample",
    )(q, k, v, sg, state, y_all)


def _rotary_tables(pos):
    half = RET_QK_DIM // 2
    inv_freq = ROPE_BASE ** (-jnp.linspace(0.0, 1.0, half, dtype=F32))
    ang = pos[:, None] * inv_freq[None, :]
    cos, sin = jnp.cos(ang), jnp.sin(ang)
    return jnp.concatenate([cos, cos], axis=1), jnp.concatenate([-sin, sin], axis=1)


def _head_tiled(gain):
    return jnp.tile(gain.astype(F32), FOX_HEADS)


def kernel(x_prompt, x_sample, cache_fox_k, cache_fox_v, cache_fox_logf, state_ret, norm_gains, w_ffn_in,
           w_ffn_out, fox_w_in, fox_b_f, fox_q_gain, fox_k_gain, fox_w_out, ret_w_in, ret_w_out, final_gain):
    batch, seq, _ = x_prompt.shape
    dec_batch, n_new, _ = x_sample.shape
    past = cache_fox_k.shape[2]
    depth = norm_gains.shape[0]
    n_prompt = batch * seq
    n_sample = dec_batch * n_new
    assert seq % TOKEN_TILE == 0 and n_sample == TOKEN_TILE and TOKEN_TILE % n_new == 0
    assert seq % ATTN_Q == 0 and ATTN_Q % ATTN_K == 0 and past % MXU_DIM == 0
    assert TOKEN_TILE % CHUNK == 0 and n_new <= CHUNK
    n_fox = (depth + 1) // 2

    xs = (x_prompt.reshape(n_prompt, D_MODEL), x_sample.reshape(n_sample, D_MODEL))
    pos =jnp.concatenate([jnp.tile(jnp.arange(seq, dtype=F32), batch),
                           jnp.tile(past + jnp.arange(n_new, dtype=F32), dec_batch)])
    cos, sin = _rotary_tables(pos)
    grp_idx = np.arange(MXU_DIM) // FOX_HEAD_DIM
    grp = jnp.asarray(grp_idx[:, None] == grp_idx[None, :], dtype=BF16)
    sel_k, sel_q, const_k, const_q = _aug_tables()
    selk, ck = jnp.asarray(sel_k, BF16), jnp.asarray(const_k, F32)
    selqt, cqt = jnp.asarray(sel_q.T, BF16), jnp.asarray(const_q.T, F32)
    fgain = final_gain.reshape(1, D_MODEL)

    feature_major = lambda c: c.transpose(0, 1, 3, 4, 2).reshape(n_fox * dec_batch, D_MODEL, past)
    cache_k_t, cache_v_t = feature_major(cache_fox_k), feature_major(cache_fox_v)
    cache_lf_t = cache_fox_logf.transpose(0, 1, 3, 2).reshape(n_fox * dec_batch, FOX_HEADS, past)
    w_in_bf, w_out_bf = w_ffn_in.astype(BF16), w_ffn_out.astype(BF16)
    rs_p, fl_s, rs_s = [], [], []
    fox_stacked = ()
    for layer in range(depth):
        g = norm_gains[layer].reshape(3, 1, D_MODEL)
        x = _ffn(xs, g[0], w_in_bf, w_out_bf, (layer, 0), n_prompt, n_sample)
        j = layer // 2
        if layer % 2 == 0:
            w = fox_w_in[j]
            wf = jnp.pad(w[:, 4 * D_MODEL:], ((0, 0), (0, LANES - FOX_HEADS)))
            bf = jnp.pad(fox_b_f[j], (0, LANES - FOX_HEADS))
            wt = jnp.concatenate([w[:, 0:D_MODEL], w[:, 2 * D_MODEL:3 * D_MODEL]], axis=1).T
            qg, kg = _head_tiled(fox_q_gain[j]), _head_tiled(fox_k_gain[j])
            token_major = (x, g[1], w[:, :4 * D_MODEL].astype(BF16), wf.astype(BF16), bf.reshape(1, LANES))
            kg2, qg2 = kg.reshape(1, D_MODEL), qg.reshape(1, D_MODEL)
            bound = 1.01 * (FOX_HEAD_DIM ** 0.5) * jnp.max(jnp.abs(fox_q_gain[j])) * jnp.max(jnp.abs(fox_k_gain[j]))
            shift = jnp.full((LANES, 1), bound * LOG2E, F32)
            proj_ins = token_major + (
                kg2, grp, selk, ck, wt.astype(BF16), wf.T.astype(BF16), bf.reshape(LANES, 1),
                (qg * (FOX_Q_SCALE * LOG2E)).reshape(D_MODEL, 1), selqt, cqt, shift)
            qt, aqt, vt, kb, ak, gate, _, d, *stk_p = _fox_proj(
                proj_ins, fox_stacked[:3], j, n_fox, n_prompt, seq // TOKEN_TILE)
            qs, ks, vs, gate_s, lf_s, *stk_s = _fox_proj_sample(
                token_major + (qg2, kg2, grp), fox_stacked[3:], j, n_fox, n_prompt // TOKEN_TILE)
            fox_stacked = (*stk_p, *stk_s)
            dh = d[:, :FOX_HEADS].reshape(batch, seq, FOX_HEADS)
            dfirst = dh[:, 0::ATTN_Q, :].transpose(0, 2, 1).reshape(-1)
            dlast = dh[:, ATTN_K - 1::ATTN_K, :].transpose(0, 2, 1).reshape(-1)
            thr = (2.0 * bound + SKIP_MARGIN).reshape(1).astype(F32)
            frozen = (2.0 * LOG2E * bound < FROZEN_MAX_RANGE).reshape(1).astype(jnp.int32)
            y = _fox_attn_prompt(dfirst, dlast, thr, frozen, qt, aqt, kb, ak, vt, gate, batch, seq,
                                 n_prompt + n_sample)
            rt, nc, nt = _sample_bias(cache_lf_t, lf_s, n_new, 0, dec_batch, j)
            y = _fox_attn_sample(
                qs, vs, ks, gate_s, cache_k_t, cache_v_t, rt.reshape(dec_batch, FOX_PAIRS, 2, past),
                nc[:, :, :FOX_HEADS].reshape(dec_batch, n_new, FOX_PAIRS, 2).transpose(0, 2, 1, 3),
                nt.reshape(dec_batch, FOX_PAIRS, 2, n_new), y, n_new, n_prompt, j)
            mix = (y, fox_w_out[j].astype(BF16))
            fl_s.append(lf_s[:, :FOX_HEADS].reshape(dec_batch, n_new, FOX_HEADS))
        else:
            q, k, v, sg = _ret_proj(x, g[1], ret_w_in[j].astype(BF16), cos, sin)
            y, st_p = _ret_prompt(q, k, v, sg, batch, seq)
            y, st_s = _ret_sample(q, k, v, sg, state_ret[j], y, n_new, n_prompt)
            mix = (y, ret_w_out[j].astype(BF16))
            rs_p.append(st_p)
            rs_s.append(st_s)
        x = _ffn((x,), g[2], w_in_bf, w_out_bf, (layer, 1), n_prompt, n_sample,
                 mix=mix, final_gain=fgain if layer == depth - 1 else None)
        xs = (x,)
    y_prompt, y_sample = x
    kft_p, vft_p, lft_p, kf_s, vf_s = fox_stacked
    per_head = lambda a: a.reshape(n_fox, batch, FOX_HEADS, FOX_HEAD_DIM, seq).transpose(0, 1, 4, 2, 3)
    heads_s = (n_fox, dec_batch, n_new, FOX_HEADS, FOX_HEAD_DIM)
    return (y_prompt.reshape(batch, seq, D_MODEL), y_sample.reshape(dec_batch, n_new, D_MODEL),
            per_head(kft_p), per_head(vft_p), lft_p.transpose(0, 1, 3, 2), jnp.stack(rs_p),
            kf_s.reshape(heads_s), vf_s.reshape(heads_s), jnp.stack(fl_s), jnp.stack(rs_s))
```

```python
import functools
import math

import numpy as np
import jax
import jax.numpy as jnp
from jax import lax
from jax.experimental import pallas as pl
from jax.experimental.pallas import tpu as pltpu

F32 = jnp.float32
BF16 = jnp.bfloat16

D_MODEL = 1024
CHUNK = 64
FOX_HEADS = 16
FOX_HEAD_DIM = 64
FOX_PAIRS = FOX_HEADS // 2
RET_HEADS = 8
RET_QK_DIM = 128
RET_V_DIM = 256
RET_QK_WIDTH = RET_HEADS * RET_QK_DIM
RET_V_WIDTH = RET_HEADS * RET_V_DIM
D_FF = 2816
RMS_EPS = 1e-6
GN_EPS = 1e-5
ROPE_BASE = 10000.0

LANES = 128
MXU_DIM = 256
TOKEN_TILE = 512
FFN_CHUNK = MXU_DIM
ATTN_Q = 512
ATTN_K = 256
ATTN_GROUP = 4
OWN_EXTRA = 2
VMEM_LIMIT = 56 * 1024 * 1024
NEG_BIG = -1e30
LOG2E = 1.4426950408889634
FOX_Q_SCALE = FOX_HEAD_DIM ** -0.5
AUG_PIECES = 3
AUG_STRIDE = 16
SKIP_MARGIN = 32.0
FROZEN_MAX_RANGE = 100.0

_NT = (((1,), (1,)), ((), ()))
_TN = (((0,), (0,)), ((), ()))


def _params(*sem):
    return pltpu.CompilerParams(dimension_semantics=sem, vmem_limit_bytes=VMEM_LIMIT)


def _resident(shape):
    nd = len(shape)
    return pl.BlockSpec(shape, lambda *_: (0,) * nd, pipeline_mode=pl.Buffered(1))


def _rmsnorm(x, gain):
    ms = jnp.mean(x * x, axis=-1, keepdims=True)
    return x * lax.rsqrt(ms + RMS_EPS) * gain


def _dot(a, b):
    return jnp.dot(a, b, preferred_element_type=F32)


def _split3(x):
    hi = x.astype(BF16)
    r = x - hi.astype(F32)
    mid = r.astype(BF16)
    lo = (r - mid.astype(F32)).astype(BF16)
    return hi, mid, lo


def _tri(n, upper=False):
    row = lax.broadcasted_iota(jnp.int32, (n, n), 0)
    col = lax.broadcasted_iota(jnp.int32, (n, n), 1)
    return jnp.where((row <= col) if upper else (row >= col), 1.0, 0.0).astype(BF16)


def _cumsum_rows(tri, x):
    hi, mid, lo = _split3(x)
    return _dot(tri, hi) + _dot(tri, mid) + _dot(tri, lo)


def _cumsum_cols(tri_upper, x):
    hi, mid, lo = _split3(x)
    return _dot(hi, tri_upper) + _dot(mid, tri_upper) + _dot(lo, tri_upper)


def _log_sigmoid(z):
    return jnp.minimum(z, 0.0) - jnp.log1p(jnp.exp(-jnp.abs(z)))


def _ffn_body(*refs, first, mixed, final, n_prompt_tiles):
    refs = list(refs)
    x_refs = [refs.pop(0) for _ in range(2 if first else 1)]
    y_ref, wmix_ref = (refs.pop(0), refs.pop(0)) if mixed else (None, None)
    g_ref, win_ref, wout_ref = refs.pop(0), refs.pop(0), refs.pop(0)
    fg_ref = refs.pop(0) if final else None
    o_refs = [refs.pop(0) for _ in range(2 if final else 1)]
    (acc_ref,) = refs
    is_prompt = pl.program_id(0) < n_prompt_tiles
    x = jnp.where(is_prompt, x_refs[0][...], x_refs[1][...]) if first else x_refs[0][...]
    if mixed:
        x = x + _dot(y_ref[...], wmix_ref[...])
    h = _rmsnorm(x, g_ref[...]).astype(BF16)
    for c in range(D_FF // FFN_CHUNK):
        lo, hi = c * FFN_CHUNK, (c + 1) * FFN_CHUNK
        gate = _dot(h, win_ref[0, 0, :, lo:hi])
        up = _dot(h, win_ref[0, 0, :, D_FF + lo:D_FF + hi])
        act = (gate * jax.nn.sigmoid(gate) * up).astype(BF16)
        part = _dot(act, wout_ref[0, 0, lo:hi, :])
        if c == 0:
            acc_ref[...] = part
        else:
            acc_ref[...] += part
    y = x + 0.5 * acc_ref[...]
    if not final:
        o_refs[0][...] = y
        return
    y = _rmsnorm(y, fg_ref[...])

    @pl.when(is_prompt)
    def _():
        o_refs[0][...] = y

    @pl.when(jnp.logical_not(is_prompt))
    def _():
        o_refs[1][...] = y


def _ffn(xs, gain, w_in, w_out, which, n_prompt, n_sample, mix=None, final_gain=None):
    first, mixed, final = len(xs) == 2, mix is not None, final_gain is not None
    n = n_prompt + n_sample
    n_prompt_tiles = n_prompt // TOKEN_TILE
    tile = pl.BlockSpec((TOKEN_TILE, D_MODEL), lambda i: (i, 0))
    prompt_tile = pl.BlockSpec((TOKEN_TILE, D_MODEL), lambda i: (jnp.minimum(i, n_prompt_tiles - 1), 0))
    sample_tile = pl.BlockSpec((TOKEN_TILE, D_MODEL), lambda i: (jnp.maximum(i - n_prompt_tiles, 0), 0))
    ins, in_specs = list(xs), ([prompt_tile, sample_tile] if first else [tile])
    if mixed:
        y, w_mix = mix
        ins += [y, w_mix]
        in_specs += [pl.BlockSpec((TOKEN_TILE, y.shape[1]), lambda i: (i, 0)), _resident(w_mix.shape)]
    ins += [gain, w_in, w_out]
    pick = lambda *_: (*which, 0, 0)
    in_specs += [_resident((1, D_MODEL)),
                 pl.BlockSpec((1, 1, D_MODEL, 2 * D_FF), pick, pipeline_mode=pl.Buffered(1)),
                 pl.BlockSpec((1, 1, D_FF, D_MODEL), pick, pipeline_mode=pl.Buffered(1))]
    if final:
        ins.append(final_gain)
        in_specs.append(_resident((1, D_MODEL)))
        out_shape = (jax.ShapeDtypeStruct((n_prompt, D_MODEL), F32), jax.ShapeDtypeStruct((n_sample, D_MODEL), F32))
        out_specs = (prompt_tile, sample_tile)
    else:
        out_shape, out_specs = jax.ShapeDtypeStruct((n, D_MODEL), F32), tile
    return pl.pallas_call(
        functools.partial(_ffn_body, first=first, mixed=mixed, final=final, n_prompt_tiles=n_prompt_tiles),
        out_shape=out_shape,
        grid=(n // TOKEN_TILE,),
        in_specs=in_specs,
        out_specs=out_specs,
        scratch_shapes=[pltpu.VMEM((TOKEN_TILE, D_MODEL), F32)],
        compiler_params=_params("arbitrary"),
        name="ffn_final" if final else ("ffn_mixed" if mixed else "ffn"),
    )(*ins)


def _aug_tables():
    assert FOX_PAIRS * AUG_STRIDE == LANES
    sel_k = np.zeros((AUG_PIECES * LANES, LANES), np.float32)
    sel_q = np.zeros((AUG_PIECES * LANES, LANES), np.float32)
    const_k = np.zeros((1, LANES), np.float32)
    const_q = np.zeros((1, LANES), np.float32)
    for p in range(FOX_PAIRS):
        for hh in range(2):
            for piece in range(AUG_PIECES):
                src = piece * LANES + 2 * p + hh
                sel_k[src, p * AUG_STRIDE + 3 * hh + piece] = 1.0
                sel_q[src, p * AUG_STRIDE + 6 + 3 * hh + piece] = 1.0
                const_q[0, p * AUG_STRIDE + 3 * hh + piece] = -1.0
                const_k[0, p * AUG_STRIDE + 6 + 3 * hh + piece] = 1.0
    return sel_k, sel_q, const_k, const_q


def _head_rmsnorm(t, grp, gain):
    outs = []
    for c in range(D_MODEL // MXU_DIM):
        blk = t[:, c * MXU_DIM:(c + 1) * MXU_DIM]
        sq = blk * blk
        hi = sq.astype(BF16)
        lo = (sq - hi.astype(F32)).astype(BF16)
        ss = _dot(hi, grp) + _dot(lo, grp)
        outs.append(blk * lax.rsqrt(ss * (1.0 / FOX_HEAD_DIM) + RMS_EPS))
    return jnp.concatenate(outs, axis=1) * gain


def _head_rmsnorm_t(t, grp, gain):
    outs = []
    for c in range(D_MODEL // MXU_DIM):
        blk = t[c * MXU_DIM:(c + 1) * MXU_DIM, :]
        sq = blk * blk
        hi = sq.astype(BF16)
        lo = (sq - hi.astype(F32)).astype(BF16)
        ss = _dot(grp, hi) + _dot(grp, lo)
        outs.append(blk * lax.rsqrt(ss * (1.0 / FOX_HEAD_DIM) + RMS_EPS))
    return jnp.concatenate(outs, axis=0) * gain


N_FOX_PROJ_INPUTS = 14
N_FOX_SAMPLE_INPUTS = 8


def _fox_token_major(h, w_ref, wf_ref, bf_ref, kg_ref, grp, kb_o, gate_o, lf_o):
    w = D_MODEL
    k = _head_rmsnorm(_dot(h, w_ref[:, w:2 * w]), grp, kg_ref[...])
    kb_o[...] = k.astype(BF16)
    v = _dot(h, w_ref[:, 2 * w:3 * w])
    gate_o[...] = jax.nn.sigmoid(_dot(h, w_ref[:, 3 * w:4 * w])).astype(BF16)
    logf = _log_sigmoid(_dot(h, wf_ref[...]) + bf_ref[...])
    lf_o[...] = logf
    return k, v, logf


def _fox_proj_sample_body(*refs, n_stacked):
    x_ref, g_ref, w_ref, wf_ref, bf_ref, qg_ref, kg_ref, grp_ref = refs[:N_FOX_SAMPLE_INPUTS]
    qs_o, kb_o, vs_o, gate_o, lf_o, kf_o, vf_o = refs[N_FOX_SAMPLE_INPUTS + n_stacked:]
    h = _rmsnorm(x_ref[...], g_ref[...]).astype(BF16)
    grp = grp_ref[...]
    k, v, _ = _fox_token_major(h, w_ref, wf_ref, bf_ref, kg_ref, grp, kb_o, gate_o, lf_o)
    for hd in range(FOX_HEADS):
        cols = slice(hd * FOX_HEAD_DIM, (hd + 1) * FOX_HEAD_DIM)
        rows = pl.ds(hd, TOKEN_TILE, stride=FOX_HEADS)
        kf_o.at[0][rows, :] = k[:, cols]
        vf_o.at[0][rows, :] = v[:, cols]
    vs_o[...] = v.astype(BF16)
    q = _head_rmsnorm(_dot(h, w_ref[:, 0:D_MODEL]), grp, qg_ref[...])
    qs_o[...] = (q * FOX_Q_SCALE).astype(BF16)


def _fox_proj_sample(ins, stacked, layer, n_layers, tile0):
    assert len(ins) == N_FOX_SAMPLE_INPUTS
    n = ins[0].shape[0] - tile0 * TOKEN_TILE
    tile = pl.BlockSpec((TOKEN_TILE, D_MODEL), lambda i: (i, 0))
    slab = pl.BlockSpec((1, TOKEN_TILE * FOX_HEADS, FOX_HEAD_DIM), lambda i: (layer, i, 0))
    wide = jax.ShapeDtypeStruct((n, D_MODEL), BF16)
    stack = jax.ShapeDtypeStruct((n_layers, n * FOX_HEADS, FOX_HEAD_DIM), F32)
    n_out_before_stacks = 5
    return pl.pallas_call(
        functools.partial(_fox_proj_sample_body, n_stacked=len(stacked)),
        out_shape=(wide, wide, wide, wide, jax.ShapeDtypeStruct((n, LANES), F32), stack, stack),
        grid=(n // TOKEN_TILE,),
        in_specs=([pl.BlockSpec((TOKEN_TILE, D_MODEL), lambda i: (tile0 + i, 0))]
                  + [_resident(a.shape) for a in ins[1:]] + [pl.BlockSpec(memory_space=pl.ANY)] * len(stacked)),
        out_specs=(tile, tile, tile, tile, pl.BlockSpec((TOKEN_TILE, LANES), lambda i: (i, 0)), slab, slab),
        input_output_aliases={N_FOX_SAMPLE_INPUTS + s: n_out_before_stacks + s for s in range(len(stacked))},
        compiler_params=_params("parallel"),
        name="fox_proj_sample",
    )(*ins, *stacked)


def _fox_proj_body(*refs, tiles_per_seq, n_stacked):
    (x_ref, g_ref, w_ref, kg_ref, grp_ref, selk_ref, ck_ref,
     wt_ref, wft_ref, bft_ref, qgt_ref, selqt_ref, cqt_ref, shift_ref) = refs[:N_FOX_PROJ_INPUTS]
    (qt_o, aqt_o, vt_o, kb_o, ak_o, gate_o, dt_o, kft_o, vft_o, lft_o,
     carryt_ref) = refs[N_FOX_PROJ_INPUTS + n_stacked:]
    i = pl.program_id(0)
    tm, w = TOKEN_TILE, D_MODEL
    hf = _rmsnorm(x_ref[...], g_ref[...])
    h = hf.astype(BF16)
    ht = hf.T.astype(BF16)
    grp = grp_ref[...]

    @pl.when(i % tiles_per_seq == 0)
    def _():
        carryt_ref[...] = jnp.zeros_like(carryt_ref)

    k = _head_rmsnorm(_dot(h, w_ref[:, w:2 * w]), grp, kg_ref[...])
    kb_o[...] = k.astype(BF16)
    kft_o[0, 0] = k.T
    gate_o[...] = jax.nn.sigmoid(_dot(h, w_ref[:, 3 * w:4 * w])).astype(BF16)

    qt = _head_rmsnorm_t(_dot(wt_ref[0:w, :], ht), grp, qgt_ref[...])
    qt_o[...] = qt.astype(BF16)
    vt = _dot(wt_ref[w:2 * w, :], ht)
    vft_o[0, 0] = vt
    vt_o[...] = vt.astype(BF16)
    logft = _log_sigmoid(_dot(wft_ref[...], ht) + bft_ref[...])
    lft_o[0, 0] = logft[0:FOX_HEADS, :]
    dt = _cumsum_cols(_tri(tm, upper=True), logft) + carryt_ref[:, 0:1]
    carryt_ref[...] = jnp.broadcast_to(dt[:, tm - 1:tm], carryt_ref.shape)
    dt_o[...] = dt[0:FOX_HEADS, :]

    piecest = jnp.concatenate(_split3(dt * LOG2E - shift_ref[...]), axis=0)
    aqt_o[...] = (_dot(selqt_ref[...], piecest) + cqt_ref[...]).astype(BF16)
    pieces = jnp.concatenate(_split3(dt.T * LOG2E), axis=1)
    ak_o[...] = (_dot(pieces, selk_ref[...]) + ck_ref[...]).astype(BF16)


def _fox_proj(ins, stacked, layer, n_layers, n, tiles_per_seq):
    assert len(ins) == N_FOX_PROJ_INPUTS
    tps = tiles_per_seq
    batch, seq = n // (tps * TOKEN_TILE), tps * TOKEN_TILE
    tile = pl.BlockSpec((TOKEN_TILE, D_MODEL), lambda i: (i, 0))
    tile_t = pl.BlockSpec((D_MODEL, TOKEN_TILE), lambda i: (0, i))
    nar = pl.BlockSpec((TOKEN_TILE, LANES), lambda i: (i, 0))
    slab = pl.BlockSpec((1, 1, D_MODEL, TOKEN_TILE), lambda i: (layer, i // tps, 0, i % tps))
    slab_lf = pl.BlockSpec((1, 1, FOX_HEADS, TOKEN_TILE), lambda i: (layer, i // tps, 0, i % tps))
    wide = jax.ShapeDtypeStruct((n, D_MODEL), BF16)
    wide_t = jax.ShapeDtypeStruct((D_MODEL, n), BF16)
    stack = jax.ShapeDtypeStruct((n_layers, batch, D_MODEL, seq), F32)
    stack_lf = jax.ShapeDtypeStruct((n_layers, batch, FOX_HEADS, seq), F32)
    n_out_before_stacks = 7
    slots_t = pl.BlockSpec((LANES, TOKEN_TILE), lambda i: (0, i))
    heads_t = pl.BlockSpec((FOX_HEADS, TOKEN_TILE), lambda i: (0, i))
    return pl.pallas_call(
        functools.partial(_fox_proj_body, tiles_per_seq=tiles_per_seq, n_stacked=len(stacked)),
        out_shape=(wide_t, jax.ShapeDtypeStruct((LANES, n), BF16), wide_t, wide,
                   jax.ShapeDtypeStruct((n, LANES), BF16), wide, jax.ShapeDtypeStruct((FOX_HEADS, n), F32),
                   stack, stack, stack_lf),
        grid=(n // TOKEN_TILE,),
        in_specs=([tile] + [_resident(a.shape) for a in ins[1:]]
                  + [pl.BlockSpec(memory_space=pl.ANY)] * len(stacked)),
        out_specs=(tile_t, slots_t, tile_t, tile, nar, tile, heads_t, slab, slab, slab_lf),
        input_output_aliases={N_FOX_PROJ_INPUTS + s: n_out_before_stacks + s for s in range(len(stacked))},
        scratch_shapes=[pltpu.VMEM((LANES, LANES), F32)],
        compiler_params=_params("arbitrary"),
        name="fox_proj",
    )(*ins, *stacked)


def _fox_attn_body(dfirst_ref, dlast_ref, thr_ref, frozen_ref, qt_ref, aqt_ref, k_ref, ak_ref, vt_ref, gate_ref,
                   o_ref, qh_ref, m_ref, acc_ref, *, nq, nkb):
    tq, tk = ATTN_Q, ATTN_K
    half = FOX_HEAD_DIM
    per_q = tq // tk
    b, p, i = pl.program_id(0), pl.program_id(1), pl.program_id(2)

    qt = jnp.concatenate([qt_ref[...], aqt_ref[...]], axis=0)
    sub = lax.broadcasted_iota(jnp.int32, qt.shape, 0)
    slot = sub - LANES - p * AUG_STRIDE
    keep_a = (sub < half) | ((slot >= 0) & (slot < 3)) | ((slot >= 6) & (slot < 9))
    zero = jnp.zeros_like(qt)
    qh_ref[0] = jnp.where(keep_a, qt, zero)
    keep_b = ((sub >= half) & (sub < LANES)) | ((slot >= 3) & (slot < 6)) | ((slot >= 9) & (slot < 12))
    qh_ref[1] = jnp.where(keep_b, qt, zero)

    def step(j0, nblocks, heads, online, own=False):
        rows = nblocks * tk
        first_own_row = rows - tq
        off = pl.multiple_of(j0 * tk, tk)
        kk = jnp.concatenate([k_ref[pl.ds(off, rows), :], ak_ref[pl.ds(off, rows), :]], axis=1)
        vt = vt_ref[:, pl.ds(off, rows)]
        top = lax.broadcasted_iota(jnp.int32, (LANES, rows), 0) < half
        ones = jnp.ones((LANES, rows), BF16)
        for hh in heads:
            vsel = jnp.where(top, vt, ones) if hh == 0 else jnp.where(top, ones, vt)
            s = _dot(kk, qh_ref[hh])
            if own:
                kpos = lax.broadcasted_iota(jnp.int32, (rows, tq), 0) - first_own_row
                qpos = lax.broadcasted_iota(jnp.int32, (rows, tq), 1)
                s = jnp.where(kpos <= qpos, s, NEG_BIG)
            if not online:
                pv = _dot(vsel, jnp.exp2(s).astype(BF16))
                acc_ref[hh] = pv if own else acc_ref[hh] + pv
                continue
            if own:
                m_new = jnp.max(s, axis=0, keepdims=True)
                acc_ref[hh] = _dot(vsel, jnp.exp2(s - m_new).astype(BF16))
                m_ref[hh] = jnp.broadcast_to(m_new, m_ref.shape[1:])
                continue
            m_old = m_ref[hh, 0:1, :]
            m_new = jnp.maximum(m_old, jnp.max(s, axis=0, keepdims=True))
            alpha = jnp.exp2(m_old - m_new)
            pr = jnp.exp2(s - m_new).astype(BF16)
            acc_ref[hh] = alpha * acc_ref[hh] + _dot(vsel, pr)
            m_ref[hh] = jnp.broadcast_to(m_new, m_ref.shape[1:])

    thr = thr_ref[0]
    frozen_ok = frozen_ref[0] != 0
    j_top = i * per_q - 1
    group = ATTN_GROUP

    def live_blocks(head):
        first = dfirst_ref[(b * FOX_HEADS + head) * nq + i]
        base = (b * FOX_HEADS + head) * nkb

        def near(n):
            j = jnp.maximum(j_top - n, 0)
            return (n <= j_top) & (first - dlast_ref[base + j] >= -thr)

        return lax.while_loop(near, lambda n: n + 1, jnp.int32(0))

    n_live = (live_blocks(2 * p), live_blocks(2 * p + 1))

    def all_blocks(online):
        @pl.when(i == 0)
        def _():
            step(0, per_q, (0, 1), online, own=True)

        @pl.when(i > 0)
        def _():
            step(i * per_q - OWN_EXTRA, per_q + OWN_EXTRA, (0, 1), online, own=True)

        taken = jnp.where(i > 0, OWN_EXTRA, 0)
        j_top = i * per_q - 1 - taken
        n_left = tuple(jnp.maximum(n - taken, 0) for n in n_live)
        n_groups = (n_left[0] // group, n_left[1] // group)
        n_joint = jnp.minimum(n_groups[0], n_groups[1])

        def run_groups(lo, hi, heads):
            def body(g, carry):
                step(j_top - group * (g + 1) + 1, group, heads, online)
                return carry
            lax.fori_loop(lo, hi, body, 0)

        run_groups(0, n_joint, (0, 1))
        for hh in range(2):
            run_groups(n_joint, n_groups[hh], (hh,))
            done = n_groups[hh] * group
            size = group // 2
            while size >= 1:
                take = ((n_left[hh] - done) & size) != 0

                @pl.when(take)
                def _(done=done, size=size, hh=hh):
                    step(j_top - done - size + 1, size, (hh,), online)

                done = done + jnp.where(take, size, 0)
                size //= 2

    @pl.when(frozen_ok)
    def _():
        all_blocks(online=False)

    @pl.when(jnp.logical_not(frozen_ok))
    def _():
        all_blocks(online=True)

    acca, accb = acc_ref[0], acc_ref[1]
    ot = jnp.concatenate([acca[0:half, :] / acca[half:half + 1, :],
                          accb[half:LANES, :] / accb[0:1, :]], axis=0)
    o_ref[...] = (ot.T * gate_ref[...].astype(F32)).astype(BF16)


def _fox_attn_prompt(dfirst, dlast, thr, frozen, qt, aqt, kb, ak, vt, gate, batch, seq, n):
    nq = seq // ATTN_Q
    qspec_t = pl.BlockSpec((LANES, ATTN_Q), lambda b, p, i, *_: (p, b * nq + i))
    tok = pl.BlockSpec((ATTN_Q, LANES), lambda b, p, i, *_: (b * nq + i, p))
    kspec = pl.BlockSpec((seq, LANES), lambda b, p, i, *_: (b, p))
    vspec_t = pl.BlockSpec((LANES, seq), lambda b, p, i, *_: (p, b))
    return pl.pallas_call(
        functools.partial(_fox_attn_body, nq=nq, nkb=seq // ATTN_K),
        out_shape=jax.ShapeDtypeStruct((n, D_MODEL), BF16),
        grid_spec=pltpu.PrefetchScalarGridSpec(
            num_scalar_prefetch=4,
            grid=(batch, FOX_PAIRS, nq),
            in_specs=[qspec_t, pl.BlockSpec((LANES, ATTN_Q), lambda b, p, i, *_: (0, b * nq + i)), kspec,
                      pl.BlockSpec((seq, LANES), lambda b, p, i, *_: (b, 0)), vspec_t, tok],
            out_specs=tok,
            scratch_shapes=[pltpu.VMEM((2, 2 * LANES, ATTN_Q), BF16), pltpu.VMEM((2, 8, ATTN_Q), F32),
                            pltpu.VMEM((2, LANES, ATTN_Q), F32)]),
        compiler_params=_params("parallel", "parallel", "arbitrary"),
        name="fox_attn_prompt",
    )(dfirst, dlast, thr, frozen, qt, aqt, kb, ak, vt, gate)


def _pair_split(q2):
    lane = lax.broadcasted_iota(jnp.int32, q2.shape, 1)
    left = lane < FOX_HEAD_DIM
    zero = jnp.zeros_like(q2)
    return left, (jnp.where(left, q2, zero), jnp.where(left, zero, q2))


def _sample_bias_body(cl_ref, ln_ref, rt_o, ncol_o, nt_o):
    past = cl_ref.shape[2]
    n_new = ln_ref.shape[0]
    tri = _tri(MXU_DIM, upper=True)
    carry = jnp.zeros((FOX_HEADS, 1), F32)
    prefix = []
    for b in range(past // MXU_DIM):
        p = _cumsum_cols(tri, cl_ref[0, :, b * MXU_DIM:(b + 1) * MXU_DIM]) + carry
        carry = p[:, MXU_DIM - 1:MXU_DIM]
        prefix.append(p)
    for b, p in enumerate(prefix):
        rt_o[0, :, b * MXU_DIM:(b + 1) * MXU_DIM] = carry - p
    ln =jnp.concatenate([ln_ref[...], jnp.zeros((LANES - n_new, LANES), F32)], axis=0)
    nn = _cumsum_rows(_tri(LANES), ln)
    ncol_o[0] = nn[0:n_new, :]
    nt_o[0] = nn.T[0:FOX_HEADS, 0:n_new]


def _sample_bias(cl_t, lf, n_new, row0, dec_batch, layer):
    past = cl_t.shape[2]
    blk0 = row0 // n_new
    return pl.pallas_call(
        _sample_bias_body,
        out_shape=(jax.ShapeDtypeStruct((dec_batch, FOX_HEADS, past), F32),
                   jax.ShapeDtypeStruct((dec_batch, n_new, LANES), F32),
                   jax.ShapeDtypeStruct((dec_batch, FOX_HEADS, n_new), F32)),
        grid=(dec_batch,),
        in_specs=[pl.BlockSpec((1, FOX_HEADS, past), lambda b: (layer * dec_batch + b, 0, 0)),
                  pl.BlockSpec((n_new, LANES), lambda b: (blk0 + b, 0))],
        out_specs=(pl.BlockSpec((1, FOX_HEADS, past), lambda b: (b, 0, 0)),
                   pl.BlockSpec((1, n_new, LANES), lambda b: (b, 0, 0)),
                   pl.BlockSpec((1, FOX_HEADS, n_new), lambda b: (b, 0, 0))),
        compiler_params=_params("parallel"),
        name="fox_sample_bias",
    )(cl_t, lf)


def _fox_sample_body(q_ref, vn_ref, kn_ref, gate_ref, kc_ref, vc_ref, rt_ref, nc_ref, nt_ref,
                     y_any, o_ref):
    del y_any
    n_new = q_ref.shape[0]
    left, qs = _pair_split(q_ref[...])
    kc = kc_ref[0].astype(BF16)
    vc = vc_ref[0].astype(BF16)
    kn = kn_ref[...]
    vn = vn_ref[...]
    rt = rt_ref[0, 0]
    nc = nc_ref[0, 0]
    nt = nt_ref[0, 0]
    row = lax.broadcasted_iota(jnp.int32, (n_new, n_new), 0)
    col = lax.broadcasted_iota(jnp.int32, (n_new, n_new), 1)
    outs = []
    for hh in range(2):
        nq = nc[:, hh:hh + 1]
        s1 = _dot(qs[hh], kc) + (nq + rt[hh:hh + 1, :])
        s2 = lax.dot_general(qs[hh], kn, _NT, preferred_element_type=F32) + (nq - nt[hh:hh + 1, :])
        s2 = jnp.where(col <= row, s2, NEG_BIG)
        m = jnp.maximum(jnp.max(s1, axis=1, keepdims=True), jnp.max(s2, axis=1, keepdims=True))
        p1 = jnp.exp(s1 - m)
        p2 = jnp.exp(s2 - m)
        l = jnp.sum(p1, axis=1, keepdims=True) + jnp.sum(p2, axis=1, keepdims=True)
        pv1 = lax.dot_general(p1.astype(BF16), vc, _NT, preferred_element_type=F32)
        outs.append((pv1 + _dot(p2.astype(BF16), vn)) / l)
    o = jnp.where(left, outs[0], outs[1])
    o_ref[...] = (o * gate_ref[...].astype(F32)).astype(BF16)


def _fox_attn_sample(qs, vs, kb, gate, kc, vc, rt, nc, nt, y_all, n_new, row0, layer):
    dec_batch, _, _, past = rt.shape
    blk0 = row0 // n_new
    own = pl.BlockSpec((n_new, LANES), lambda b, p: (b, p))
    new = pl.BlockSpec((n_new, LANES), lambda b, p: (blk0 + b, p))
    cache = pl.BlockSpec((1, LANES, past), lambda b, p: (layer * dec_batch + b, p, 0))
    return pl.pallas_call(
        _fox_sample_body,
        out_shape=jax.ShapeDtypeStruct(y_all.shape, BF16),
        grid=(dec_batch, FOX_PAIRS),
        in_specs=[own, own, own, own, cache, cache,
                  pl.BlockSpec((1, 1, 2, past), lambda b, p: (b, p, 0, 0)),
                  pl.BlockSpec((1, 1, n_new, 2), lambda b, p: (b, p, 0, 0)),
                  pl.BlockSpec((1, 1, 2, n_new), lambda b, p: (b, p, 0, 0)),
                  pl.BlockSpec(memory_space=pl.ANY)],
        out_specs=new,
        input_output_aliases={9: 0},
        compiler_params=_params("parallel", "parallel"),
        name="fox_attn_sample",
    )(qs, vs, kb, gate, kc, vc, rt, nc, nt, y_all)


def _ret_log_gammas():
    return [float(v) for v in np.log1p(-np.exp2(-5.0 - np.arange(RET_HEADS, dtype=np.float32)))]


def _ret_proj_body(x_ref, g_ref, w_ref, cos_ref, sin_ref, q_o, k_o, v_o, sg_o):
    h = _rmsnorm(x_ref[...], g_ref[...]).astype(BF16)
    cos = cos_ref[...]
    sin = sin_ref[...]
    a, b2 = RET_QK_WIDTH, 2 * RET_QK_WIDTH

    def rotary(t, scale):
        outs = []
        for hd in range(RET_HEADS):
            th = t[:, hd * RET_QK_DIM:(hd + 1) * RET_QK_DIM]
            outs.append((th * cos + pltpu.roll(th, RET_QK_DIM // 2, 1) * sin) * scale)
        return jnp.concatenate(outs, axis=1).astype(BF16)

    q_o[...] = rotary(_dot(h, w_ref[:, 0:a]), 1.0)
    k_o[...] = rotary(_dot(h, w_ref[:, a:b2]), RET_QK_DIM ** -0.5)
    v_o[...] = _dot(h, w_ref[:, b2:b2 + RET_V_WIDTH]).astype(BF16)
    g = _dot(h, w_ref[:, b2 + RET_V_WIDTH:b2 + 2 * RET_V_WIDTH])
    sg_o[...] = (g * jax.nn.sigmoid(g)).astype(BF16)


def _ret_proj(x, gain, w, cos, sin):
    n = x.shape[0]
    tile = lambda width: pl.BlockSpec((TOKEN_TILE, width), lambda i: (i, 0))
    return pl.pallas_call(
        _ret_proj_body,
        out_shape=(jax.ShapeDtypeStruct((n, RET_QK_WIDTH), BF16), jax.ShapeDtypeStruct((n, RET_QK_WIDTH), BF16),
                   jax.ShapeDtypeStruct((n, RET_V_WIDTH), BF16), jax.ShapeDtypeStruct((n, RET_V_WIDTH), BF16)),
        grid=(n // TOKEN_TILE,),
        in_specs=[tile(D_MODEL), _resident((1, D_MODEL)), _resident(w.shape), tile(LANES), tile(LANES)],
        out_specs=(tile(RET_QK_WIDTH), tile(RET_QK_WIDTH), tile(RET_V_WIDTH), tile(RET_V_WIDTH)),
        compiler_params=_params("parallel"),
        name="ret_proj",
    )(x, gain, w, cos, sin)


def _ret_decay_matrix(t, lg):
    row = lax.broadcasted_iota(jnp.int32, (t, t), 0)
    col = lax.broadcasted_iota(jnp.int32, (t, t), 1)
    dist = jnp.abs(row - col).astype(F32)
    visible = (col // CHUNK) <= (row // CHUNK)
    return jnp.where(visible, jnp.exp(dist * lg), 0.0)


def _ret_tile(q_ref, k_ref, v_ref, sg_ref, y_ref, state_in, state_out, decay, t):
    lgs = _ret_log_gammas()
    pos = lax.broadcasted_iota(jnp.int32, (t, 1), 0).astype(F32)
    for hd in range(RET_HEADS):
        lg = lgs[hd]
        qs = slice(hd * RET_QK_DIM, (hd + 1) * RET_QK_DIM)
        vs = slice(hd * RET_V_DIM, (hd + 1) * RET_V_DIM)
        qh = q_ref[:, qs]
        kh = k_ref[:, qs]
        vh = v_ref[:, vs]
        state = state_in(hd)
        sc = lax.dot_general(qh, kh, _NT, preferred_element_type=F32) * decay(hd)
        o = _dot(sc.astype(BF16), vh) + _dot(qh, state.astype(BF16)) * jnp.exp(pos * lg)
        kd = (kh.astype(F32) * jnp.exp((t - pos) * lg)).astype(BF16)
        state_out(hd, state * math.exp(t * lg) + lax.dot_general(kd, vh, _TN, preferred_element_type=F32))
        mu = jnp.mean(o, axis=-1, keepdims=True)
        dev = o - mu
        var = jnp.mean(dev * dev, axis=-1, keepdims=True)
        y_ref[:, vs] = (sg_ref[:, vs].astype(F32) * (dev * lax.rsqrt(var + GN_EPS))).astype(BF16)


def _ret_prompt_body(q_ref, k_ref, v_ref, sg_ref, y_ref, st_ref, decay_ref):
    t = TOKEN_TILE

    @pl.when((pl.program_id(0) == 0) & (pl.program_id(1) == 0))
    def _():
        for hd, lg in enumerate(_ret_log_gammas()):
            decay_ref[hd] = _ret_decay_matrix(t, lg)

    @pl.when(pl.program_id(1) == 0)
    def _():
        st_ref[...] = jnp.zeros_like(st_ref)

    def state_out(hd, val):
        st_ref[0, hd] = val

    _ret_tile(q_ref, k_ref, v_ref, sg_ref, y_ref, lambda hd: st_ref[0, hd], state_out,
              lambda hd: decay_ref[hd], t)


def _ret_prompt(q, k, v, sg, batch, seq):
    nt = seq // TOKEN_TILE
    tile = lambda width: pl.BlockSpec((TOKEN_TILE, width), lambda b, i: (b * nt + i, 0))
    return pl.pallas_call(
        _ret_prompt_body,
        out_shape=(jax.ShapeDtypeStruct((q.shape[0], RET_V_WIDTH), BF16),
                   jax.ShapeDtypeStruct((batch, RET_HEADS, RET_QK_DIM, RET_V_DIM), F32)),
        grid=(batch, nt),
        in_specs=[tile(RET_QK_WIDTH), tile(RET_QK_WIDTH), tile(RET_V_WIDTH), tile(RET_V_WIDTH)],
        out_specs=(tile(RET_V_WIDTH),
                   pl.BlockSpec((1, RET_HEADS, RET_QK_DIM, RET_V_DIM), lambda b, i: (b, 0, 0, 0))),
        scratch_shapes=[pltpu.VMEM((RET_HEADS, TOKEN_TILE, TOKEN_TILE), F32)],
        compiler_params=_params("arbitrary", "arbitrary"),
        name="ret_prompt",
    )(q, k, v, sg)


def _ret_sample_body(q_ref, k_ref, v_ref, sg_ref, st_in_ref, y_any, y_ref, st_out_ref):
    del y_any
    t = q_ref.shape[0]
    lgs = _ret_log_gammas()

    def state_out(hd, val):
        st_out_ref[0, hd] = val

    _ret_tile(q_ref, k_ref, v_ref, sg_ref, y_ref, lambda hd: st_in_ref[0, hd], state_out,
              lambda hd: _ret_decay_matrix(t, lgs[hd]), t)


def _ret_sample(q, k, v, sg, state, y_all, n_new, row0):
    dec_batch = state.shape[0]
    blk0 = row0 // n_new
    tile = lambda width: pl.BlockSpec((n_new, width), lambda b: (blk0 + b, 0))
    st = pl.BlockSpec((1, RET_HEADS, RET_QK_DIM, RET_V_DIM), lambda b: (b, 0, 0, 0))
    return pl.pallas_call(
        _ret_sample_body,
        out_shape=(jax.ShapeDtypeStruct(y_all.shape, BF16), jax.ShapeDtypeStruct(state.shape, F32)),
        grid=(dec_batch,),
        in_specs=[tile(RET_QK_WIDTH), tile(RET_QK_WIDTH), tile(RET_V_WIDTH), tile(RET_V_WIDTH), st,
                  pl.BlockSpec(memory_space=pl.ANY)],
        out_specs=(tile(RET_V_WIDTH), st),
        input_output_aliases={5: 0},
        compiler_params=_params("parallel"),
        name="ret_sample",
    )(q, k, v, sg, state, y_all)


def _rotary_tables(pos):
    half = RET_QK_DIM // 2
    inv_freq = ROPE_BASE ** (-jnp.linspace(0.0, 1.0, half, dtype=F32))
    ang = pos[:, None] * inv_freq[None, :]
    cos, sin = jnp.cos(ang), jnp.sin(ang)
    return jnp.concatenate([cos, cos], axis=1), jnp.concatenate([-sin, sin], axis=1)


def _head_tiled(gain):
    return jnp.tile(gain.astype(F32), FOX_HEADS)


def kernel(x_prompt, x_sample, cache_fox_k, cache_fox_v, cache_fox_logf, state_ret, norm_gains, w_ffn_in,
           w_ffn_out, fox_w_in, fox_b_f, fox_q_gain, fox_k_gain, fox_w_out, ret_w_in, ret_w_out, final_gain):
    batch, seq, _ = x_prompt.shape
    dec_batch, n_new, _ = x_sample.shape
    past = cache_fox_k.shape[2]
    depth = norm_gains.shape[0]
    n_prompt = batch * seq
    n_sample = dec_batch * n_new
    assert seq % TOKEN_TILE == 0 and n_sample == TOKEN_TILE and TOKEN_TILE % n_new == 0
    assert seq % ATTN_Q == 0 and ATTN_Q % ATTN_K == 0 and past % MXU_DIM == 0
    assert TOKEN_TILE % CHUNK == 0 and n_new <= CHUNK
    n_fox = (depth + 1) // 2

    xs = (x_prompt.reshape(n_prompt, D_MODEL), x_sample.reshape(n_sample, D_MODEL))
    pos =jnp.concatenate([jnp.tile(jnp.arange(seq, dtype=F32), batch),
                           jnp.tile(past + jnp.arange(n_new, dtype=F32), dec_batch)])
    cos, sin = _rotary_tables(pos)
    grp_idx = np.arange(MXU_DIM) // FOX_HEAD_DIM
    grp = jnp.asarray(grp_idx[:, None] == grp_idx[None, :], dtype=BF16)
    sel_k, sel_q, const_k, const_q = _aug_tables()
    selk, ck = jnp.asarray(sel_k, BF16), jnp.asarray(const_k, F32)
    selqt, cqt = jnp.asarray(sel_q.T, BF16), jnp.asarray(const_q.T, F32)
    fgain = final_gain.reshape(1, D_MODEL)

    feature_major = lambda c: c.transpose(0, 1, 3, 4, 2).reshape(n_fox * dec_batch, D_MODEL, past)
    cache_k_t, cache_v_t = feature_major(cache_fox_k), feature_major(cache_fox_v)
    cache_lf_t = cache_fox_logf.transpose(0, 1, 3, 2).reshape(n_fox * dec_batch, FOX_HEADS, past)
    w_in_bf, w_out_bf = w_ffn_in.astype(BF16), w_ffn_out.astype(BF16)
    rs_p, fl_s, rs_s = [], [], []
    fox_stacked = ()
    for layer in range(depth):
        g = norm_gains[layer].reshape(3, 1, D_MODEL)
        x = _ffn(xs, g[0], w_in_bf, w_out_bf, (layer, 0), n_prompt, n_sample)
        j = layer // 2
        if layer % 2 == 0:
            w = fox_w_in[j]
            wf = jnp.pad(w[:, 4 * D_MODEL:], ((0, 0), (0, LANES - FOX_HEADS)))
            bf = jnp.pad(fox_b_f[j], (0, LANES - FOX_HEADS))
            wt = jnp.concatenate([w[:, 0:D_MODEL], w[:, 2 * D_MODEL:3 * D_MODEL]], axis=1).T
            qg, kg = _head_tiled(fox_q_gain[j]), _head_tiled(fox_k_gain[j])
            token_major = (x, g[1], w[:, :4 * D_MODEL].astype(BF16), wf.astype(BF16), bf.reshape(1, LANES))
            kg2, qg2 = kg.reshape(1, D_MODEL), qg.reshape(1, D_MODEL)
            bound = 1.01 * (FOX_HEAD_DIM ** 0.5) * jnp.max(jnp.abs(fox_q_gain[j])) * jnp.max(jnp.abs(fox_k_gain[j]))
            shift = jnp.full((LANES, 1), bound * LOG2E, F32)
            proj_ins = token_major[:3] + (
                kg2, grp, selk, ck, wt.astype(BF16), wf.T.astype(BF16), bf.reshape(LANES, 1),
                (qg * (FOX_Q_SCALE * LOG2E)).reshape(D_MODEL, 1), selqt, cqt, shift)
            qt, aqt, vt, kb, ak, gate, dt, *stk_p = _fox_proj(
                proj_ins, fox_stacked[:3], j, n_fox, n_prompt, seq // TOKEN_TILE)
            qs, ks, vs, gate_s, lf_s, *stk_s = _fox_proj_sample(
                token_major + (qg2, kg2, grp), fox_stacked[3:], j, n_fox, n_prompt // TOKEN_TILE)
            fox_stacked = (*stk_p, *stk_s)
            dh = dt.reshape(FOX_HEADS, batch, seq)
            dfirst = dh[:, :, 0::ATTN_Q].transpose(1, 0, 2).reshape(-1)
            dlast = dh[:, :, ATTN_K - 1::ATTN_K].transpose(1, 0, 2).reshape(-1)
            thr = (2.0 * bound + SKIP_MARGIN).reshape(1).astype(F32)
            frozen = (2.0 * LOG2E * bound < FROZEN_MAX_RANGE).reshape(1).astype(jnp.int32)
            y = _fox_attn_prompt(dfirst, dlast, thr, frozen, qt, aqt, kb, ak, vt, gate, batch, seq,
                                 n_prompt + n_sample)
            rt, nc, nt = _sample_bias(cache_lf_t, lf_s, n_new, 0, dec_batch, j)
            y = _fox_attn_sample(
                qs, vs, ks, gate_s, cache_k_t, cache_v_t, rt.reshape(dec_batch, FOX_PAIRS, 2, past),
                nc[:, :, :FOX_HEADS].reshape(dec_batch, n_new, FOX_PAIRS, 2).transpose(0, 2, 1, 3),
                nt.reshape(dec_batch, FOX_PAIRS, 2, n_new), y, n_new, n_prompt, j)
            mix = (y, fox_w_out[j].astype(BF16))
            fl_s.append(lf_s[:, :FOX_HEADS].reshape(dec_batch, n_new, FOX_HEADS))
        else:
            q, k, v, sg = _ret_proj(x, g[1], ret_w_in[j].astype(BF16), cos, sin)
            y, st_p = _ret_prompt(q, k, v, sg, batch, seq)
            y, st_s = _ret_sample(q, k, v, sg, state_ret[j], y, n_new, n_prompt)
            mix = (y, ret_w_out[j].astype(BF16))
            rs_p.append(st_p)
            rs_s.append(st_s)
        x = _ffn((x,), g[2], w_in_bf, w_out_bf, (layer, 1), n_prompt, n_sample,
                 mix=mix, final_gain=fgain if layer == depth - 1 else None)
        xs = (x,)
    y_prompt, y_sample = x
    kft_p, vft_p, lft_p, kf_s, vf_s = fox_stacked
    per_head = lambda a: a.reshape(n_fox, batch, FOX_HEADS, FOX_HEAD_DIM, seq).transpose(0, 1, 4, 2, 3)
    heads_s = (n_fox, dec_batch, n_new, FOX_HEADS, FOX_HEAD_DIM)
    return (y_prompt.reshape(batch, seq, D_MODEL), y_sample.reshape(dec_batch, n_new, D_MODEL),
            per_head(kft_p), per_head(vft_p), lft_p.transpose(0, 1, 3, 2), jnp.stack(rs_p),
            kf_s.reshape(heads_s), vf_s.reshape(heads_s), jnp.stack(fl_s), jnp.stack(rs_s))
```

```python
import functools
import math

import numpy as np
import jax
import jax.numpy as jnp
from jax import lax
from jax.experimental import pallas as pl
from jax.experimental.pallas import tpu as pltpu

F32 = jnp.float32
BF16 = jnp.bfloat16

D_MODEL = 1024
CHUNK = 64
FOX_HEADS = 16
FOX_HEAD_DIM = 64
FOX_PAIRS = FOX_HEADS // 2
RET_HEADS = 8
RET_QK_DIM = 128
RET_V_DIM = 256
RET_QK_WIDTH = RET_HEADS * RET_QK_DIM
RET_V_WIDTH = RET_HEADS * RET_V_DIM
D_FF = 2816
RMS_EPS = 1e-6
GN_EPS = 1e-5
ROPE_BASE = 10000.0

LANES = 128
MXU_DIM = 256
TOKEN_TILE = 512
FFN_CHUNK = MXU_DIM
ATTN_Q = 512
ATTN_K = 256
ATTN_GROUP = 4
OWN_EXTRA = 2
SAMPLE_PAIRS = 2
VMEM_LIMIT = 56 * 1024 * 1024
NEG_BIG = -1e30
LOG2E = 1.4426950408889634
FOX_Q_SCALE = FOX_HEAD_DIM ** -0.5
AUG_PIECES = 3
AUG_STRIDE = 16
SKIP_MARGIN = 32.0
FROZEN_MAX_RANGE = 100.0

_NT = (((1,), (1,)), ((), ()))
_TN = (((0,), (0,)), ((), ()))


def _params(*sem):
    return pltpu.CompilerParams(dimension_semantics=sem, vmem_limit_bytes=VMEM_LIMIT)


def _resident(shape):
    nd = len(shape)
    return pl.BlockSpec(shape, lambda *_: (0,) * nd, pipeline_mode=pl.Buffered(1))


def _rmsnorm(x, gain):
    ms = jnp.mean(x * x, axis=-1, keepdims=True)
    return x * lax.rsqrt(ms + RMS_EPS) * gain


def _dot(a, b):
    return jnp.dot(a, b, preferred_element_type=F32)


def _split3(x):
    hi = x.astype(BF16)
    r = x - hi.astype(F32)
    mid = r.astype(BF16)
    lo = (r - mid.astype(F32)).astype(BF16)
    return hi, mid, lo


def _tri(n, upper=False):
    row = lax.broadcasted_iota(jnp.int32, (n, n), 0)
    col = lax.broadcasted_iota(jnp.int32, (n, n), 1)
    return jnp.where((row <= col) if upper else (row >= col), 1.0, 0.0).astype(BF16)


def _cumsum_rows(tri, x):
    hi, mid, lo = _split3(x)
    return _dot(tri, hi) + _dot(tri, mid) + _dot(tri, lo)


def _cumsum_cols(tri_upper, x):
    hi, mid, lo = _split3(x)
    return _dot(hi, tri_upper) + _dot(mid, tri_upper) + _dot(lo, tri_upper)


def _log_sigmoid(z):
    return jnp.minimum(z, 0.0) - jnp.log1p(jnp.exp(-jnp.abs(z)))


def _ffn_body(*refs, first, mixed, final, n_prompt_tiles):
    refs = list(refs)
    x_refs = [refs.pop(0) for _ in range(2 if first else 1)]
    y_ref, wmix_ref = (refs.pop(0), refs.pop(0)) if mixed else (None, None)
    g_ref, win_ref, wout_ref = refs.pop(0), refs.pop(0), refs.pop(0)
    fg_ref = refs.pop(0) if final else None
    o_refs = [refs.pop(0) for _ in range(2 if final else 1)]
    (acc_ref,) = refs
    is_prompt = pl.program_id(0) < n_prompt_tiles
    x = jnp.where(is_prompt, x_refs[0][...], x_refs[1][...]) if first else x_refs[0][...]
    if mixed:
        x = x + _dot(y_ref[...], wmix_ref[...])
    h = _rmsnorm(x, g_ref[...]).astype(BF16)
    for c in range(D_FF // FFN_CHUNK):
        lo, hi = c * FFN_CHUNK, (c + 1) * FFN_CHUNK
        gate = _dot(h, win_ref[0, 0, :, lo:hi])
        up = _dot(h, win_ref[0, 0, :, D_FF + lo:D_FF + hi])
        act = (gate * jax.nn.sigmoid(gate) * up).astype(BF16)
        part = _dot(act, wout_ref[0, 0, lo:hi, :])
        if c == 0:
            acc_ref[...] = part
        else:
            acc_ref[...] += part
    y = x + 0.5 * acc_ref[...]
    if not final:
        o_refs[0][...] = y
        return
    y = _rmsnorm(y, fg_ref[...])

    @pl.when(is_prompt)
    def _():
        o_refs[0][...] = y

    @pl.when(jnp.logical_not(is_prompt))
    def _():
        o_refs[1][...] = y


def _ffn(xs, gain, w_in, w_out, which, n_prompt, n_sample, mix=None, final_gain=None):
    first, mixed, final = len(xs) == 2, mix is not None, final_gain is not None
    n = n_prompt + n_sample
    n_prompt_tiles = n_prompt // TOKEN_TILE
    tile = pl.BlockSpec((TOKEN_TILE, D_MODEL), lambda i: (i, 0))
    prompt_tile = pl.BlockSpec((TOKEN_TILE, D_MODEL), lambda i: (jnp.minimum(i, n_prompt_tiles - 1), 0))
    sample_tile = pl.BlockSpec((TOKEN_TILE, D_MODEL), lambda i: (jnp.maximum(i - n_prompt_tiles, 0), 0))
    ins, in_specs = list(xs), ([prompt_tile, sample_tile] if first else [tile])
    if mixed:
        y, w_mix = mix
        ins += [y, w_mix]
        in_specs += [pl.BlockSpec((TOKEN_TILE, y.shape[1]), lambda i: (i, 0)), _resident(w_mix.shape)]
    ins += [gain, w_in, w_out]
    pick = lambda *_: (*which, 0, 0)
    in_specs += [_resident((1, D_MODEL)),
                 pl.BlockSpec((1, 1, D_MODEL, 2 * D_FF), pick, pipeline_mode=pl.Buffered(1)),
                 pl.BlockSpec((1, 1, D_FF, D_MODEL), pick, pipeline_mode=pl.Buffered(1))]
    if final:
        ins.append(final_gain)
        in_specs.append(_resident((1, D_MODEL)))
        out_shape = (jax.ShapeDtypeStruct((n_prompt, D_MODEL), F32), jax.ShapeDtypeStruct((n_sample, D_MODEL), F32))
        out_specs = (prompt_tile, sample_tile)
    else:
        out_shape, out_specs = jax.ShapeDtypeStruct((n, D_MODEL), F32), tile
    return pl.pallas_call(
        functools.partial(_ffn_body, first=first, mixed=mixed, final=final, n_prompt_tiles=n_prompt_tiles),
        out_shape=out_shape,
        grid=(n // TOKEN_TILE,),
        in_specs=in_specs,
        out_specs=out_specs,
        scratch_shapes=[pltpu.VMEM((TOKEN_TILE, D_MODEL), F32)],
        compiler_params=_params("arbitrary"),
        name="ffn_final" if final else ("ffn_mixed" if mixed else "ffn"),
    )(*ins)


def _aug_tables():
    assert FOX_PAIRS * AUG_STRIDE == LANES
    sel_k = np.zeros((AUG_PIECES * LANES, LANES), np.float32)
    sel_q = np.zeros((AUG_PIECES * LANES, LANES), np.float32)
    const_k = np.zeros((1, LANES), np.float32)
    const_q = np.zeros((1, LANES), np.float32)
    for p in range(FOX_PAIRS):
        for hh in range(2):
            for piece in range(AUG_PIECES):
                src = piece * LANES + 2 * p + hh
                sel_k[src, p * AUG_STRIDE + 3 * hh + piece] = 1.0
                sel_q[src, p * AUG_STRIDE + 6 + 3 * hh + piece] = 1.0
                const_q[0, p * AUG_STRIDE + 3 * hh + piece] = -1.0
                const_k[0, p * AUG_STRIDE + 6 + 3 * hh + piece] = 1.0
    return sel_k, sel_q, const_k, const_q


def _head_rmsnorm(t, grp, gain):
    outs = []
    for c in range(D_MODEL // MXU_DIM):
        blk = t[:, c * MXU_DIM:(c + 1) * MXU_DIM]
        sq = blk * blk
        hi = sq.astype(BF16)
        lo = (sq - hi.astype(F32)).astype(BF16)
        ss = _dot(hi, grp) + _dot(lo, grp)
        outs.append(blk * lax.rsqrt(ss * (1.0 / FOX_HEAD_DIM) + RMS_EPS))
    return jnp.concatenate(outs, axis=1) * gain


def _head_rmsnorm_t(t, grp, gain):
    outs = []
    for c in range(D_MODEL // MXU_DIM):
        blk = t[c * MXU_DIM:(c + 1) * MXU_DIM, :]
        sq = blk * blk
        hi = sq.astype(BF16)
        lo = (sq - hi.astype(F32)).astype(BF16)
        ss = _dot(grp, hi) + _dot(grp, lo)
        outs.append(blk * lax.rsqrt(ss * (1.0 / FOX_HEAD_DIM) + RMS_EPS))
    return jnp.concatenate(outs, axis=0) * gain


N_FOX_PROJ_INPUTS = 14
N_FOX_SAMPLE_INPUTS = 8


def _fox_token_major(h, w_ref, wf_ref, bf_ref, kg_ref, grp, kb_o, gate_o, lf_o):
    w = D_MODEL
    k = _head_rmsnorm(_dot(h, w_ref[:, w:2 * w]), grp, kg_ref[...])
    kb_o[...] = k.astype(BF16)
    v = _dot(h, w_ref[:, 2 * w:3 * w])
    gate_o[...] = jax.nn.sigmoid(_dot(h, w_ref[:, 3 * w:4 * w])).astype(BF16)
    logf = _log_sigmoid(_dot(h, wf_ref[...]) + bf_ref[...])
    lf_o[...] = logf
    return k, v, logf


def _fox_proj_sample_body(*refs, n_stacked):
    x_ref, g_ref, w_ref, wf_ref, bf_ref, qg_ref, kg_ref, grp_ref = refs[:N_FOX_SAMPLE_INPUTS]
    qs_o, kb_o, vs_o, gate_o, lf_o, kf_o, vf_o = refs[N_FOX_SAMPLE_INPUTS + n_stacked:]
    h = _rmsnorm(x_ref[...], g_ref[...]).astype(BF16)
    grp = grp_ref[...]
    k, v, _ = _fox_token_major(h, w_ref, wf_ref, bf_ref, kg_ref, grp, kb_o, gate_o, lf_o)
    for hd in range(FOX_HEADS):
        cols = slice(hd * FOX_HEAD_DIM, (hd + 1) * FOX_HEAD_DIM)
        rows = pl.ds(hd, TOKEN_TILE, stride=FOX_HEADS)
        kf_o.at[0][rows, :] = k[:, cols]
        vf_o.at[0][rows, :] = v[:, cols]
    vs_o[...] = v.astype(BF16)
    q = _head_rmsnorm(_dot(h, w_ref[:, 0:D_MODEL]), grp, qg_ref[...])
    qs_o[...] = (q * FOX_Q_SCALE).astype(BF16)


def _fox_proj_sample(ins, stacked, layer, n_layers, tile0):
    assert len(ins) == N_FOX_SAMPLE_INPUTS
    n = ins[0].shape[0] - tile0 * TOKEN_TILE
    tile = pl.BlockSpec((TOKEN_TILE, D_MODEL), lambda i: (i, 0))
    slab = pl.BlockSpec((1, TOKEN_TILE * FOX_HEADS, FOX_HEAD_DIM), lambda i: (layer, i, 0))
    wide = jax.ShapeDtypeStruct((n, D_MODEL), BF16)
    stack = jax.ShapeDtypeStruct((n_layers, n * FOX_HEADS, FOX_HEAD_DIM), F32)
    n_out_before_stacks = 5
    return pl.pallas_call(
        functools.partial(_fox_proj_sample_body, n_stacked=len(stacked)),
        out_shape=(wide, wide, wide, wide, jax.ShapeDtypeStruct((n, LANES), F32), stack, stack),
        grid=(n // TOKEN_TILE,),
        in_specs=([pl.BlockSpec((TOKEN_TILE, D_MODEL), lambda i: (tile0 + i, 0))]
                  + [_resident(a.shape) for a in ins[1:]] + [pl.BlockSpec(memory_space=pl.ANY)] * len(stacked)),
        out_specs=(tile, tile, tile, tile, pl.BlockSpec((TOKEN_TILE, LANES), lambda i: (i, 0)), slab, slab),
        input_output_aliases={N_FOX_SAMPLE_INPUTS + s: n_out_before_stacks + s for s in range(len(stacked))},
        compiler_params=_params("parallel"),
        name="fox_proj_sample",
    )(*ins, *stacked)


def _fox_proj_body(*refs, tiles_per_seq, n_stacked):
    (x_ref, g_ref, w_ref, kg_ref, grp_ref, selk_ref, ck_ref,
     wt_ref, wft_ref, bft_ref, qgt_ref, selqt_ref, cqt_ref, shift_ref) = refs[:N_FOX_PROJ_INPUTS]
    (qt_o, aqt_o, vt_o, kb_o, ak_o, gate_o, dt_o, kft_o, vft_o, lft_o,
     carryt_ref) = refs[N_FOX_PROJ_INPUTS + n_stacked:]
    i = pl.program_id(0)
    tm, w = TOKEN_TILE, D_MODEL
    hf = _rmsnorm(x_ref[...], g_ref[...])
    h = hf.astype(BF16)
    ht = hf.T.astype(BF16)
    grp = grp_ref[...]

    @pl.when(i % tiles_per_seq == 0)
    def _():
        carryt_ref[...] = jnp.zeros_like(carryt_ref)

    k = _head_rmsnorm(_dot(h, w_ref[:, w:2 * w]), grp, kg_ref[...])
    kb_o[...] = k.astype(BF16)
    kft_o[0, 0] = k.T
    gate_o[...] = jax.nn.sigmoid(_dot(h, w_ref[:, 3 * w:4 * w])).astype(BF16)

    qt = _head_rmsnorm_t(_dot(wt_ref[0:w, :], ht), grp, qgt_ref[...])
    qt_o[...] = qt.astype(BF16)
    vt = _dot(wt_ref[w:2 * w, :], ht)
    vft_o[0, 0] = vt
    vt_o[...] = vt.astype(BF16)
    logft = _log_sigmoid(_dot(wft_ref[...], ht) + bft_ref[...])
    lft_o[0, 0] = logft[0:FOX_HEADS, :]
    dt = _cumsum_cols(_tri(tm, upper=True), logft) + carryt_ref[:, 0:1]
    carryt_ref[...] = jnp.broadcast_to(dt[:, tm - 1:tm], carryt_ref.shape)
    dt_o[...] = dt[0:FOX_HEADS, :]

    piecest = jnp.concatenate(_split3(dt * LOG2E - shift_ref[...]), axis=0)
    aqt_o[...] = (_dot(selqt_ref[...], piecest) + cqt_ref[...]).astype(BF16)
    pieces = jnp.concatenate(_split3(dt.T * LOG2E), axis=1)
    ak_o[...] = (_dot(pieces, selk_ref[...]) + ck_ref[...]).astype(BF16)


def _fox_proj(ins, stacked, layer, n_layers, n, tiles_per_seq):
    assert len(ins) == N_FOX_PROJ_INPUTS
    tps = tiles_per_seq
    batch, seq = n // (tps * TOKEN_TILE), tps * TOKEN_TILE
    tile = pl.BlockSpec((TOKEN_TILE, D_MODEL), lambda i: (i, 0))
    tile_t = pl.BlockSpec((D_MODEL, TOKEN_TILE), lambda i: (0, i))
    nar = pl.BlockSpec((TOKEN_TILE, LANES), lambda i: (i, 0))
    slab = pl.BlockSpec((1, 1, D_MODEL, TOKEN_TILE), lambda i: (layer, i // tps, 0, i % tps))
    slab_lf = pl.BlockSpec((1, 1, FOX_HEADS, TOKEN_TILE), lambda i: (layer, i // tps, 0, i % tps))
    wide = jax.ShapeDtypeStruct((n, D_MODEL), BF16)
    wide_t = jax.ShapeDtypeStruct((D_MODEL, n), BF16)
    stack = jax.ShapeDtypeStruct((n_layers, batch, D_MODEL, seq), F32)
    stack_lf = jax.ShapeDtypeStruct((n_layers, batch, FOX_HEADS, seq), F32)
    n_out_before_stacks = 7
    slots_t = pl.BlockSpec((LANES, TOKEN_TILE), lambda i: (0, i))
    heads_t = pl.BlockSpec((FOX_HEADS, TOKEN_TILE), lambda i: (0, i))
    return pl.pallas_call(
        functools.partial(_fox_proj_body, tiles_per_seq=tiles_per_seq, n_stacked=len(stacked)),
        out_shape=(wide_t, jax.ShapeDtypeStruct((LANES, n), BF16), wide_t, wide,
                   jax.ShapeDtypeStruct((n, LANES), BF16), wide, jax.ShapeDtypeStruct((FOX_HEADS, n), F32),
                   stack, stack, stack_lf),
        grid=(n // TOKEN_TILE,),
        in_specs=([tile] + [_resident(a.shape) for a in ins[1:]]
                  + [pl.BlockSpec(memory_space=pl.ANY)] * len(stacked)),
        out_specs=(tile_t, slots_t, tile_t, tile, nar, tile, heads_t, slab, slab, slab_lf),
        input_output_aliases={N_FOX_PROJ_INPUTS + s: n_out_before_stacks + s for s in range(len(stacked))},
        scratch_shapes=[pltpu.VMEM((LANES, LANES), F32)],
        compiler_params=_params("arbitrary"),
        name="fox_proj",
    )(*ins, *stacked)


def _fox_attn_body(dfirst_ref, dlast_ref, thr_ref, frozen_ref, qt_ref, aqt_ref, k_ref, ak_ref, vt_ref, gate_ref,
                   o_ref, qh_ref, m_ref, acc_ref, *, nq, nkb):
    tq, tk = ATTN_Q, ATTN_K
    half = FOX_HEAD_DIM
    per_q = tq // tk
    b, p, i = pl.program_id(0), pl.program_id(1), pl.program_id(2)

    qt = jnp.concatenate([qt_ref[...], aqt_ref[...]], axis=0)
    sub = lax.broadcasted_iota(jnp.int32, qt.shape, 0)
    slot = sub - LANES - p * AUG_STRIDE
    keep_a = (sub < half) | ((slot >= 0) & (slot < 3)) | ((slot >= 6) & (slot < 9))
    zero = jnp.zeros_like(qt)
    qh_ref[0] = jnp.where(keep_a, qt, zero)
    keep_b = ((sub >= half) & (sub < LANES)) | ((slot >= 3) & (slot < 6)) | ((slot >= 9) & (slot < 12))
    qh_ref[1] = jnp.where(keep_b, qt, zero)

    def step(j0, nblocks, heads, online, own=False):
        rows = nblocks * tk
        first_own_row = rows - tq
        off = pl.multiple_of(j0 * tk, tk)
        kk = jnp.concatenate([k_ref[pl.ds(off, rows), :], ak_ref[pl.ds(off, rows), :]], axis=1)
        vt = vt_ref[:, pl.ds(off, rows)]
        top = lax.broadcasted_iota(jnp.int32, (LANES, rows), 0) < half
        ones = jnp.ones((LANES, rows), BF16)
        for hh in heads:
            vsel = jnp.where(top, vt, ones) if hh == 0 else jnp.where(top, ones, vt)
            s = _dot(kk, qh_ref[hh])
            if own:
                kpos = lax.broadcasted_iota(jnp.int32, (rows, tq), 0) - first_own_row
                qpos = lax.broadcasted_iota(jnp.int32, (rows, tq), 1)
                s = jnp.where(kpos <= qpos, s, NEG_BIG)
            if not online:
                pv = _dot(vsel, jnp.exp2(s).astype(BF16))
                acc_ref[hh] = pv if own else acc_ref[hh] + pv
                continue
            if own:
                m_new = jnp.max(s, axis=0, keepdims=True)
                acc_ref[hh] = _dot(vsel, jnp.exp2(s - m_new).astype(BF16))
                m_ref[hh] = jnp.broadcast_to(m_new, m_ref.shape[1:])
                continue
            m_old = m_ref[hh, 0:1, :]
            m_new = jnp.maximum(m_old, jnp.max(s, axis=0, keepdims=True))
            alpha = jnp.exp2(m_old - m_new)
            pr = jnp.exp2(s - m_new).astype(BF16)
            acc_ref[hh] = alpha * acc_ref[hh] + _dot(vsel, pr)
            m_ref[hh] = jnp.broadcast_to(m_new, m_ref.shape[1:])

    thr = thr_ref[0]
    frozen_ok = frozen_ref[0] != 0
    j_top = i * per_q - 1
    group = ATTN_GROUP

    def live_blocks(head):
        first = dfirst_ref[(b * FOX_HEADS + head) * nq + i]
        base = (b * FOX_HEADS + head) * nkb

        def near(n):
            j = jnp.maximum(j_top - n, 0)
            return (n <= j_top) & (first - dlast_ref[base + j] >= -thr)

        return lax.while_loop(near, lambda n: n + 1, jnp.int32(0))

    n_live = (live_blocks(2 * p), live_blocks(2 * p + 1))

    def all_blocks(online):
        @pl.when(i == 0)
        def _():
            step(0, per_q, (0, 1), online, own=True)

        @pl.when(i > 0)
        def _():
            step(i * per_q - OWN_EXTRA, per_q + OWN_EXTRA, (0, 1), online, own=True)

        taken = jnp.where(i > 0, OWN_EXTRA, 0)
        j_top = i * per_q - 1 - taken
        n_left = tuple(jnp.maximum(n - taken, 0) for n in n_live)
        n_groups = (n_left[0] // group, n_left[1] // group)
        n_joint = jnp.minimum(n_groups[0], n_groups[1])

        def run_groups(lo, hi, heads):
            def body(g, carry):
                step(j_top - group * (g + 1) + 1, group, heads, online)
                return carry
            lax.fori_loop(lo, hi, body, 0)

        run_groups(0, n_joint, (0, 1))
        for hh in range(2):
            run_groups(n_joint, n_groups[hh], (hh,))
            done = n_groups[hh] * group
            size = group // 2
            while size >= 1:
                take = ((n_left[hh] - done) & size) != 0

                @pl.when(take)
                def _(done=done, size=size, hh=hh):
                    step(j_top - done - size + 1, size, (hh,), online)

                done = done + jnp.where(take, size, 0)
                size //= 2

    @pl.when(frozen_ok)
    def _():
        all_blocks(online=False)

    @pl.when(jnp.logical_not(frozen_ok))
    def _():
        all_blocks(online=True)

    acca, accb = acc_ref[0], acc_ref[1]
    ot = jnp.concatenate([acca[0:half, :] / acca[half:half + 1, :],
                          accb[half:LANES, :] / accb[0:1, :]], axis=0)
    o_ref[...] = (ot.T * gate_ref[...].astype(F32)).astype(BF16)


def _fox_attn_prompt(dfirst, dlast, thr, frozen, qt, aqt, kb, ak, vt, gate, batch, seq, n):
    nq = seq // ATTN_Q
    qspec_t = pl.BlockSpec((LANES, ATTN_Q), lambda b, p, i, *_: (p, b * nq + i))
    tok = pl.BlockSpec((ATTN_Q, LANES), lambda b, p, i, *_: (b * nq + i, p))
    kspec = pl.BlockSpec((seq, LANES), lambda b, p, i, *_: (b, p))
    vspec_t = pl.BlockSpec((LANES, seq), lambda b, p, i, *_: (p, b))
    return pl.pallas_call(
        functools.partial(_fox_attn_body, nq=nq, nkb=seq // ATTN_K),
        out_shape=jax.ShapeDtypeStruct((n, D_MODEL), BF16),
        grid_spec=pltpu.PrefetchScalarGridSpec(
            num_scalar_prefetch=4,
            grid=(batch, FOX_PAIRS, nq),
            in_specs=[qspec_t, pl.BlockSpec((LANES, ATTN_Q), lambda b, p, i, *_: (0, b * nq + i)), kspec,
                      pl.BlockSpec((seq, LANES), lambda b, p, i, *_: (b, 0)), vspec_t, tok],
            out_specs=tok,
            scratch_shapes=[pltpu.VMEM((2, 2 * LANES, ATTN_Q), BF16), pltpu.VMEM((2, 8, ATTN_Q), F32),
                            pltpu.VMEM((2, LANES, ATTN_Q), F32)]),
        compiler_params=_params("parallel", "parallel", "arbitrary"),
        name="fox_attn_prompt",
    )(dfirst, dlast, thr, frozen, qt, aqt, kb, ak, vt, gate)


def _pair_split(q2):
    lane = lax.broadcasted_iota(jnp.int32, q2.shape, 1)
    left = lane < FOX_HEAD_DIM
    zero = jnp.zeros_like(q2)
    return left, (jnp.where(left, q2, zero), jnp.where(left, zero, q2))


def _sample_bias_body(cl_ref, ln_ref, rt_o, ncol_o, nt_o):
    past = cl_ref.shape[2]
    n_new = ln_ref.shape[0]
    tri = _tri(MXU_DIM, upper=True)
    carry = jnp.zeros((FOX_HEADS, 1), F32)
    prefix = []
    for b in range(past // MXU_DIM):
        p = _cumsum_cols(tri, cl_ref[0, :, b * MXU_DIM:(b + 1) * MXU_DIM]) + carry
        carry = p[:, MXU_DIM - 1:MXU_DIM]
        prefix.append(p)
    for b, p in enumerate(prefix):
        rt_o[0, :, b * MXU_DIM:(b + 1) * MXU_DIM] = carry - p
    ln =jnp.concatenate([ln_ref[...], jnp.zeros((LANES - n_new, LANES), F32)], axis=0)
    nn = _cumsum_rows(_tri(LANES), ln)
    ncol_o[0] = nn[0:n_new, :]
    nt_o[0] = nn.T[0:FOX_HEADS, 0:n_new]


def _sample_bias(cl_t, lf, n_new, row0, dec_batch, layer):
    past = cl_t.shape[2]
    blk0 = row0 // n_new
    return pl.pallas_call(
        _sample_bias_body,
        out_shape=(jax.ShapeDtypeStruct((dec_batch, FOX_HEADS, past), F32),
                   jax.ShapeDtypeStruct((dec_batch, n_new, LANES), F32),
                   jax.ShapeDtypeStruct((dec_batch, FOX_HEADS, n_new), F32)),
        grid=(dec_batch,),
        in_specs=[pl.BlockSpec((1, FOX_HEADS, past), lambda b: (layer * dec_batch + b, 0, 0)),
                  pl.BlockSpec((n_new, LANES), lambda b: (blk0 + b, 0))],
        out_specs=(pl.BlockSpec((1, FOX_HEADS, past), lambda b: (b, 0, 0)),
                   pl.BlockSpec((1, n_new, LANES), lambda b: (b, 0, 0)),
                   pl.BlockSpec((1, FOX_HEADS, n_new), lambda b: (b, 0, 0))),
        compiler_params=_params("parallel"),
        name="fox_sample_bias",
    )(cl_t, lf)


def _fox_sample_body(q_ref, vn_ref, kn_ref, gate_ref, kc_ref, vc_ref, rt_ref, nc_ref, nt_ref,
                     y_any, o_ref):
    del y_any
    n_new = q_ref.shape[0]
    rt = rt_ref[0, 0]
    nc = nc_ref[0, 0]
    nt = nt_ref[0, 0]
    row = lax.broadcasted_iota(jnp.int32, (n_new, n_new), 0)
    col = lax.broadcasted_iota(jnp.int32, (n_new, n_new), 1)
    for pp in range(SAMPLE_PAIRS):
        lanes = slice(pp * LANES, (pp + 1) * LANES)
        left, qs = _pair_split(q_ref[:, lanes])
        kc = kc_ref[0, lanes, :].astype(BF16)
        vc = vc_ref[0, lanes, :].astype(BF16)
        kn = kn_ref[:, lanes]
        vn = vn_ref[:, lanes]
        outs = []
        for hh in range(2):
            hd = 2 * pp + hh
            nq = nc[:, hd:hd + 1]
            s1 = _dot(qs[hh], kc) + (nq + rt[hd:hd + 1, :])
            s2 = lax.dot_general(qs[hh], kn, _NT, preferred_element_type=F32) + (nq - nt[hd:hd + 1, :])
            s2 = jnp.where(col <= row, s2, NEG_BIG)
            m = jnp.maximum(jnp.max(s1, axis=1, keepdims=True), jnp.max(s2, axis=1, keepdims=True))
            p1 = jnp.exp(s1 - m)
            p2 = jnp.exp(s2 - m)
            l = jnp.sum(p1, axis=1, keepdims=True) + jnp.sum(p2, axis=1, keepdims=True)
            pv1 = lax.dot_general(p1.astype(BF16), vc, _NT, preferred_element_type=F32)
            outs.append((pv1 + _dot(p2.astype(BF16), vn)) / l)
        o = jnp.where(left, outs[0], outs[1])
        o_ref[:, lanes] = (o * gate_ref[:, lanes].astype(F32)).astype(BF16)


def _fox_attn_sample(qs, vs, kb, gate, kc, vc, rt, nc, nt, y_all, n_new, row0, layer):
    dec_batch, n_groups, _, past = rt.shape
    heads = 2 * SAMPLE_PAIRS
    width = SAMPLE_PAIRS * LANES
    blk0 = row0 // n_new
    own = pl.BlockSpec((n_new, width), lambda b, p: (b, p))
    new = pl.BlockSpec((n_new, width), lambda b, p: (blk0 + b, p))
    cache = pl.BlockSpec((1, width, past), lambda b, p: (layer * dec_batch + b, p, 0))
    return pl.pallas_call(
        _fox_sample_body,
        out_shape=jax.ShapeDtypeStruct(y_all.shape, BF16),
        grid=(dec_batch, n_groups),
        in_specs=[own, own, own, own, cache, cache,
                  pl.BlockSpec((1, 1, heads, past), lambda b, p: (b, p, 0, 0)),
                  pl.BlockSpec((1, 1, n_new, heads), lambda b, p: (b, p, 0, 0)),
                  pl.BlockSpec((1, 1, heads, n_new), lambda b, p: (b, p, 0, 0)),
                  pl.BlockSpec(memory_space=pl.ANY)],
        out_specs=new,
        input_output_aliases={9: 0},
        compiler_params=_params("parallel", "parallel"),
        name="fox_attn_sample",
    )(qs, vs, kb, gate, kc, vc, rt, nc, nt, y_all)


def _ret_log_gammas():
    return [float(v) for v in np.log1p(-np.exp2(-5.0 - np.arange(RET_HEADS, dtype=np.float32)))]


def _ret_proj_body(x_ref, g_ref, w_ref, cosp_ref, sinp_ref, coss_ref, sins_ref, q_o, k_o, v_o, sg_o,
                   *, n_prompt_tiles):
    h = _rmsnorm(x_ref[...], g_ref[...]).astype(BF16)
    is_prompt = pl.program_id(0) < n_prompt_tiles
    cos = jnp.where(is_prompt, cosp_ref[...], coss_ref[...])
    sin = jnp.where(is_prompt, sinp_ref[...], sins_ref[...])
    a, b2 = RET_QK_WIDTH, 2 * RET_QK_WIDTH

    def rotary(t, scale):
        outs = []
        for hd in range(RET_HEADS):
            th = t[:, hd * RET_QK_DIM:(hd + 1) * RET_QK_DIM]
            outs.append((th * cos + pltpu.roll(th, RET_QK_DIM // 2, 1) * sin) * scale)
        return jnp.concatenate(outs, axis=1).astype(BF16)

    q_o[...] = rotary(_dot(h, w_ref[:, 0:a]), 1.0)
    k_o[...] = rotary(_dot(h, w_ref[:, a:b2]), RET_QK_DIM ** -0.5)
    v_o[...] = _dot(h, w_ref[:, b2:b2 + RET_V_WIDTH]).astype(BF16)
    g = _dot(h, w_ref[:, b2 + RET_V_WIDTH:b2 + 2 * RET_V_WIDTH])
    sg_o[...] = (g * jax.nn.sigmoid(g)).astype(BF16)


def _ret_proj(x, gain, w, tables_p, tables_s, n_prompt):
    n = x.shape[0]
    n_prompt_tiles = n_prompt // TOKEN_TILE
    tiles_per_seq = tables_p[0].shape[0] // TOKEN_TILE
    tile = lambda width: pl.BlockSpec((TOKEN_TILE, width), lambda i: (i, 0))
    table_p = pl.BlockSpec((TOKEN_TILE, LANES), lambda i: (i % tiles_per_seq, 0))
    table_s = pl.BlockSpec((TOKEN_TILE, LANES), lambda i: (jnp.maximum(i - n_prompt_tiles, 0), 0))
    return pl.pallas_call(
        functools.partial(_ret_proj_body, n_prompt_tiles=n_prompt_tiles),
        out_shape=(jax.ShapeDtypeStruct((n, RET_QK_WIDTH), BF16), jax.ShapeDtypeStruct((n, RET_QK_WIDTH), BF16),
                   jax.ShapeDtypeStruct((n, RET_V_WIDTH), BF16), jax.ShapeDtypeStruct((n, RET_V_WIDTH), BF16)),
        grid=(n // TOKEN_TILE,),
        in_specs=[tile(D_MODEL), _resident((1, D_MODEL)), _resident(w.shape), table_p, table_p, table_s, table_s],
        out_specs=(tile(RET_QK_WIDTH), tile(RET_QK_WIDTH), tile(RET_V_WIDTH), tile(RET_V_WIDTH)),
        compiler_params=_params("parallel"),
        name="ret_proj",
    )(x, gain, w, *tables_p, *tables_s)


def _ret_decay_matrix(t, lg):
    row = lax.broadcasted_iota(jnp.int32, (t, t), 0)
    col = lax.broadcasted_iota(jnp.int32, (t, t), 1)
    dist = jnp.abs(row - col).astype(F32)
    visible = (col // CHUNK) <= (row // CHUNK)
    return jnp.where(visible, jnp.exp(dist * lg), 0.0)


def _ret_tile(q_ref, k_ref, v_ref, sg_ref, y_ref, state_in, state_out, decay, t):
    lgs = _ret_log_gammas()
    pos = lax.broadcasted_iota(jnp.int32, (t, 1), 0).astype(F32)
    for hd in range(RET_HEADS):
        lg = lgs[hd]
        qs = slice(hd * RET_QK_DIM, (hd + 1) * RET_QK_DIM)
        vs = slice(hd * RET_V_DIM, (hd + 1) * RET_V_DIM)
        qh = q_ref[:, qs]
        kh = k_ref[:, qs]
        vh = v_ref[:, vs]
        state = state_in(hd)
        sc = lax.dot_general(qh, kh, _NT, preferred_element_type=F32) * decay(hd)
        o = _dot(sc.astype(BF16), vh) + _dot(qh, state.astype(BF16)) * jnp.exp(pos * lg)
        kd = (kh.astype(F32) * jnp.exp((t - pos) * lg)).astype(BF16)
        state_out(hd, state * math.exp(t * lg) + lax.dot_general(kd, vh, _TN, preferred_element_type=F32))
        mu = jnp.mean(o, axis=-1, keepdims=True)
        dev = o - mu
        var = jnp.mean(dev * dev, axis=-1, keepdims=True)
        y_ref[:, vs] = (sg_ref[:, vs].astype(F32) * (dev * lax.rsqrt(var + GN_EPS))).astype(BF16)


def _ret_prompt_body(q_ref, k_ref, v_ref, sg_ref, y_ref, st_ref, decay_ref):
    t = TOKEN_TILE

    @pl.when((pl.program_id(0) == 0) & (pl.program_id(1) == 0))
    def _():
        for hd, lg in enumerate(_ret_log_gammas()):
            decay_ref[hd] = _ret_decay_matrix(t, lg)

    @pl.when(pl.program_id(1) == 0)
    def _():
        st_ref[...] = jnp.zeros_like(st_ref)

    def state_out(hd, val):
        st_ref[0, hd] = val

    _ret_tile(q_ref, k_ref, v_ref, sg_ref, y_ref, lambda hd: st_ref[0, hd], state_out,
              lambda hd: decay_ref[hd], t)


def _ret_prompt(q, k, v, sg, batch, seq):
    nt = seq // TOKEN_TILE
    tile = lambda width: pl.BlockSpec((TOKEN_TILE, width), lambda b, i: (b * nt + i, 0))
    return pl.pallas_call(
        _ret_prompt_body,
        out_shape=(jax.ShapeDtypeStruct((q.shape[0], RET_V_WIDTH), BF16),
                   jax.ShapeDtypeStruct((batch, RET_HEADS, RET_QK_DIM, RET_V_DIM), F32)),
        grid=(batch, nt),
        in_specs=[tile(RET_QK_WIDTH), tile(RET_QK_WIDTH), tile(RET_V_WIDTH), tile(RET_V_WIDTH)],
        out_specs=(tile(RET_V_WIDTH),
                   pl.BlockSpec((1, RET_HEADS, RET_QK_DIM, RET_V_DIM), lambda b, i: (b, 0, 0, 0))),
        scratch_shapes=[pltpu.VMEM((RET_HEADS, TOKEN_TILE, TOKEN_TILE), F32)],
        compiler_params=_params("arbitrary", "arbitrary"),
        name="ret_prompt",
    )(q, k, v, sg)


def _ret_sample_body(q_ref, k_ref, v_ref, sg_ref, st_in_ref, *rest):
    y_ref, st_out_ref = rest[-2:]
    t = q_ref.shape[0]
    lgs = _ret_log_gammas()

    def state_out(hd, val):
        st_out_ref[0, 0, hd] = val

    _ret_tile(q_ref, k_ref, v_ref, sg_ref, y_ref, lambda hd: st_in_ref[0, 0, hd], state_out,
              lambda hd: _ret_decay_matrix(t, lgs[hd]), t)


def _ret_sample(q, k, v, sg, states, y_all, stacked, n_new, row0, layer):
    dec_batch = states.shape[1]
    blk0 = row0 // n_new
    tile = lambda width: pl.BlockSpec((n_new, width), lambda b: (blk0 + b, 0))
    st = pl.BlockSpec((1, 1, RET_HEADS, RET_QK_DIM, RET_V_DIM), lambda b: (layer, b, 0, 0, 0))
    any_spec = pl.BlockSpec(memory_space=pl.ANY)
    return pl.pallas_call(
        _ret_sample_body,
        out_shape=(jax.ShapeDtypeStruct(y_all.shape, BF16), jax.ShapeDtypeStruct(states.shape, F32)),
        grid=(dec_batch,),
        in_specs=[tile(RET_QK_WIDTH), tile(RET_QK_WIDTH), tile(RET_V_WIDTH), tile(RET_V_WIDTH), st, any_spec]
        + [any_spec] * len(stacked),
        out_specs=(tile(RET_V_WIDTH), st),
        input_output_aliases={5: 0, **({6: 1} if stacked else {})},
        compiler_params=_params("parallel"),
        name="ret_sample",
    )(q, k, v, sg, states, y_all, *stacked)


def _rotary_tables(pos):
    half = RET_QK_DIM // 2
    inv_freq = ROPE_BASE ** (-jnp.linspace(0.0, 1.0, half, dtype=F32))
    ang = pos[:, None] * inv_freq[None, :]
    cos, sin = jnp.cos(ang), jnp.sin(ang)
    return jnp.concatenate([cos, cos], axis=1), jnp.concatenate([-sin, sin], axis=1)


def _head_tiled(gain):
    return jnp.tile(gain.astype(F32), FOX_HEADS)


def kernel(x_prompt, x_sample, cache_fox_k, cache_fox_v, cache_fox_logf, state_ret, norm_gains, w_ffn_in,
           w_ffn_out, fox_w_in, fox_b_f, fox_q_gain, fox_k_gain, fox_w_out, ret_w_in, ret_w_out, final_gain):
    batch, seq, _ = x_prompt.shape
    dec_batch, n_new, _ = x_sample.shape
    past = cache_fox_k.shape[2]
    depth = norm_gains.shape[0]
    n_prompt = batch * seq
    n_sample = dec_batch * n_new
    assert seq % TOKEN_TILE == 0 and n_sample == TOKEN_TILE and TOKEN_TILE % n_new == 0
    assert seq % ATTN_Q == 0 and ATTN_Q % ATTN_K == 0 and past % MXU_DIM == 0
    assert TOKEN_TILE % CHUNK == 0 and n_new <= CHUNK
    n_fox = (depth + 1) // 2

    xs = (x_prompt.reshape(n_prompt, D_MODEL), x_sample.reshape(n_sample, D_MODEL))
    tables_p = _rotary_tables(jnp.arange(seq, dtype=F32))
    tables_s = tuple(jnp.tile(t, (dec_batch, 1)) for t in _rotary_tables(past + jnp.arange(n_new, dtype=F32)))
    grp_idx = np.arange(MXU_DIM) // FOX_HEAD_DIM
    grp = jnp.asarray(grp_idx[:, None] == grp_idx[None, :], dtype=BF16)
    sel_k, sel_q, const_k, const_q = _aug_tables()
    selk, ck = jnp.asarray(sel_k, BF16), jnp.asarray(const_k, F32)
    selqt, cqt = jnp.asarray(sel_q.T, BF16), jnp.asarray(const_q.T, F32)
    fgain = final_gain.reshape(1, D_MODEL)

    feature_major = lambda c: c.transpose(0, 1, 3, 4, 2).reshape(n_fox * dec_batch, D_MODEL, past)
    cache_k_t, cache_v_t = feature_major(cache_fox_k), feature_major(cache_fox_v)
    cache_lf_t = cache_fox_logf.transpose(0, 1, 3, 2).reshape(n_fox * dec_batch, FOX_HEADS, past)
    w_in_bf, w_out_bf = w_ffn_in.astype(BF16), w_ffn_out.astype(BF16)
    rs_p, fl_s = [], []
    ret_stacked = ()
    fox_stacked = ()
    for layer in range(depth):
        g = norm_gains[layer].reshape(3, 1, D_MODEL)
        x = _ffn(xs, g[0], w_in_bf, w_out_bf, (layer, 0), n_prompt, n_sample)
        j = layer // 2
        if layer % 2 == 0:
            w = fox_w_in[j]
            wf = jnp.pad(w[:, 4 * D_MODEL:], ((0, 0), (0, LANES - FOX_HEADS)))
            bf = jnp.pad(fox_b_f[j], (0, LANES - FOX_HEADS))
            wt = jnp.concatenate([w[:, 0:D_MODEL], w[:, 2 * D_MODEL:3 * D_MODEL]], axis=1).T
            qg, kg = _head_tiled(fox_q_gain[j]), _head_tiled(fox_k_gain[j])
            token_major = (x, g[1], w[:, :4 * D_MODEL].astype(BF16), wf.astype(BF16), bf.reshape(1, LANES))
            kg2, qg2 = kg.reshape(1, D_MODEL), qg.reshape(1, D_MODEL)
            bound = 1.01 * (FOX_HEAD_DIM ** 0.5) * jnp.max(jnp.abs(fox_q_gain[j])) * jnp.max(jnp.abs(fox_k_gain[j]))
            shift = jnp.full((LANES, 1), bound * LOG2E, F32)
            proj_ins = token_major[:3] + (
                kg2, grp, selk, ck, wt.astype(BF16), wf.T.astype(BF16), bf.reshape(LANES, 1),
                (qg * (FOX_Q_SCALE * LOG2E)).reshape(D_MODEL, 1), selqt, cqt, shift)
            qt, aqt, vt, kb, ak, gate, dt, *stk_p = _fox_proj(
                proj_ins, fox_stacked[:3], j, n_fox, n_prompt, seq // TOKEN_TILE)
            qs, ks, vs, gate_s, lf_s, *stk_s = _fox_proj_sample(
                token_major + (qg2, kg2, grp), fox_stacked[3:], j, n_fox, n_prompt // TOKEN_TILE)
            fox_stacked = (*stk_p, *stk_s)
            dh = dt.reshape(FOX_HEADS, batch, seq)
            dfirst = dh[:, :, 0::ATTN_Q].transpose(1, 0, 2).reshape(-1)
            dlast = dh[:, :, ATTN_K - 1::ATTN_K].transpose(1, 0, 2).reshape(-1)
            thr = (2.0 * bound + SKIP_MARGIN).reshape(1).astype(F32)
            frozen = (2.0 * LOG2E * bound < FROZEN_MAX_RANGE).reshape(1).astype(jnp.int32)
            y = _fox_attn_prompt(dfirst, dlast, thr, frozen, qt, aqt, kb, ak, vt, gate, batch, seq,
                                 n_prompt + n_sample)
            rt, nc, nt = _sample_bias(cache_lf_t, lf_s, n_new, 0, dec_batch, j)
            per_group = 2 * SAMPLE_PAIRS
            groups = FOX_HEADS // per_group
            y = _fox_attn_sample(
                qs, vs, ks, gate_s, cache_k_t, cache_v_t, rt.reshape(dec_batch, groups, per_group, past),
                nc[:, :, :FOX_HEADS].reshape(dec_batch, n_new, groups, per_group).transpose(0, 2, 1, 3),
                nt.reshape(dec_batch, groups, per_group, n_new), y, n_new, n_prompt, j)
            mix = (y, fox_w_out[j].astype(BF16))
            fl_s.append(lf_s[:, :FOX_HEADS].reshape(dec_batch, n_new, FOX_HEADS))
        else:
            q, k, v, sg = _ret_proj(x, g[1], ret_w_in[j].astype(BF16), tables_p, tables_s, n_prompt)
            y, st_p = _ret_prompt(q, k, v, sg, batch, seq)
            y, st_s = _ret_sample(q, k, v, sg, state_ret, y, ret_stacked, n_new, n_prompt, j)
            ret_stacked = (st_s,)
            mix = (y, ret_w_out[j].astype(BF16))
            rs_p.append(st_p)
        x = _ffn((x,), g[2], w_in_bf, w_out_bf, (layer, 1), n_prompt, n_sample,
                 mix=mix, final_gain=fgain if layer == depth - 1 else None)
        xs = (x,)
    y_prompt, y_sample = x
    kft_p, vft_p, lft_p, kf_s, vf_s = fox_stacked
    per_head = lambda a: a.reshape(n_fox, batch, FOX_HEADS, FOX_HEAD_DIM, seq).transpose(0, 1, 4, 2, 3)
    heads_s = (n_fox, dec_batch, n_new, FOX_HEADS, FOX_HEAD_DIM)
    return (y_prompt.reshape(batch, seq, D_MODEL), y_sample.reshape(dec_batch, n_new, D_MODEL),
            per_head(kft_p), per_head(vft_p), lft_p.transpose(0, 1, 3, 2), jnp.stack(rs_p),
            kf_s.reshape(heads_s), vf_s.reshape(heads_s), jnp.stack(fl_s), ret_stacked[0])
```

```python
import functools
import math

import numpy as np
import jax
import jax.numpy as jnp
from jax import lax
from jax.experimental import pallas as pl
from jax.experimental.pallas import tpu as pltpu

F32 = jnp.float32
BF16 = jnp.bfloat16

D_MODEL = 1024
CHUNK = 64
FOX_HEADS = 16
FOX_HEAD_DIM = 64
FOX_PAIRS = FOX_HEADS // 2
RET_HEADS = 8
RET_QK_DIM = 128
RET_V_DIM = 256
RET_QK_WIDTH = RET_HEADS * RET_QK_DIM
RET_V_WIDTH = RET_HEADS * RET_V_DIM
D_FF = 2816
RMS_EPS = 1e-6
GN_EPS = 1e-5
ROPE_BASE = 10000.0

LANES = 128
MXU_DIM = 256
TOKEN_TILE = 512
FFN_CHUNK = MXU_DIM
ATTN_Q = 512
ATTN_K = 256
ATTN_GROUP = 4
OWN_EXTRA = 2
SAMPLE_PAIRS = 4
VMEM_LIMIT = 56 * 1024 * 1024
NEG_BIG = -1e30
LOG2E = 1.4426950408889634
FOX_Q_SCALE = FOX_HEAD_DIM ** -0.5
AUG_PIECES = 3
AUG_STRIDE = 16
SKIP_MARGIN = 32.0
FROZEN_MAX_RANGE = 100.0

_NT = (((1,), (1,)), ((), ()))
_TN = (((0,), (0,)), ((), ()))


def _params(*sem):
    return pltpu.CompilerParams(dimension_semantics=sem, vmem_limit_bytes=VMEM_LIMIT)


def _resident(shape):
    nd = len(shape)
    return pl.BlockSpec(shape, lambda *_: (0,) * nd, pipeline_mode=pl.Buffered(1))


def _rmsnorm(x, gain):
    ms = jnp.mean(x * x, axis=-1, keepdims=True)
    return x * lax.rsqrt(ms + RMS_EPS) * gain


def _dot(a, b):
    return jnp.dot(a, b, preferred_element_type=F32)


def _split3(x):
    hi = x.astype(BF16)
    r = x - hi.astype(F32)
    mid = r.astype(BF16)
    lo = (r - mid.astype(F32)).astype(BF16)
    return hi, mid, lo


def _tri(n, upper=False):
    row = lax.broadcasted_iota(jnp.int32, (n, n), 0)
    col = lax.broadcasted_iota(jnp.int32, (n, n), 1)
    return jnp.where((row <= col) if upper else (row >= col), 1.0, 0.0).astype(BF16)


def _cumsum_rows(tri, x):
    hi, mid, lo = _split3(x)
    return _dot(tri, hi) + _dot(tri, mid) + _dot(tri, lo)


def _cumsum_cols(tri_upper, x):
    hi, mid, lo = _split3(x)
    return _dot(hi, tri_upper) + _dot(mid, tri_upper) + _dot(lo, tri_upper)


def _log_sigmoid(z):
    return jnp.minimum(z, 0.0) - jnp.log1p(jnp.exp(-jnp.abs(z)))


def _ffn_body(*refs, first, mixed, final, n_prompt_tiles):
    refs = list(refs)
    x_refs = [refs.pop(0) for _ in range(2 if first else 1)]
    y_ref, wmix_ref = (refs.pop(0), refs.pop(0)) if mixed else (None, None)
    g_ref, win_ref, wout_ref = refs.pop(0), refs.pop(0), refs.pop(0)
    fg_ref = refs.pop(0) if final else None
    o_refs = [refs.pop(0) for _ in range(2 if final else 1)]
    (acc_ref,) = refs
    is_prompt = pl.program_id(0) < n_prompt_tiles
    x = jnp.where(is_prompt, x_refs[0][...], x_refs[1][...]) if first else x_refs[0][...]
    if mixed:
        x = x + _dot(y_ref[...], wmix_ref[...])
    h = _rmsnorm(x, g_ref[...]).astype(BF16)
    for c in range(D_FF // FFN_CHUNK):
        lo, hi = c * FFN_CHUNK, (c + 1) * FFN_CHUNK
        gate = _dot(h, win_ref[0, 0, :, lo:hi])
        up = _dot(h, win_ref[0, 0, :, D_FF + lo:D_FF + hi])
        act = (gate * jax.nn.sigmoid(gate) * up).astype(BF16)
        part = _dot(act, wout_ref[0, 0, lo:hi, :])
        if c == 0:
            acc_ref[...] = part
        else:
            acc_ref[...] += part
    y = x + 0.5 * acc_ref[...]
    if not final:
        o_refs[0][...] = y
        return
    y = _rmsnorm(y, fg_ref[...])

    @pl.when(is_prompt)
    def _():
        o_refs[0][...] = y

    @pl.when(jnp.logical_not(is_prompt))
    def _():
        o_refs[1][...] = y


def _ffn(xs, gain, w_in, w_out, which, n_prompt, n_sample, mix=None, final_gain=None):
    first, mixed, final = len(xs) == 2, mix is not None, final_gain is not None
    n = n_prompt + n_sample
    n_prompt_tiles = n_prompt // TOKEN_TILE
    tile = pl.BlockSpec((TOKEN_TILE, D_MODEL), lambda i: (i, 0))
    prompt_tile = pl.BlockSpec((TOKEN_TILE, D_MODEL), lambda i: (jnp.minimum(i, n_prompt_tiles - 1), 0))
    sample_tile = pl.BlockSpec((TOKEN_TILE, D_MODEL), lambda i: (jnp.maximum(i - n_prompt_tiles, 0), 0))
    ins, in_specs = list(xs), ([prompt_tile, sample_tile] if first else [tile])
    if mixed:
        y, w_mix = mix
        ins += [y, w_mix]
        in_specs += [pl.BlockSpec((TOKEN_TILE, y.shape[1]), lambda i: (i, 0)), _resident(w_mix.shape)]
    ins += [gain, w_in, w_out]
    pick = lambda *_: (*which, 0, 0)
    in_specs += [_resident((1, D_MODEL)),
                 pl.BlockSpec((1, 1, D_MODEL, 2 * D_FF), pick, pipeline_mode=pl.Buffered(1)),
                 pl.BlockSpec((1, 1, D_FF, D_MODEL), pick, pipeline_mode=pl.Buffered(1))]
    if final:
        ins.append(final_gain)
        in_specs.append(_resident((1, D_MODEL)))
        out_shape = (jax.ShapeDtypeStruct((n_prompt, D_MODEL), F32), jax.ShapeDtypeStruct((n_sample, D_MODEL), F32))
        out_specs = (prompt_tile, sample_tile)
    else:
        out_shape, out_specs = jax.ShapeDtypeStruct((n, D_MODEL), F32), tile
    return pl.pallas_call(
        functools.partial(_ffn_body, first=first, mixed=mixed, final=final, n_prompt_tiles=n_prompt_tiles),
        out_shape=out_shape,
        grid=(n // TOKEN_TILE,),
        in_specs=in_specs,
        out_specs=out_specs,
        scratch_shapes=[pltpu.VMEM((TOKEN_TILE, D_MODEL), F32)],
        compiler_params=_params("arbitrary"),
        name="ffn_final" if final else ("ffn_mixed" if mixed else "ffn"),
    )(*ins)


def _aug_tables():
    assert FOX_PAIRS * AUG_STRIDE == LANES
    sel_k = np.zeros((AUG_PIECES * LANES, LANES), np.float32)
    sel_q = np.zeros((AUG_PIECES * LANES, LANES), np.float32)
    const_k = np.zeros((1, LANES), np.float32)
    const_q = np.zeros((1, LANES), np.float32)
    for p in range(FOX_PAIRS):
        for hh in range(2):
            for piece in range(AUG_PIECES):
                src = piece * LANES + 2 * p + hh
                sel_k[src, p * AUG_STRIDE + 3 * hh + piece] = 1.0
                sel_q[src, p * AUG_STRIDE + 6 + 3 * hh + piece] = 1.0
                const_q[0, p * AUG_STRIDE + 3 * hh + piece] = -1.0
                const_k[0, p * AUG_STRIDE + 6 + 3 * hh + piece] = 1.0
    return sel_k, sel_q, const_k, const_q


def _head_rmsnorm(t, grp, gain):
    outs = []
    for c in range(D_MODEL // MXU_DIM):
        blk = t[:, c * MXU_DIM:(c + 1) * MXU_DIM]
        sq = blk * blk
        hi = sq.astype(BF16)
        lo = (sq - hi.astype(F32)).astype(BF16)
        ss = _dot(hi, grp) + _dot(lo, grp)
        outs.append(blk * lax.rsqrt(ss * (1.0 / FOX_HEAD_DIM) + RMS_EPS))
    return jnp.concatenate(outs, axis=1) * gain


def _head_rmsnorm_t(t, grp, gain):
    outs = []
    for c in range(D_MODEL // MXU_DIM):
        blk = t[c * MXU_DIM:(c + 1) * MXU_DIM, :]
        sq = blk * blk
        hi = sq.astype(BF16)
        lo = (sq - hi.astype(F32)).astype(BF16)
        ss = _dot(grp, hi) + _dot(grp, lo)
        outs.append(blk * lax.rsqrt(ss * (1.0 / FOX_HEAD_DIM) + RMS_EPS))
    return jnp.concatenate(outs, axis=0) * gain


N_FOX_PROJ_INPUTS = 14
N_FOX_SAMPLE_INPUTS = 8


def _fox_token_major(h, w_ref, wf_ref, bf_ref, kg_ref, grp, kb_o, gate_o, lf_o):
    w = D_MODEL
    k = _head_rmsnorm(_dot(h, w_ref[:, w:2 * w]), grp, kg_ref[...])
    kb_o[...] = k.astype(BF16)
    v = _dot(h, w_ref[:, 2 * w:3 * w])
    gate_o[...] = jax.nn.sigmoid(_dot(h, w_ref[:, 3 * w:4 * w])).astype(BF16)
    logf = _log_sigmoid(_dot(h, wf_ref[...]) + bf_ref[...])
    lf_o[...] = logf
    return k, v, logf


def _fox_proj_sample_body(*refs, n_stacked):
    x_ref, g_ref, w_ref, wf_ref, bf_ref, qg_ref, kg_ref, grp_ref = refs[:N_FOX_SAMPLE_INPUTS]
    qs_o, kb_o, vs_o, gate_o, lf_o, kf_o, vf_o = refs[N_FOX_SAMPLE_INPUTS + n_stacked:]
    h = _rmsnorm(x_ref[...], g_ref[...]).astype(BF16)
    grp = grp_ref[...]
    k, v, _ = _fox_token_major(h, w_ref, wf_ref, bf_ref, kg_ref, grp, kb_o, gate_o, lf_o)
    for hd in range(FOX_HEADS):
        cols = slice(hd * FOX_HEAD_DIM, (hd + 1) * FOX_HEAD_DIM)
        rows = pl.ds(hd, TOKEN_TILE, stride=FOX_HEADS)
        kf_o.at[0][rows, :] = k[:, cols]
        vf_o.at[0][rows, :] = v[:, cols]
    vs_o[...] = v.astype(BF16)
    q = _head_rmsnorm(_dot(h, w_ref[:, 0:D_MODEL]), grp, qg_ref[...])
    qs_o[...] = (q * FOX_Q_SCALE).astype(BF16)


def _fox_proj_sample(ins, stacked, layer, n_layers, tile0):
    assert len(ins) == N_FOX_SAMPLE_INPUTS
    n = ins[0].shape[0] - tile0 * TOKEN_TILE
    tile = pl.BlockSpec((TOKEN_TILE, D_MODEL), lambda i: (i, 0))
    slab = pl.BlockSpec((1, TOKEN_TILE * FOX_HEADS, FOX_HEAD_DIM), lambda i: (layer, i, 0))
    wide = jax.ShapeDtypeStruct((n, D_MODEL), BF16)
    stack = jax.ShapeDtypeStruct((n_layers, n * FOX_HEADS, FOX_HEAD_DIM), F32)
    n_out_before_stacks = 5
    return pl.pallas_call(
        functools.partial(_fox_proj_sample_body, n_stacked=len(stacked)),
        out_shape=(wide, wide, wide, wide, jax.ShapeDtypeStruct((n, LANES), F32), stack, stack),
        grid=(n // TOKEN_TILE,),
        in_specs=([pl.BlockSpec((TOKEN_TILE, D_MODEL), lambda i: (tile0 + i, 0))]
                  + [_resident(a.shape) for a in ins[1:]] + [pl.BlockSpec(memory_space=pl.ANY)] * len(stacked)),
        out_specs=(tile, tile, tile, tile, pl.BlockSpec((TOKEN_TILE, LANES), lambda i: (i, 0)), slab, slab),
        input_output_aliases={N_FOX_SAMPLE_INPUTS + s: n_out_before_stacks + s for s in range(len(stacked))},
        compiler_params=_params("parallel"),
        name="fox_proj_sample",
    )(*ins, *stacked)


def _fox_proj_body(*refs, tiles_per_seq, n_stacked):
    (x_ref, g_ref, w_ref, kg_ref, grp_ref, selk_ref, ck_ref,
     wt_ref, wft_ref, bft_ref, qgt_ref, selqt_ref, cqt_ref, shift_ref) = refs[:N_FOX_PROJ_INPUTS]
    (qt_o, aqt_o, vt_o, kb_o, ak_o, gate_o, dt_o, kft_o, vft_o, lft_o,
     carryt_ref) = refs[N_FOX_PROJ_INPUTS + n_stacked:]
    i = pl.program_id(0)
    tm, w = TOKEN_TILE, D_MODEL
    hf = _rmsnorm(x_ref[...], g_ref[...])
    h = hf.astype(BF16)
    ht = hf.T.astype(BF16)
    grp = grp_ref[...]

    @pl.when(i % tiles_per_seq == 0)
    def _():
        carryt_ref[...] = jnp.zeros_like(carryt_ref)

    k = _head_rmsnorm(_dot(h, w_ref[:, w:2 * w]), grp, kg_ref[...])
    kb_o[...] = k.astype(BF16)
    kft_o[0, 0] = k.T
    gate_o[...] = jax.nn.sigmoid(_dot(h, w_ref[:, 3 * w:4 * w])).astype(BF16)

    qt = _head_rmsnorm_t(_dot(wt_ref[0:w, :], ht), grp, qgt_ref[...])
    qt_o[...] = qt.astype(BF16)
    vt = _dot(wt_ref[w:2 * w, :], ht)
    vft_o[0, 0] = vt
    vt_o[...] = vt.astype(BF16)
    logft = _log_sigmoid(_dot(wft_ref[...], ht) + bft_ref[...])
    lft_o[0, 0] = logft[0:FOX_HEADS, :]
    dt = _cumsum_cols(_tri(tm, upper=True), logft) + carryt_ref[:, 0:1]
    carryt_ref[...] = jnp.broadcast_to(dt[:, tm - 1:tm], carryt_ref.shape)
    dt_o[...] = dt[0:FOX_HEADS, :]

    piecest = jnp.concatenate(_split3(dt * LOG2E - shift_ref[...]), axis=0)
    aqt_o[...] = (_dot(selqt_ref[...], piecest) + cqt_ref[...]).astype(BF16)
    pieces = jnp.concatenate(_split3(dt.T * LOG2E), axis=1)
    ak_o[...] = (_dot(pieces, selk_ref[...]) + ck_ref[...]).astype(BF16)


def _fox_proj(ins, stacked, layer, n_layers, n, tiles_per_seq):
    assert len(ins) == N_FOX_PROJ_INPUTS
    tps = tiles_per_seq
    batch, seq = n // (tps * TOKEN_TILE), tps * TOKEN_TILE
    tile = pl.BlockSpec((TOKEN_TILE, D_MODEL), lambda i: (i, 0))
    tile_t = pl.BlockSpec((D_MODEL, TOKEN_TILE), lambda i: (0, i))
    nar = pl.BlockSpec((TOKEN_TILE, LANES), lambda i: (i, 0))
    slab = pl.BlockSpec((1, 1, D_MODEL, TOKEN_TILE), lambda i: (layer, i // tps, 0, i % tps))
    slab_lf = pl.BlockSpec((1, 1, FOX_HEADS, TOKEN_TILE), lambda i: (layer, i // tps, 0, i % tps))
    wide = jax.ShapeDtypeStruct((n, D_MODEL), BF16)
    wide_t = jax.ShapeDtypeStruct((D_MODEL, n), BF16)
    stack = jax.ShapeDtypeStruct((n_layers, batch, D_MODEL, seq), F32)
    stack_lf = jax.ShapeDtypeStruct((n_layers, batch, FOX_HEADS, seq), F32)
    n_out_before_stacks = 7
    slots_t = pl.BlockSpec((LANES, TOKEN_TILE), lambda i: (0, i))
    heads_t = pl.BlockSpec((FOX_HEADS, TOKEN_TILE), lambda i: (0, i))
    return pl.pallas_call(
        functools.partial(_fox_proj_body, tiles_per_seq=tiles_per_seq, n_stacked=len(stacked)),
        out_shape=(wide_t, jax.ShapeDtypeStruct((LANES, n), BF16), wide_t, wide,
                   jax.ShapeDtypeStruct((n, LANES), BF16), wide, jax.ShapeDtypeStruct((FOX_HEADS, n), F32),
                   stack, stack, stack_lf),
        grid=(n // TOKEN_TILE,),
        in_specs=([tile] + [_resident(a.shape) for a in ins[1:]]
                  + [pl.BlockSpec(memory_space=pl.ANY)] * len(stacked)),
        out_specs=(tile_t, slots_t, tile_t, tile, nar, tile, heads_t, slab, slab, slab_lf),
        input_output_aliases={N_FOX_PROJ_INPUTS + s: n_out_before_stacks + s for s in range(len(stacked))},
        scratch_shapes=[pltpu.VMEM((LANES, LANES), F32)],
        compiler_params=_params("arbitrary"),
        name="fox_proj",
    )(*ins, *stacked)


def _fox_attn_body(dfirst_ref, dlast_ref, thr_ref, frozen_ref, qt_ref, aqt_ref, k_ref, ak_ref, vt_ref, gate_ref,
                   o_ref, qh_ref, m_ref, acc_ref, *, nq, nkb):
    tq, tk = ATTN_Q, ATTN_K
    half = FOX_HEAD_DIM
    per_q = tq // tk
    b, p, i = pl.program_id(0), pl.program_id(1), pl.program_id(2)

    qt = jnp.concatenate([qt_ref[...], aqt_ref[...]], axis=0)
    sub = lax.broadcasted_iota(jnp.int32, qt.shape, 0)
    slot = sub - LANES - p * AUG_STRIDE
    keep_a = (sub < half) | ((slot >= 0) & (slot < 3)) | ((slot >= 6) & (slot < 9))
    zero = jnp.zeros_like(qt)
    qh_ref[0] = jnp.where(keep_a, qt, zero)
    keep_b = ((sub >= half) & (sub < LANES)) | ((slot >= 3) & (slot < 6)) | ((slot >= 9) & (slot < 12))
    qh_ref[1] = jnp.where(keep_b, qt, zero)

    def step(j0, nblocks, heads, online, own=False):
        rows = nblocks * tk
        first_own_row = rows - tq
        off = pl.multiple_of(j0 * tk, tk)
        kk = jnp.concatenate([k_ref[pl.ds(off, rows), :], ak_ref[pl.ds(off, rows), :]], axis=1)
        vt = vt_ref[:, pl.ds(off, rows)]
        top = lax.broadcasted_iota(jnp.int32, (LANES, rows), 0) < half
        ones = jnp.ones((LANES, rows), BF16)
        for hh in heads:
            vsel = jnp.where(top, vt, ones) if hh == 0 else jnp.where(top, ones, vt)
            s = _dot(kk, qh_ref[hh])
            if own:
                kpos = lax.broadcasted_iota(jnp.int32, (rows, tq), 0) - first_own_row
                qpos = lax.broadcasted_iota(jnp.int32, (rows, tq), 1)
                s = jnp.where(kpos <= qpos, s, NEG_BIG)
            if not online:
                pv = _dot(vsel, jnp.exp2(s).astype(BF16))
                acc_ref[hh] = pv if own else acc_ref[hh] + pv
                continue
            if own:
                m_new = jnp.max(s, axis=0, keepdims=True)
                acc_ref[hh] = _dot(vsel, jnp.exp2(s - m_new).astype(BF16))
                m_ref[hh] = jnp.broadcast_to(m_new, m_ref.shape[1:])
                continue
            m_old = m_ref[hh, 0:1, :]
            m_new = jnp.maximum(m_old, jnp.max(s, axis=0, keepdims=True))
            alpha = jnp.exp2(m_old - m_new)
            pr = jnp.exp2(s - m_new).astype(BF16)
            acc_ref[hh] = alpha * acc_ref[hh] + _dot(vsel, pr)
            m_ref[hh] = jnp.broadcast_to(m_new, m_ref.shape[1:])

    thr = thr_ref[0]
    frozen_ok = frozen_ref[0] != 0
    j_top = i * per_q - 1
    group = ATTN_GROUP

    def live_blocks(head):
        first = dfirst_ref[(b * FOX_HEADS + head) * nq + i]
        base = (b * FOX_HEADS + head) * nkb

        def near(n):
            j = jnp.maximum(j_top - n, 0)
            return (n <= j_top) & (first - dlast_ref[base + j] >= -thr)

        return lax.while_loop(near, lambda n: n + 1, jnp.int32(0))

    n_live = (live_blocks(2 * p), live_blocks(2 * p + 1))

    def all_blocks(online):
        @pl.when(i == 0)
        def _():
            step(0, per_q, (0, 1), online, own=True)

        @pl.when(i > 0)
        def _():
            step(i * per_q - OWN_EXTRA, per_q + OWN_EXTRA, (0, 1), online, own=True)

        taken = jnp.where(i > 0, OWN_EXTRA, 0)
        j_top = i * per_q - 1 - taken
        n_left = tuple(jnp.maximum(n - taken, 0) for n in n_live)
        n_groups = (n_left[0] // group, n_left[1] // group)
        n_joint = jnp.minimum(n_groups[0], n_groups[1])

        def run_groups(lo, hi, heads):
            def body(g, carry):
                step(j_top - group * (g + 1) + 1, group, heads, online)
                return carry
            lax.fori_loop(lo, hi, body, 0)

        run_groups(0, n_joint, (0, 1))
        for hh in range(2):
            run_groups(n_joint, n_groups[hh], (hh,))
            done = n_groups[hh] * group
            size = group // 2
            while size >= 1:
                take = ((n_left[hh] - done) & size) != 0

                @pl.when(take)
                def _(done=done, size=size, hh=hh):
                    step(j_top - done - size + 1, size, (hh,), online)

                done = done + jnp.where(take, size, 0)
                size //= 2

    @pl.when(frozen_ok)
    def _():
        all_blocks(online=False)

    @pl.when(jnp.logical_not(frozen_ok))
    def _():
        all_blocks(online=True)

    acca, accb = acc_ref[0], acc_ref[1]
    ot = jnp.concatenate([acca[0:half, :] / acca[half:half + 1, :],
                          accb[half:LANES, :] / accb[0:1, :]], axis=0)
    o_ref[...] = (ot.T * gate_ref[...].astype(F32)).astype(BF16)


def _fox_attn_prompt(dfirst, dlast, thr, frozen, qt, aqt, kb, ak, vt, gate, batch, seq, n):
    nq = seq // ATTN_Q
    qspec_t = pl.BlockSpec((LANES, ATTN_Q), lambda b, p, i, *_: (p, b * nq + i))
    tok = pl.BlockSpec((ATTN_Q, LANES), lambda b, p, i, *_: (b * nq + i, p))
    kspec = pl.BlockSpec((seq, LANES), lambda b, p, i, *_: (b, p))
    vspec_t = pl.BlockSpec((LANES, seq), lambda b, p, i, *_: (p, b))
    return pl.pallas_call(
        functools.partial(_fox_attn_body, nq=nq, nkb=seq // ATTN_K),
        out_shape=jax.ShapeDtypeStruct((n, D_MODEL), BF16),
        grid_spec=pltpu.PrefetchScalarGridSpec(
            num_scalar_prefetch=4,
            grid=(batch, FOX_PAIRS, nq),
            in_specs=[qspec_t, pl.BlockSpec((LANES, ATTN_Q), lambda b, p, i, *_: (0, b * nq + i)), kspec,
                      pl.BlockSpec((seq, LANES), lambda b, p, i, *_: (b, 0)), vspec_t, tok],
            out_specs=tok,
            scratch_shapes=[pltpu.VMEM((2, 2 * LANES, ATTN_Q), BF16), pltpu.VMEM((2, 8, ATTN_Q), F32),
                            pltpu.VMEM((2, LANES, ATTN_Q), F32)]),
        compiler_params=_params("parallel", "parallel", "arbitrary"),
        name="fox_attn_prompt",
    )(dfirst, dlast, thr, frozen, qt, aqt, kb, ak, vt, gate)


def _pair_split(q2):
    lane = lax.broadcasted_iota(jnp.int32, q2.shape, 1)
    left = lane < FOX_HEAD_DIM
    zero = jnp.zeros_like(q2)
    return left, (jnp.where(left, q2, zero), jnp.where(left, zero, q2))


def _sample_bias_body(cl_ref, ln_ref, rt_o, ncol_o, nt_o):
    past = cl_ref.shape[2]
    n_new = ln_ref.shape[0]
    tri = _tri(MXU_DIM, upper=True)
    carry = jnp.zeros((FOX_HEADS, 1), F32)
    prefix = []
    for b in range(past // MXU_DIM):
        p = _cumsum_cols(tri, cl_ref[0, :, b * MXU_DIM:(b + 1) * MXU_DIM]) + carry
        carry = p[:, MXU_DIM - 1:MXU_DIM]
        prefix.append(p)
    for b, p in enumerate(prefix):
        rt_o[0, :, b * MXU_DIM:(b + 1) * MXU_DIM] = carry - p
    ln =jnp.concatenate([ln_ref[...], jnp.zeros((LANES - n_new, LANES), F32)], axis=0)
    nn = _cumsum_rows(_tri(LANES), ln)
    ncol_o[0] = nn[0:n_new, :]
    nt_o[0] = nn.T[0:FOX_HEADS, 0:n_new]


def _sample_bias(cl_t, lf, n_new, row0, dec_batch, layer):
    past = cl_t.shape[2]
    blk0 = row0 // n_new
    return pl.pallas_call(
        _sample_bias_body,
        out_shape=(jax.ShapeDtypeStruct((dec_batch, FOX_HEADS, past), F32),
                   jax.ShapeDtypeStruct((dec_batch, n_new, LANES), F32),
                   jax.ShapeDtypeStruct((dec_batch, FOX_HEADS, n_new), F32)),
        grid=(dec_batch,),
        in_specs=[pl.BlockSpec((1, FOX_HEADS, past), lambda b: (layer * dec_batch + b, 0, 0)),
                  pl.BlockSpec((n_new, LANES), lambda b: (blk0 + b, 0))],
        out_specs=(pl.BlockSpec((1, FOX_HEADS, past), lambda b: (b, 0, 0)),
                   pl.BlockSpec((1, n_new, LANES), lambda b: (b, 0, 0)),
                   pl.BlockSpec((1, FOX_HEADS, n_new), lambda b: (b, 0, 0))),
        compiler_params=_params("parallel"),
        name="fox_sample_bias",
    )(cl_t, lf)


def _fox_sample_body(q_ref, vn_ref, kn_ref, gate_ref, kc_ref, vc_ref, rt_ref, nc_ref, nt_ref,
                     y_any, o_ref):
    del y_any
    n_new = q_ref.shape[0]
    rt = rt_ref[0, 0]
    nc = nc_ref[0, 0]
    nt = nt_ref[0, 0]
    row = lax.broadcasted_iota(jnp.int32, (n_new, n_new), 0)
    col = lax.broadcasted_iota(jnp.int32, (n_new, n_new), 1)
    for pp in range(SAMPLE_PAIRS):
        lanes = slice(pp * LANES, (pp + 1) * LANES)
        left, qs = _pair_split(q_ref[:, lanes])
        kc = kc_ref[0, lanes, :].astype(BF16)
        vc = vc_ref[0, lanes, :].astype(BF16)
        kn = kn_ref[:, lanes]
        vn = vn_ref[:, lanes]
        outs = []
        for hh in range(2):
            hd = 2 * pp + hh
            nq = nc[:, hd:hd + 1]
            s1 = _dot(qs[hh], kc) + (nq + rt[hd:hd + 1, :])
            s2 = lax.dot_general(qs[hh], kn, _NT, preferred_element_type=F32) + (nq - nt[hd:hd + 1, :])
            s2 = jnp.where(col <= row, s2, NEG_BIG)
            m = jnp.maximum(jnp.max(s1, axis=1, keepdims=True), jnp.max(s2, axis=1, keepdims=True))
            p1 = jnp.exp(s1 - m)
            p2 = jnp.exp(s2 - m)
            l = jnp.sum(p1, axis=1, keepdims=True) + jnp.sum(p2, axis=1, keepdims=True)
            pv1 = lax.dot_general(p1.astype(BF16), vc, _NT, preferred_element_type=F32)
            outs.append((pv1 + _dot(p2.astype(BF16), vn)) / l)
        o = jnp.where(left, outs[0], outs[1])
        o_ref[:, lanes] = (o * gate_ref[:, lanes].astype(F32)).astype(BF16)


def _fox_attn_sample(qs, vs, kb, gate, kc, vc, rt, nc, nt, y_all, n_new, row0, layer):
    dec_batch, n_groups, _, past = rt.shape
    heads = 2 * SAMPLE_PAIRS
    width = SAMPLE_PAIRS * LANES
    blk0 = row0 // n_new
    own = pl.BlockSpec((n_new, width), lambda b, p: (b, p))
    new = pl.BlockSpec((n_new, width), lambda b, p: (blk0 + b, p))
    cache = pl.BlockSpec((1, width, past), lambda b, p: (layer * dec_batch + b, p, 0))
    return pl.pallas_call(
        _fox_sample_body,
        out_shape=jax.ShapeDtypeStruct(y_all.shape, BF16),
        grid=(dec_batch, n_groups),
        in_specs=[own, own, own, own, cache, cache,
                  pl.BlockSpec((1, 1, heads, past), lambda b, p: (b, p, 0, 0)),
                  pl.BlockSpec((1, 1, n_new, heads), lambda b, p: (b, p, 0, 0)),
                  pl.BlockSpec((1, 1, heads, n_new), lambda b, p: (b, p, 0, 0)),
                  pl.BlockSpec(memory_space=pl.ANY)],
        out_specs=new,
        input_output_aliases={9: 0},
        compiler_params=_params("parallel", "parallel"),
        name="fox_attn_sample",
    )(qs, vs, kb, gate, kc, vc, rt, nc, nt, y_all)


def _ret_log_gammas():
    return [float(v) for v in np.log1p(-np.exp2(-5.0 - np.arange(RET_HEADS, dtype=np.float32)))]


def _ret_proj_body(x_ref, g_ref, w_ref, cosp_ref, sinp_ref, coss_ref, sins_ref, q_o, k_o, v_o, sg_o,
                   *, n_prompt_tiles):
    h = _rmsnorm(x_ref[...], g_ref[...]).astype(BF16)
    is_prompt = pl.program_id(0) < n_prompt_tiles
    cos = jnp.where(is_prompt, cosp_ref[...], coss_ref[...])
    sin = jnp.where(is_prompt, sinp_ref[...], sins_ref[...])
    a, b2 = RET_QK_WIDTH, 2 * RET_QK_WIDTH

    def rotary(t, scale):
        outs = []
        for hd in range(RET_HEADS):
            th = t[:, hd * RET_QK_DIM:(hd + 1) * RET_QK_DIM]
            outs.append((th * cos + pltpu.roll(th, RET_QK_DIM // 2, 1) * sin) * scale)
        return jnp.concatenate(outs, axis=1).astype(BF16)

    q_o[...] = rotary(_dot(h, w_ref[:, 0:a]), 1.0)
    k_o[...] = rotary(_dot(h, w_ref[:, a:b2]), RET_QK_DIM ** -0.5)
    v_o[...] = _dot(h, w_ref[:, b2:b2 + RET_V_WIDTH]).astype(BF16)
    g = _dot(h, w_ref[:, b2 + RET_V_WIDTH:b2 + 2 * RET_V_WIDTH])
    sg_o[...] = (g * jax.nn.sigmoid(g)).astype(BF16)


def _ret_proj(x, gain, w, tables_p, tables_s, n_prompt):
    n = x.shape[0]
    n_prompt_tiles = n_prompt // TOKEN_TILE
    tiles_per_seq = tables_p[0].shape[0] // TOKEN_TILE
    tile = lambda width: pl.BlockSpec((TOKEN_TILE, width), lambda i: (i, 0))
    table_p = pl.BlockSpec((TOKEN_TILE, LANES), lambda i: (i % tiles_per_seq, 0))
    table_s = pl.BlockSpec((TOKEN_TILE, LANES), lambda i: (jnp.maximum(i - n_prompt_tiles, 0), 0))
    return pl.pallas_call(
        functools.partial(_ret_proj_body, n_prompt_tiles=n_prompt_tiles),
        out_shape=(jax.ShapeDtypeStruct((n, RET_QK_WIDTH), BF16), jax.ShapeDtypeStruct((n, RET_QK_WIDTH), BF16),
                   jax.ShapeDtypeStruct((n, RET_V_WIDTH), BF16), jax.ShapeDtypeStruct((n, RET_V_WIDTH), BF16)),
        grid=(n // TOKEN_TILE,),
        in_specs=[tile(D_MODEL), _resident((1, D_MODEL)), _resident(w.shape), table_p, table_p, table_s, table_s],
        out_specs=(tile(RET_QK_WIDTH), tile(RET_QK_WIDTH), tile(RET_V_WIDTH), tile(RET_V_WIDTH)),
        compiler_params=_params("parallel"),
        name="ret_proj",
    )(x, gain, w, *tables_p, *tables_s)


def _ret_decay_matrix(t, lg):
    row = lax.broadcasted_iota(jnp.int32, (t, t), 0)
    col = lax.broadcasted_iota(jnp.int32, (t, t), 1)
    dist = jnp.abs(row - col).astype(F32)
    visible = (col // CHUNK) <= (row // CHUNK)
    return jnp.where(visible, jnp.exp(dist * lg), 0.0)


def _ret_tile(q_ref, k_ref, v_ref, sg_ref, y_ref, state_in, state_out, decay, t):
    lgs = _ret_log_gammas()
    pos = lax.broadcasted_iota(jnp.int32, (t, 1), 0).astype(F32)
    for hd in range(RET_HEADS):
        lg = lgs[hd]
        qs = slice(hd * RET_QK_DIM, (hd + 1) * RET_QK_DIM)
        vs = slice(hd * RET_V_DIM, (hd + 1) * RET_V_DIM)
        qh = q_ref[:, qs]
        kh = k_ref[:, qs]
        vh = v_ref[:, vs]
        state = state_in(hd)
        sc = lax.dot_general(qh, kh, _NT, preferred_element_type=F32) * decay(hd)
        o = _dot(sc.astype(BF16), vh) + _dot(qh, state.astype(BF16)) * jnp.exp(pos * lg)
        kd = (kh.astype(F32) * jnp.exp((t - pos) * lg)).astype(BF16)
        state_out(hd, state * math.exp(t * lg) + lax.dot_general(kd, vh, _TN, preferred_element_type=F32))
        mu = jnp.mean(o, axis=-1, keepdims=True)
        dev = o - mu
        var = jnp.mean(dev * dev, axis=-1, keepdims=True)
        y_ref[:, vs] = (sg_ref[:, vs].astype(F32) * (dev * lax.rsqrt(var + GN_EPS))).astype(BF16)


def _ret_prompt_body(q_ref, k_ref, v_ref, sg_ref, y_ref, st_ref, decay_ref):
    t = TOKEN_TILE

    @pl.when((pl.program_id(0) == 0) & (pl.program_id(1) == 0))
    def _():
        for hd, lg in enumerate(_ret_log_gammas()):
            decay_ref[hd] = _ret_decay_matrix(t, lg)

    @pl.when(pl.program_id(1) == 0)
    def _():
        st_ref[...] = jnp.zeros_like(st_ref)

    def state_out(hd, val):
        st_ref[0, hd] = val

    _ret_tile(q_ref, k_ref, v_ref, sg_ref, y_ref, lambda hd: st_ref[0, hd], state_out,
              lambda hd: decay_ref[hd], t)


def _ret_prompt(q, k, v, sg, batch, seq):
    nt = seq // TOKEN_TILE
    tile = lambda width: pl.BlockSpec((TOKEN_TILE, width), lambda b, i: (b * nt + i, 0))
    return pl.pallas_call(
        _ret_prompt_body,
        out_shape=(jax.ShapeDtypeStruct((q.shape[0], RET_V_WIDTH), BF16),
                   jax.ShapeDtypeStruct((batch, RET_HEADS, RET_QK_DIM, RET_V_DIM), F32)),
        grid=(batch, nt),
        in_specs=[tile(RET_QK_WIDTH), tile(RET_QK_WIDTH), tile(RET_V_WIDTH), tile(RET_V_WIDTH)],
        out_specs=(tile(RET_V_WIDTH),
                   pl.BlockSpec((1, RET_HEADS, RET_QK_DIM, RET_V_DIM), lambda b, i: (b, 0, 0, 0))),
        scratch_shapes=[pltpu.VMEM((RET_HEADS, TOKEN_TILE, TOKEN_TILE), F32)],
        compiler_params=_params("arbitrary", "arbitrary"),
        name="ret_prompt",
    )(q, k, v, sg)


def _ret_sample_body(q_ref, k_ref, v_ref, sg_ref, st_in_ref, *rest):
    y_ref, st_out_ref = rest[-2:]
    t = q_ref.shape[0]
    lgs = _ret_log_gammas()

    def state_out(hd, val):
        st_out_ref[0, 0, hd] = val

    _ret_tile(q_ref, k_ref, v_ref, sg_ref, y_ref, lambda hd: st_in_ref[0, 0, hd], state_out,
              lambda hd: _ret_decay_matrix(t, lgs[hd]), t)


def _ret_sample(q, k, v, sg, states, y_all, stacked, n_new, row0, layer):
    dec_batch = states.shape[1]
    blk0 = row0 // n_new
    tile = lambda width: pl.BlockSpec((n_new, width), lambda b: (blk0 + b, 0))
    st = pl.BlockSpec((1, 1, RET_HEADS, RET_QK_DIM, RET_V_DIM), lambda b: (layer, b, 0, 0, 0))
    any_spec = pl.BlockSpec(memory_space=pl.ANY)
    return pl.pallas_call(
        _ret_sample_body,
        out_shape=(jax.ShapeDtypeStruct(y_all.shape, BF16), jax.ShapeDtypeStruct(states.shape, F32)),
        grid=(dec_batch,),
        in_specs=[tile(RET_QK_WIDTH), tile(RET_QK_WIDTH), tile(RET_V_WIDTH), tile(RET_V_WIDTH), st, any_spec]
        + [any_spec] * len(stacked),
        out_specs=(tile(RET_V_WIDTH), st),
        input_output_aliases={5: 0, **({6: 1} if stacked else {})},
        compiler_params=_params("parallel"),
        name="ret_sample",
    )(q, k, v, sg, states, y_all, *stacked)


def _rotary_tables(pos):
    half = RET_QK_DIM // 2
    inv_freq = ROPE_BASE ** (-jnp.linspace(0.0, 1.0, half, dtype=F32))
    ang = pos[:, None] * inv_freq[None, :]
    cos, sin = jnp.cos(ang), jnp.sin(ang)
    return jnp.concatenate([cos, cos], axis=1), jnp.concatenate([-sin, sin], axis=1)


def _head_tiled(gain):
    return jnp.tile(gain.astype(F32), FOX_HEADS)


def kernel(x_prompt, x_sample, cache_fox_k, cache_fox_v, cache_fox_logf, state_ret, norm_gains, w_ffn_in,
           w_ffn_out, fox_w_in, fox_b_f, fox_q_gain, fox_k_gain, fox_w_out, ret_w_in, ret_w_out, final_gain):
    batch, seq, _ = x_prompt.shape
    dec_batch, n_new, _ = x_sample.shape
    past = cache_fox_k.shape[2]
    depth = norm_gains.shape[0]
    n_prompt = batch * seq
    n_sample = dec_batch * n_new
    assert seq % TOKEN_TILE == 0 and n_sample == TOKEN_TILE and TOKEN_TILE % n_new == 0
    assert seq % ATTN_Q == 0 and ATTN_Q % ATTN_K == 0 and past % MXU_DIM == 0
    assert TOKEN_TILE % CHUNK == 0 and n_new <= CHUNK
    n_fox = (depth + 1) // 2

    xs = (x_prompt.reshape(n_prompt, D_MODEL), x_sample.reshape(n_sample, D_MODEL))
    tables_p = _rotary_tables(jnp.arange(seq, dtype=F32))
    tables_s = tuple(jnp.tile(t, (dec_batch, 1)) for t in _rotary_tables(past + jnp.arange(n_new, dtype=F32)))
    grp_idx = np.arange(MXU_DIM) // FOX_HEAD_DIM
    grp = jnp.asarray(grp_idx[:, None] == grp_idx[None, :], dtype=BF16)
    sel_k, sel_q, const_k, const_q = _aug_tables()
    selk, ck = jnp.asarray(sel_k, BF16), jnp.asarray(const_k, F32)
    selqt, cqt = jnp.asarray(sel_q.T, BF16), jnp.asarray(const_q.T, F32)
    fgain = final_gain.reshape(1, D_MODEL)

    feature_major = lambda c: c.transpose(0, 1, 3, 4, 2).reshape(n_fox * dec_batch, D_MODEL, past)
    cache_k_t, cache_v_t = feature_major(cache_fox_k), feature_major(cache_fox_v)
    cache_lf_t = cache_fox_logf.transpose(0, 1, 3, 2).reshape(n_fox * dec_batch, FOX_HEADS, past)
    w_in_bf, w_out_bf = w_ffn_in.astype(BF16), w_ffn_out.astype(BF16)
    rs_p, fl_s = [], []
    ret_stacked = ()
    fox_stacked = ()
    for layer in range(depth):
        g = norm_gains[layer].reshape(3, 1, D_MODEL)
        x = _ffn(xs, g[0], w_in_bf, w_out_bf, (layer, 0), n_prompt, n_sample)
        j = layer // 2
        if layer % 2 == 0:
            w = fox_w_in[j]
            wf = jnp.pad(w[:, 4 * D_MODEL:], ((0, 0), (0, LANES - FOX_HEADS)))
            bf = jnp.pad(fox_b_f[j], (0, LANES - FOX_HEADS))
            wt = jnp.concatenate([w[:, 0:D_MODEL], w[:, 2 * D_MODEL:3 * D_MODEL]], axis=1).T
            qg, kg = _head_tiled(fox_q_gain[j]), _head_tiled(fox_k_gain[j])
            token_major = (x, g[1], w[:, :4 * D_MODEL].astype(BF16), wf.astype(BF16), bf.reshape(1, LANES))
            kg2, qg2 = kg.reshape(1, D_MODEL), qg.reshape(1, D_MODEL)
            bound = 1.01 * (FOX_HEAD_DIM ** 0.5) * jnp.max(jnp.abs(fox_q_gain[j])) * jnp.max(jnp.abs(fox_k_gain[j]))
            shift = jnp.full((LANES, 1), bound * LOG2E, F32)
            proj_ins = token_major[:3] + (
                kg2, grp, selk, ck, wt.astype(BF16), wf.T.astype(BF16), bf.reshape(LANES, 1),
                (qg * (FOX_Q_SCALE * LOG2E)).reshape(D_MODEL, 1), selqt, cqt, shift)
            qt, aqt, vt, kb, ak, gate, dt, *stk_p = _fox_proj(
                proj_ins, fox_stacked[:3], j, n_fox, n_prompt, seq // TOKEN_TILE)
            qs, ks, vs, gate_s, lf_s, *stk_s = _fox_proj_sample(
                token_major + (qg2, kg2, grp), fox_stacked[3:], j, n_fox, n_prompt // TOKEN_TILE)
            fox_stacked = (*stk_p, *stk_s)
            dh = dt.reshape(FOX_HEADS, batch, seq)
            dfirst = dh[:, :, 0::ATTN_Q].transpose(1, 0, 2).reshape(-1)
            dlast = dh[:, :, ATTN_K - 1::ATTN_K].transpose(1, 0, 2).reshape(-1)
            thr = (2.0 * bound + SKIP_MARGIN).reshape(1).astype(F32)
            frozen = (2.0 * LOG2E * bound < FROZEN_MAX_RANGE).reshape(1).astype(jnp.int32)
            y = _fox_attn_prompt(dfirst, dlast, thr, frozen, qt, aqt, kb, ak, vt, gate, batch, seq,
                                 n_prompt + n_sample)
            rt, nc, nt = _sample_bias(cache_lf_t, lf_s, n_new, 0, dec_batch, j)
            per_group = 2 * SAMPLE_PAIRS
            groups = FOX_HEADS // per_group
            y = _fox_attn_sample(
                qs, vs, ks, gate_s, cache_k_t, cache_v_t, rt.reshape(dec_batch, groups, per_group, past),
                nc[:, :, :FOX_HEADS].reshape(dec_batch, n_new, groups, per_group).transpose(0, 2, 1, 3),
                nt.reshape(dec_batch, groups, per_group, n_new), y, n_new, n_prompt, j)
            mix = (y, fox_w_out[j].astype(BF16))
            fl_s.append(lf_s[:, :FOX_HEADS].reshape(dec_batch, n_new, FOX_HEADS))
        else:
            q, k, v, sg = _ret_proj(x, g[1], ret_w_in[j].astype(BF16), tables_p, tables_s, n_prompt)
            y, st_p = _ret_prompt(q, k, v, sg, batch, seq)
            y, st_s = _ret_sample(q, k, v, sg, state_ret, y, ret_stacked, n_new, n_prompt, j)
            ret_stacked = (st_s,)
            mix = (y, ret_w_out[j].astype(BF16))
            rs_p.append(st_p)
        x = _ffn((x,), g[2], w_in_bf, w_out_bf, (layer, 1), n_prompt, n_sample,
                 mix=mix, final_gain=fgain if layer == depth - 1 else None)
        xs = (x,)
    y_prompt, y_sample = x
    kft_p, vft_p, lft_p, kf_s, vf_s = fox_stacked
    per_head = lambda a: a.reshape(n_fox, batch, FOX_HEADS, FOX_HEAD_DIM, seq).transpose(0, 1, 4, 2, 3)
    heads_s = (n_fox, dec_batch, n_new, FOX_HEADS, FOX_HEAD_DIM)
    return (y_prompt.reshape(batch, seq, D_MODEL), y_sample.reshape(dec_batch, n_new, D_MODEL),
            per_head(kft_p), per_head(vft_p), lft_p.transpose(0, 1, 3, 2), jnp.stack(rs_p),
            kf_s.reshape(heads_s), vf_s.reshape(heads_s), jnp.stack(fl_s), ret_stacked[0])
```
